```python
import functools
import math
import jax, jax.numpy as jnp
from jax import lax
import numpy as np

D_MODEL = 2048
BATCH = 2
SEQ = 4096
DEPTH = 1
DEC_BATCH = 8
DEC_SEQ = 4
PAST_LEN = 16384
PAGE_SIZE = 128

D_SSM = D_MODEL // 2
SSM_GROUP = 16
N_GROUPS = D_SSM // SSM_GROUP
SSM_STATE = 64
HEAD_DIM = 128
N_HEADS = (D_MODEL // 2) // HEAD_DIM
D_ATT = N_HEADS * HEAD_DIM
D_FF = ((8 * D_MODEL + 3 * 256 - 1) // (3 * 256)) * 256
Q_BLOCK = 128
N_IN = D_SSM + 3 * D_ATT + N_HEADS + 2 * D_MODEL
SPLIT_POINTS = [D_SSM, D_SSM + D_ATT, D_SSM + 2 * D_ATT, D_SSM + 3 * D_ATT,
                D_SSM + 3 * D_ATT + N_HEADS, D_SSM + 3 * D_ATT + N_HEADS + D_MODEL]
RMS_EPS = 1e-6
DT_MIN = 1e-3
DT_MAX = 1e-1
NEG_INF = -1e30
FGATE_BIAS_LO = 2.0
FGATE_BIAS_HI = 12.0

kernel_name = 'hybrid_s5_fox_gated_decode_step'


def rms_norm(x, g):
    xf = x.astype(jnp.float32)
    y = xf * lax.rsqrt(jnp.mean(xf * xf, axis=-1, keepdims=True) + RMS_EPS)
    return (y * g.astype(jnp.float32)).astype(x.dtype)


def project_inputs(x, norm_g, w_in, b_f, q_g, k_g):
    b, l = x.shape[0], x.shape[1]
    z = rms_norm(x, norm_g) @ w_in
    u, q, k, v, f_logit, gate_ssm, gate_att = jnp.split(z, SPLIT_POINTS, axis=-1)
    q = rms_norm(q.reshape(b, l, N_HEADS, HEAD_DIM), q_g)
    k = rms_norm(k.reshape(b, l, N_HEADS, HEAD_DIM), k_g)
    v = v.reshape(b, l, N_HEADS, HEAD_DIM)
    logf = jax.nn.log_sigmoid((f_logit + b_f).astype(jnp.float32))
    return u, q, k, v, logf, gate_ssm, gate_att


def s5_discretise(a_re, a_im, log_dt, b_re, b_im):
    a_re = a_re.astype(jnp.float32)
    a_im = a_im.astype(jnp.float32)
    dt = jnp.exp(log_dt.astype(jnp.float32))[:, None]
    mag = jnp.exp(dt * a_re)
    ang = dt * a_im
    abar_re = mag * jnp.cos(ang)
    abar_im = mag * jnp.sin(ang)
    e_re = abar_re - 1.0
    e_im = abar_im
    inv_den = 1.0 / (a_re * a_re + a_im * a_im)
    f_re = (e_re * a_re + e_im * a_im) * inv_den
    f_im = (e_im * a_re - e_re * a_im) * inv_den
    b_re = b_re.astype(jnp.float32)
    b_im = b_im.astype(jnp.float32)
    bbar_re = f_re[..., None] * b_re - f_im[..., None] * b_im
    bbar_im = f_re[..., None] * b_im + f_im[..., None] * b_re
    return abar_re, abar_im, bbar_re, bbar_im


def complex_affine_combine(e1, e2):
    a1r, a1i, b1r, b1i = e1
    a2r, a2i, b2r, b2i = e2
    return (a2r * a1r - a2i * a1i,
            a2r * a1i + a2i * a1r,
            a2r * b1r - a2i * b1i + b2r,
            a2r * b1i + a2i * b1r + b2i)


def s5_branch(u, h0_re, h0_im, a_re, a_im, log_dt, b_re, b_im, c_re, c_im, d, w_glu, b_glu):
    b, l = u.shape[0], u.shape[1]
    uf = u.astype(jnp.float32)
    abar_re, abar_im, bbar_re, bbar_im = s5_discretise(a_re, a_im, log_dt, b_re, b_im)
    ug = uf.reshape(b, l, N_GROUPS, SSM_GROUP)
    bu_re = jnp.einsum('blgc,gpc->blgp', ug, bbar_re)
    bu_im = jnp.einsum('blgc,gpc->blgp', ug, bbar_im)
    a_seq_re = jnp.broadcast_to(abar_re, (1, l, N_GROUPS, SSM_STATE))
    a_seq_im = jnp.broadcast_to(abar_im, (1, l, N_GROUPS, SSM_STATE))
    acum_re, acum_im, bcum_re, bcum_im = lax.associative_scan(
        complex_affine_combine, (a_seq_re, a_seq_im, bu_re, bu_im), axis=1)
    h0r = h0_re.astype(jnp.float32)[:, None]
    h0i = h0_im.astype(jnp.float32)[:, None]
    h_re = acum_re * h0r - acum_im * h0i + bcum_re
    h_im = acum_re * h0i + acum_im * h0r + bcum_im
    y = (jnp.einsum('blgp,gcp->blgc', h_re, c_re.astype(jnp.float32))
         - jnp.einsum('blgp,gcp->blgc', h_im, c_im.astype(jnp.float32)))
    y = y.reshape(b, l, D_SSM) + d.astype(jnp.float32) * uf
    y = jax.nn.gelu(y, approximate=False)
    y = y * jax.nn.sigmoid(y @ w_glu.astype(jnp.float32) + b_glu.astype(jnp.float32))
    return y.astype(u.dtype), h_re[:, -1], h_im[:, -1]


def fox_prompt(q, k, v, logf):
    b, l = q.shape[0], q.shape[1]
    n_blocks = l // Q_BLOCK
    scale = HEAD_DIM ** -0.5
    c_t = lax.cumsum(logf, axis=1).transpose(0, 2, 1)
    q_blocks = q.reshape(b, n_blocks, Q_BLOCK, N_HEADS, HEAD_DIM).swapaxes(0, 1)
    c_blocks = c_t.reshape(b, N_HEADS, n_blocks, Q_BLOCK).transpose(2, 0, 1, 3)
    k_pos = jnp.arange(l)

    def one_block(args):
        i, qb, cb = args
        q_pos = i * Q_BLOCK + jnp.arange(Q_BLOCK)
        s = jnp.einsum('bqhd,bkhd->bhqk', qb, k).astype(jnp.float32) * scale
        s = s + cb[..., :, None] - c_t[:, :, None, :]
        s = jnp.where(k_pos[None, :] <= q_pos[:, None], s, NEG_INF)
        p = jax.nn.softmax(s, axis=-1)
        return jnp.einsum('bhqk,bkhd->bqhd', p.astype(v.dtype), v)

    out = lax.map(one_block, (jnp.arange(n_blocks), q_blocks, c_blocks))
    return out.swapaxes(0, 1).reshape(b, l, D_ATT)


def fox_sample(q, k, v, logf, k_past, v_past, logf_past):
    b, t = q.shape[0], q.shape[1]
    n_past = k_past.shape[1]
    scale = HEAD_DIM ** -0.5
    lp = logf_past.astype(jnp.float32)
    r_past = (lax.cumsum(lp, axis=1, reverse=True) - lp).transpose(0, 2, 1)
    c_new = lax.cumsum(logf, axis=1).transpose(0, 2, 1)
    s_past = (jnp.einsum('bqhd,bkhd->bhqk', q, k_past).astype(jnp.float32) * scale
              + r_past[:, :, None, :] + c_new[:, :, :, None])
    s_new = (jnp.einsum('bqhd,bkhd->bhqk', q, k).astype(jnp.float32) * scale
             + c_new[:, :, :, None] - c_new[:, :, None, :])
    causal = jnp.tril(jnp.ones((t, t), dtype=bool))
    s_new = jnp.where(causal, s_new, NEG_INF)
    p = jax.nn.softmax(jnp.concatenate([s_past, s_new], axis=-1), axis=-1)
    out = (jnp.einsum('bhqk,bkhd->bqhd', p[..., :n_past].astype(v.dtype), v_past)
           + jnp.einsum('bhqk,bkhd->bqhd', p[..., n_past:].astype(v.dtype), v))
    return out.reshape(b, t, D_ATT)


def trunk_layer(x, h0_re, h0_im, attn_fn, norm_mix_g, w_in, b_f, q_norm_g, k_norm_g,
                ssm_a_re, ssm_a_im, ssm_log_dt, ssm_b_re, ssm_b_im, ssm_c_re, ssm_c_im, ssm_d,
                w_glu, b_glu, w_br_ssm, w_br_att, w_out, norm_ffn_g, w_ffn_gate, w_ffn_up, w_ffn_down):
    u, q, k, v, logf, gate_ssm, gate_att = project_inputs(x, norm_mix_g, w_in, b_f, q_norm_g, k_norm_g)
    y_ssm, h_re, h_im = s5_branch(u, h0_re, h0_im, ssm_a_re, ssm_a_im, ssm_log_dt, ssm_b_re, ssm_b_im,
                                  ssm_c_re, ssm_c_im, ssm_d, w_glu, b_glu)
    y_att = attn_fn(q, k, v, logf)
    merged = (jax.nn.sigmoid(gate_ssm) * (y_ssm @ w_br_ssm)
              + jax.nn.sigmoid(gate_att) * (y_att @ w_br_att))
    x = x + merged @ w_out
    xn = rms_norm(x, norm_ffn_g)
    x = x + (jax.nn.silu(xn @ w_ffn_gate) * (xn @ w_ffn_up)) @ w_ffn_down
    return x, k, v, logf, h_re, h_im


def setup_inputs(seed: int = 0) -> dict:
    key = jax.random.key(seed)
    ks = jax.random.split(key, 32)
    f32 = jnp.float32

    def nrm(k, shape, scale):
        return jax.random.normal(k, shape, f32) * scale

    n_pages = PAST_LEN // PAGE_SIZE
    n_used = DEC_BATCH * n_pages
    n_pool = n_used + max(1, n_used // 4)
    perm = jax.random.permutation(ks[0], n_pool)
    page_table = perm[:n_used].reshape(DEC_BATCH, n_pages).astype(jnp.int32)

    head_bias = jnp.linspace(FGATE_BIAS_LO, FGATE_BIAS_HI, N_HEADS, dtype=f32)
    x_prompt = nrm(ks[1], (BATCH, SEQ, D_MODEL), 1.0)
    x_sample = nrm(ks[2], (DEC_BATCH, DEC_SEQ, D_MODEL), 1.0)
    cache_k = nrm(ks[3], (DEPTH, n_pool, PAGE_SIZE, N_HEADS, HEAD_DIM), 1.0)
    cache_v = nrm(ks[4], (DEPTH, n_pool, PAGE_SIZE, N_HEADS, HEAD_DIM), 1.0)
    cache_logf = jax.nn.log_sigmoid(head_bias + nrm(ks[5], (DEPTH, n_pool, PAGE_SIZE, N_HEADS), 0.5))
    state_ssm_re = nrm(ks[6], (DEPTH, DEC_BATCH, N_GROUPS, SSM_STATE), 0.3)
    state_ssm_im = nrm(ks[7], (DEPTH, DEC_BATCH, N_GROUPS, SSM_STATE), 0.3)

    norm_mix_g = 1.0 + nrm(ks[8], (DEPTH, D_MODEL), 0.02)
    w_in = nrm(ks[9], (DEPTH, D_MODEL, N_IN), D_MODEL ** -0.5)
    b_f = head_bias[None, :] + nrm(ks[10], (DEPTH, N_HEADS), 0.1)
    q_norm_g = 1.0 + nrm(ks[11], (DEPTH, HEAD_DIM), 0.02)
    k_norm_g = 1.0 + nrm(ks[12], (DEPTH, HEAD_DIM), 0.02)
    ssm_a_re = -0.5 * jnp.exp(nrm(ks[13], (DEPTH, N_GROUPS, SSM_STATE), 0.05))
    ssm_a_im = (jnp.pi * jnp.arange(SSM_STATE, dtype=f32))[None, None, :] + nrm(ks[14], (DEPTH, N_GROUPS, SSM_STATE), 0.01)
    ssm_log_dt = jax.random.uniform(ks[15], (DEPTH, N_GROUPS), f32, math.log(DT_MIN), math.log(DT_MAX))
    ssm_b_re = nrm(ks[16], (DEPTH, N_GROUPS, SSM_STATE, SSM_GROUP), (2 * SSM_GROUP) ** -0.5)
    ssm_b_im = nrm(ks[17], (DEPTH, N_GROUPS, SSM_STATE, SSM_GROUP), (2 * SSM_GROUP) ** -0.5)
    ssm_c_re = nrm(ks[18], (DEPTH, N_GROUPS, SSM_GROUP, SSM_STATE), (2 * SSM_STATE) ** -0.5)
    ssm_c_im = nrm(ks[19], (DEPTH, N_GROUPS, SSM_GROUP, SSM_STATE), (2 * SSM_STATE) ** -0.5)
    ssm_d = nrm(ks[20], (DEPTH, D_SSM), 0.5)
    w_glu = nrm(ks[21], (DEPTH, D_SSM, D_SSM), D_SSM ** -0.5)
    b_glu = nrm(ks[22], (DEPTH, D_SSM), 0.01)
    w_br_ssm = nrm(ks[23], (DEPTH, D_SSM, D_MODEL), D_SSM ** -0.5)
    w_br_att = nrm(ks[24], (DEPTH, D_ATT, D_MODEL), D_ATT ** -0.5)
    w_out = nrm(ks[25], (DEPTH, D_MODEL, D_MODEL), D_MODEL ** -0.5)
    norm_ffn_g = 1.0 + nrm(ks[26], (DEPTH, D_MODEL), 0.02)
    w_ffn_gate = nrm(ks[27], (DEPTH, D_MODEL, D_FF), D_MODEL ** -0.5)
    w_ffn_up = nrm(ks[28], (DEPTH, D_MODEL, D_FF), D_MODEL ** -0.5)
    w_ffn_down = nrm(ks[29], (DEPTH, D_FF, D_MODEL), D_FF ** -0.5)
    return {'x_prompt': x_prompt, 'x_sample': x_sample, 'cache_k': cache_k, 'cache_v': cache_v,
            'cache_logf': cache_logf, 'state_ssm_re': state_ssm_re, 'state_ssm_im': state_ssm_im,
            'page_table': page_table, 'norm_mix_g': norm_mix_g, 'w_in': w_in, 'b_f': b_f,
            'q_norm_g': q_norm_g, 'k_norm_g': k_norm_g, 'ssm_a_re': ssm_a_re, 'ssm_a_im': ssm_a_im,
            'ssm_log_dt': ssm_log_dt, 'ssm_b_re': ssm_b_re, 'ssm_b_im': ssm_b_im, 'ssm_c_re': ssm_c_re,
            'ssm_c_im': ssm_c_im, 'ssm_d': ssm_d, 'w_glu': w_glu, 'b_glu': b_glu, 'w_br_ssm': w_br_ssm,
            'w_br_att': w_br_att, 'w_out': w_out, 'norm_ffn_g': norm_ffn_g, 'w_ffn_gate': w_ffn_gate,
            'w_ffn_up': w_ffn_up, 'w_ffn_down': w_ffn_down}


def reference(x_prompt, x_sample, cache_k, cache_v, cache_logf, state_ssm_re, state_ssm_im, page_table,
              norm_mix_g, w_in, b_f, q_norm_g, k_norm_g, ssm_a_re, ssm_a_im, ssm_log_dt, ssm_b_re, ssm_b_im,
              ssm_c_re, ssm_c_im, ssm_d, w_glu, b_glu, w_br_ssm, w_br_att, w_out, norm_ffn_g,
              w_ffn_gate, w_ffn_up, w_ffn_down):
    n_seq, n_pages = page_table.shape
    y_p, y_s = x_prompt, x_sample
    kp_l, vp_l, lfp_l, hrp_l, hip_l = [], [], [], [], []
    ks_l, vs_l, lfs_l, hrs_l, his_l = [], [], [], [], []
    for l in range(DEPTH):
        lp = (norm_mix_g[l], w_in[l], b_f[l], q_norm_g[l], k_norm_g[l], ssm_a_re[l], ssm_a_im[l],
              ssm_log_dt[l], ssm_b_re[l], ssm_b_im[l], ssm_c_re[l], ssm_c_im[l], ssm_d[l], w_glu[l],
              b_glu[l], w_br_ssm[l], w_br_att[l], w_out[l], norm_ffn_g[l], w_ffn_gate[l], w_ffn_up[l],
              w_ffn_down[l])
        h_zero = jnp.zeros((y_p.shape[0], N_GROUPS, SSM_STATE), jnp.float32)
        y_p, kp, vp, lfp, hrp, hip = trunk_layer(y_p, h_zero, h_zero, fox_prompt, *lp)
        k_past = cache_k[l][page_table].reshape(n_seq, n_pages * PAGE_SIZE, N_HEADS, HEAD_DIM)
        v_past = cache_v[l][page_table].reshape(n_seq, n_pages * PAGE_SIZE, N_HEADS, HEAD_DIM)
        lf_past = cache_logf[l][page_table].reshape(n_seq, n_pages * PAGE_SIZE, N_HEADS)
        attn_sample = functools.partial(fox_sample, k_past=k_past, v_past=v_past, logf_past=lf_past)
        y_s, ksn, vsn, lfs, hrs, his = trunk_layer(y_s, state_ssm_re[l], state_ssm_im[l], attn_sample, *lp)
        kp_l.append(kp); vp_l.append(vp); lfp_l.append(lfp); hrp_l.append(hrp); hip_l.append(hip)
        ks_l.append(ksn); vs_l.append(vsn); lfs_l.append(lfs); hrs_l.append(hrs); his_l.append(his)
    new_k_prompt = jnp.stack(kp_l)
    new_v_prompt = jnp.stack(vp_l)
    new_logf_prompt = jnp.stack(lfp_l)
    new_ssm_re_prompt = jnp.stack(hrp_l)
    new_ssm_im_prompt = jnp.stack(hip_l)
    new_k_sample = jnp.stack(ks_l)
    new_v_sample = jnp.stack(vs_l)
    new_logf_sample = jnp.stack(lfs_l)
    new_ssm_re_sample = jnp.stack(hrs_l)
    new_ssm_im_sample = jnp.stack(his_l)
    return (y_p, y_s, new_k_prompt, new_v_prompt, new_logf_prompt, new_ssm_re_prompt, new_ssm_im_prompt,
            new_k_sample, new_v_sample, new_logf_sample, new_ssm_re_sample, new_ssm_im_sample)
```

```python
import functools
import math

import jax
import jax.numpy as jnp
from jax import lax
from jax.experimental import pallas as pl
from jax.experimental.pallas import tpu as pltpu

F32 = jnp.float32
BF16 = jnp.bfloat16
HIGHEST = lax.Precision.HIGHEST

D_MODEL = 2048
D_SSM = D_MODEL // 2
SSM_GROUP = 16
N_GROUPS = D_SSM // SSM_GROUP
SSM_STATE = 64
HEAD_DIM = 128
N_HEADS = (D_MODEL // 2) // HEAD_DIM
D_ATT = N_HEADS * HEAD_DIM
D_FF = ((8 * D_MODEL + 3 * 256 - 1) // (3 * 256)) * 256
PAGE_SIZE = 128
RMS_EPS = 1e-6
NEG_INF = -1e30

LANES = 128
GROUPS_PER_BLOCK = LANES // SSM_GROUP
N_BLOCKS = N_GROUPS // GROUPS_PER_BLOCK
BLOCK_STATE = GROUPS_PER_BLOCK * SSM_STATE
CHUNK = 16
VMEM_LIMIT = 56 * 1024 * 1024

_NT = (((1,), (1,)), ((), ()))


def _params(sem):
    return pltpu.CompilerParams(dimension_semantics=sem, vmem_limit_bytes=VMEM_LIMIT)


def _gelu_exact(x):
    return 0.5 * x * (1.0 + lax.erf(x * math.sqrt(0.5)))


def _sigmoid(x):
    return 1.0 / (1.0 + jnp.exp(-x))


def _log_sigmoid(x):
    return jnp.minimum(x, 0.0) - jnp.log1p(jnp.exp(-jnp.abs(x)))


def _inproj_kernel(x_ref, g_ref, w_ref, wf_ref, bf_ref, qg_ref, kg_ref,
                   u_ref, qb_ref, k_ref, kb_ref, v_ref, vb_ref, gate_ref, logf_ref, xn_sc):
    j = pl.program_id(1)

    @pl.when(j == 0)
    def _():
        x = x_ref[...]
        ms = jnp.mean(x * x, axis=-1, keepdims=True)
        xn_sc[...] = (x * lax.rsqrt(ms + RMS_EPS) * g_ref[...]).astype(BF16)
        f = jnp.dot(xn_sc[...], wf_ref[...], preferred_element_type=F32) + bf_ref[...]
        logf_ref[...] = _log_sigmoid(f)

    z = jnp.dot(xn_sc[...], w_ref[...], preferred_element_type=F32)

    def head_norm(gain):
        outs = []
        for h in range(N_HEADS):
            blk = z[:, h * HEAD_DIM:(h + 1) * HEAD_DIM]
            ms = jnp.mean(blk * blk, axis=-1, keepdims=True)
            outs.append(blk * lax.rsqrt(ms + RMS_EPS) * gain)
        return jnp.concatenate(outs, axis=1)

    @pl.when(j == 0)
    def _():
        u_ref[...] = z

    @pl.when(j == 1)
    def _():
        qn = head_norm(qg_ref[...])
        qb_ref[...] = (qn * (HEAD_DIM ** -0.5)).astype(BF16)

    @pl.when(j == 2)
    def _():
        kn = head_norm(kg_ref[...])
        k_ref[...] = kn
        kb_ref[...] = kn.astype(BF16)

    @pl.when(j == 3)
    def _():
        v_ref[...] = z
        vb_ref[...] = z.astype(BF16)

    @pl.when(j >= 4)
    def _():
        gate_ref[...] = _sigmoid(z).astype(BF16)


def _inproj(x, norm_g, w_main, w_f, b_f, q_g, k_g, tm):
    t = x.shape[0]
    nt = t // tm
    row = lambda i, j: (i, 0)
    const = lambda i, j: (0, 0)
    out_shape = (
        jax.ShapeDtypeStruct((t, D_SSM), F32),
        jax.ShapeDtypeStruct((t, D_ATT), BF16),
        jax.ShapeDtypeStruct((t, D_ATT), F32),
        jax.ShapeDtypeStruct((t, D_ATT), BF16),
        jax.ShapeDtypeStruct((t, D_ATT), F32),
        jax.ShapeDtypeStruct((t, D_ATT), BF16),
        jax.ShapeDtypeStruct((t, 2 * D_MODEL), BF16),
        jax.ShapeDtypeStruct((t, LANES), F32),
    )
    blk = lambda: pl.BlockSpec((tm, D_ATT), row)
    return pl.pallas_call(
        _inproj_kernel,
        out_shape=out_shape,
        grid=(nt, 8),
        in_specs=[
            pl.BlockSpec((tm, D_MODEL), row),
            pl.BlockSpec((1, D_MODEL), const),
            pl.BlockSpec((D_MODEL, D_ATT), lambda i, j: (0, j)),
            pl.BlockSpec((D_MODEL, LANES), const),
            pl.BlockSpec((1, LANES), const),
            pl.BlockSpec((1, HEAD_DIM), const),
            pl.BlockSpec((1, HEAD_DIM), const),
        ],
        out_specs=(blk(), blk(), blk(), blk(), blk(), blk(),
                   pl.BlockSpec((tm, D_ATT), lambda i, j: (i, jnp.maximum(j - 4, 0))),
                   pl.BlockSpec((tm, LANES), row)),
        scratch_shapes=[pltpu.VMEM((tm, D_MODEL), BF16)],
        compiler_params=_params(("arbitrary", "arbitrary")),
        name="inproj",
    )(x, norm_g, w_main, w_f, b_f, q_g, k_g)


CUM_ROWS = 256


def _cumaug_kernel(lf_ref, caq_ref, cak_ref, carry_sc):
    @pl.when(pl.program_id(1) == 0)
    def _():
        carry_sc[...] = jnp.zeros_like(carry_sc)

    lf = lf_ref[...]
    r = lax.broadcasted_iota(jnp.int32, (CUM_ROWS, CUM_ROWS), 0)
    c = lax.broadcasted_iota(jnp.int32, (CUM_ROWS, CUM_ROWS), 1)
    tri = (c <= r).astype(F32)
    cs = jnp.dot(tri, lf, precision=HIGHEST, preferred_element_type=F32) + carry_sc[...]
    carry_sc[...] = cs[CUM_ROWS - 1:CUM_ROWS, :]
    hi = cs.astype(BF16).astype(F32)
    r1 = cs - hi
    mid = r1.astype(BF16).astype(F32)
    lo = (r1 - mid).astype(BF16).astype(F32)
    lane = lax.broadcasted_iota(jnp.int32, (CUM_ROWS, LANES), 1)
    ones = jnp.where(lane < 6, 1.0, 0.0).astype(F32)
    for h in range(N_HEADS):
        a = hi[:, h:h + 1]
        b = mid[:, h:h + 1]
        d = lo[:, h:h + 1]
        aq = jnp.where(lane == 0, a, jnp.where(lane == 1, b, jnp.where(lane == 2, d, ones)))
        ak = jnp.where(lane == 3, -a, jnp.where(lane == 4, -b, jnp.where(lane == 5, -d, ones)))
        caq_ref[:, h * LANES:(h + 1) * LANES] = aq.astype(BF16)
        cak_ref[:, h * LANES:(h + 1) * LANES] = ak.astype(BF16)


def _cumaug(logf, batch, seq):
    nb = seq // CUM_ROWS
    return pl.pallas_call(
        _cumaug_kernel,
        out_shape=(jax.ShapeDtypeStruct((batch * seq, D_ATT), BF16),
                   jax.ShapeDtypeStruct((batch * seq, D_ATT), BF16)),
        grid=(batch, nb),
        in_specs=[pl.BlockSpec((CUM_ROWS, LANES), lambda b, i: (b * nb + i, 0))],
        out_specs=(pl.BlockSpec((CUM_ROWS, D_ATT), lambda b, i: (b * nb + i, 0)),
                   pl.BlockSpec((CUM_ROWS, D_ATT), lambda b, i: (b * nb + i, 0))),
        scratch_shapes=[pltpu.VMEM((1, LANES), F32)],
        compiler_params=_params(("arbitrary", "arbitrary")),
        name="cumaug",
    )(logf)


ATT_TQ = 256


def _attn_kernel(q_ref, cq_ref, k_ref, ck_ref, v_ref, o_ref):
    i = pl.program_id(2)
    tq = ATT_TQ
    q = jnp.concatenate([q_ref[...], cq_ref[...]], axis=1)

    def block(j, carry, masked):
        m, l, acc = carry
        ks = pl.multiple_of(j * tq, tq)
        kk = jnp.concatenate([k_ref[pl.ds(ks, tq), :], ck_ref[pl.ds(ks, tq), :]], axis=1)
        s = lax.dot_general(q, kk, _NT, preferred_element_type=F32)
        if masked:
            rr = lax.broadcasted_iota(jnp.int32, (tq, tq), 0)
            cc = lax.broadcasted_iota(jnp.int32, (tq, tq), 1)
            s = jnp.where(cc <= rr, s, NEG_INF)
        m_new = jnp.maximum(m, jnp.max(s, axis=-1, keepdims=True))
        alpha = jnp.exp(m - m_new)
        p = jnp.exp(s - m_new)
        l = alpha * l + jnp.sum(p, axis=-1, keepdims=True)
        acc = alpha * acc + jnp.dot(p.astype(BF16), v_ref[pl.ds(ks, tq), :],
                                    preferred_element_type=F32)
        return m_new, l, acc

    init = (jnp.full((tq, 1), NEG_INF, F32), jnp.zeros((tq, 1), F32),
            jnp.zeros((tq, HEAD_DIM), F32))
    carry = lax.fori_loop(0, i, lambda j, c: block(j, c, False), init)
    m, l, acc = block(i, carry, True)
    o_ref[...] = (acc / l).astype(BF16)


def _attention(qb, caq, kb, cak, vb, batch, seq):
    nq = seq // ATT_TQ
    qspec = lambda: pl.BlockSpec((ATT_TQ, HEAD_DIM), lambda b, h, i: (b * nq + i, h))
    kspec = lambda: pl.BlockSpec((seq, HEAD_DIM), lambda b, h, i: (b, h))
    return pl.pallas_call(
        _attn_kernel,
        out_shape=jax.ShapeDtypeStruct((batch * seq, D_ATT), BF16),
        grid=(batch, N_HEADS, nq),
        in_specs=[qspec(), qspec(), kspec(), kspec(), kspec()],
        out_specs=qspec(),
        compiler_params=_params(("arbitrary", "arbitrary", "arbitrary")),
        name="fox_prompt",
    )(qb, caq, kb, cak, vb)


def _discretise(a_re, a_im, log_dt):
    dt = jnp.exp(log_dt)
    mag = jnp.exp(dt * a_re)
    ang = dt * a_im
    abr = mag * jnp.cos(ang)
    abi = mag * jnp.sin(ang)
    e_re = abr - 1.0
    e_im = abi
    inv_den = 1.0 / (a_re * a_re + a_im * a_im)
    f_re = (e_re * a_re + e_im * a_im) * inv_den
    f_im = (e_im * a_re - e_re * a_im) * inv_den
    return abr, abi, f_re, f_im


def _powers(abr, abi, n):
    pr, pi = [jnp.ones_like(abr)], [jnp.zeros_like(abi)]
    for _ in range(n):
        r, i = pr[-1], pi[-1]
        pr.append(r * abr - i * abi)
        pi.append(r * abi + i * abr)
    return pr, pi


def _s5gen_kernel(ar_row, ai_row, dt_row, ar_col, ai_col, dt_col, bre_ref, bim_ref, cre_ref, cim_ref,
                  vrev_ref, bsum_ref, m_ref, bmat_ref, cmat_ref, a1_ref, a16_ref):
    ns = BLOCK_STATE
    abr, abi, f_re, f_im = _discretise(ar_row[...], ai_row[...], dt_row[...])
    bre, bim = bre_ref[...], bim_ref[...]
    bbr = f_re * bre - f_im * bim
    bbi = f_re * bim + f_im * bre
    bmat = jnp.concatenate([bbr, bbi], axis=1)
    bmat_ref[...] = bmat
    pr, pi = _powers(abr, abi, CHUNK)
    a1_ref[...] = jnp.concatenate([pr[1], pi[1]], axis=1)
    a16_ref[...] = jnp.concatenate([pr[CHUNK], pi[CHUNK]], axis=1)
    for i in range(CHUNK):
        r, im = pr[CHUNK - 1 - i], pi[CHUNK - 1 - i]
        blk = jnp.concatenate([r * bbr - im * bbi, r * bbi + im * bbr], axis=1)
        bsum_ref[i * LANES:(i + 1) * LANES, :] = blk.astype(BF16)
    cbr, cbi, _, _ = _discretise(ar_col[...], ai_col[...], dt_col[...])
    qr, qi = _powers(cbr, cbi, CHUNK)
    cre, cim = cre_ref[...], cim_ref[...]
    vrev_ref[(CHUNK - 1) * LANES:CHUNK * LANES, 0:LANES] = jnp.zeros((LANES, LANES), BF16)
    for tau in range(CHUNK + 1):
        blk = jnp.concatenate([cre * qr[tau] - cim * qi[tau],
                               -(cre * qi[tau] + cim * qr[tau])], axis=0)
        if tau == 0:
            cmat_ref[...] = blk
        else:
            m_ref[:, (tau - 1) * LANES:tau * LANES] = blk.astype(BF16)
        if tau < CHUNK:
            w = jnp.dot(bmat, blk, precision=HIGHEST, preferred_element_type=F32).astype(BF16)
            k = CHUNK - 1 - tau
            vrev_ref[k * LANES:(k + 1) * LANES, LANES:2 * LANES] = w
            if k >= 1:
                vrev_ref[(k - 1) * LANES:k * LANES, 0:LANES] = w


def _s5gen(ar_row, ai_row, dt_row, ar_col, ai_col, dt_col, bd_bre, bd_bim, bd_cre, bd_cim):
    ns = BLOCK_STATE
    b3 = lambda s1, s2: pl.BlockSpec((None, s1, s2), lambda o: (o, 0, 0))
    out_shape = (
        jax.ShapeDtypeStruct((N_BLOCKS, CHUNK * LANES, 2 * LANES), BF16),
        jax.ShapeDtypeStruct((N_BLOCKS, CHUNK * LANES, 2 * ns), BF16),
        jax.ShapeDtypeStruct((N_BLOCKS, 2 * ns, CHUNK * LANES), BF16),
        jax.ShapeDtypeStruct((N_BLOCKS, LANES, 2 * ns), F32),
        jax.ShapeDtypeStruct((N_BLOCKS, 2 * ns, LANES), F32),
        jax.ShapeDtypeStruct((N_BLOCKS, 1, 2 * ns), F32),
        jax.ShapeDtypeStruct((N_BLOCKS, 1, 2 * ns), F32),
    )
    return pl.pallas_call(
        _s5gen_kernel,
        out_shape=out_shape,
        grid=(N_BLOCKS,),
        in_specs=[b3(1, ns), b3(1, ns), b3(1, ns), b3(ns, 1), b3(ns, 1), b3(ns, 1),
                  b3(LANES, ns), b3(LANES, ns), b3(ns, LANES), b3(ns, LANES)],
        out_specs=(b3(CHUNK * LANES, 2 * LANES), b3(CHUNK * LANES, 2 * ns), b3(2 * ns, CHUNK * LANES),
                   b3(LANES, 2 * ns), b3(2 * ns, LANES), b3(1, 2 * ns), b3(1, 2 * ns)),
        compiler_params=_params(("arbitrary",)),
        name="s5_operators",
    )(ar_row, ai_row, dt_row, ar_col, ai_col, dt_col, bd_bre, bd_bim, bd_cre, bd_cim)


def _s5_prompt_kernel(u_ref, vrev_ref, bsum_ref, m_ref, a16_ref, d_ref,
                      y_ref, hre_ref, him_ref, ucat_sc, s_sc, hin_sc, *, n_chunks):
    ns = BLOCK_STATE
    for i in range(CHUNK):
        ucat_sc[:, i * LANES:(i + 1) * LANES] = u_ref[pl.ds(i, n_chunks, stride=CHUNK), :].astype(BF16)
    s_sc[...] = jnp.dot(ucat_sc[...], bsum_ref[...], preferred_element_type=F32)
    ar = a16_ref[:, 0:ns]
    ai = a16_ref[:, ns:2 * ns]

    def step(k, carry):
        hr, hi = carry
        hin_sc[pl.ds(k, 1), 0:ns] = hr
        hin_sc[pl.ds(k, 1), ns:2 * ns] = hi
        sr = s_sc[pl.ds(k, 1), 0:ns]
        si = s_sc[pl.ds(k, 1), ns:2 * ns]
        return ar * hr - ai * hi + sr, ar * hi + ai * hr + si

    zero = jnp.zeros((1, ns), F32)
    hr, hi = lax.fori_loop(0, n_chunks, step, (zero, zero))
    hre_ref[...] = hr
    him_ref[...] = hi
    hin = hin_sc[...].astype(BF16)
    d = d_ref[...]
    for jp in range(CHUNK // 2):
        j = 2 * jp
        kk = (j + 2) * LANES
        acc = jnp.dot(ucat_sc[:, 0:kk], vrev_ref[(CHUNK - 2 - j) * LANES:, :],
                      preferred_element_type=F32)
        acc = acc + jnp.dot(hin, m_ref[:, j * LANES:(j + 2) * LANES], preferred_element_type=F32)
        for jj in range(2):
            uj = u_ref[pl.ds(j + jj, n_chunks, stride=CHUNK), :]
            val = acc[:, jj * LANES:(jj + 1) * LANES] + d * uj
            y_ref[pl.ds(j + jj, n_chunks, stride=CHUNK), :] = _gelu_exact(val)


def _s5_prompt(u, vrev, bsum, m, a16, d, batch, seq):
    ns = BLOCK_STATE
    n_chunks = seq // CHUNK
    w3 = lambda s1, s2: pl.BlockSpec((None, s1, s2), lambda o, b: (o, 0, 0))
    return pl.pallas_call(
        functools.partial(_s5_prompt_kernel, n_chunks=n_chunks),
        out_shape=(jax.ShapeDtypeStruct((batch * seq, D_SSM), F32),
                   jax.ShapeDtypeStruct((batch, 1, N_GROUPS * SSM_STATE), F32),
                   jax.ShapeDtypeStruct((batch, 1, N_GROUPS * SSM_STATE), F32)),
        grid=(N_BLOCKS, batch),
        in_specs=[pl.BlockSpec((seq, LANES), lambda o, b: (b, o)),
                  w3(CHUNK * LANES, 2 * LANES), w3(CHUNK * LANES, 2 * ns), w3(2 * ns, CHUNK * LANES),
                  w3(1, 2 * ns),
                  pl.BlockSpec((1, LANES), lambda o, b: (0, o))],
        out_specs=(pl.BlockSpec((seq, LANES), lambda o, b: (b, o)),
                   pl.BlockSpec((None, 1, ns), lambda o, b: (b, 0, o)),
                   pl.BlockSpec((None, 1, ns), lambda o, b: (b, 0, o))),
        scratch_shapes=[pltpu.VMEM((n_chunks, CHUNK * LANES), BF16),
                        pltpu.VMEM((n_chunks, 2 * ns), F32),
                        pltpu.VMEM((n_chunks, 2 * ns), F32)],
        compiler_params=_params(("arbitrary", "arbitrary")),
        name="s5_prompt",
    )(u, vrev, bsum, m, a16, d)


def _s5_sample_kernel(u_ref, h0r_ref, h0i_ref, bmat_ref, cmat_ref, a1_ref, d_ref,
                      y_ref, hre_ref, him_ref, *, n_seq, n_tok):
    ns = BLOCK_STATE
    u = u_ref[...]
    bu = jnp.dot(u, bmat_ref[...], precision=HIGHEST, preferred_element_type=F32)
    ar = a1_ref[:, 0:ns]
    ai = a1_ref[:, ns:2 * ns]
    hr, hi = h0r_ref[...], h0i_ref[...]
    hs = []
    for t in range(n_tok):
        br = bu[t * n_seq:(t + 1) * n_seq, 0:ns]
        bi = bu[t * n_seq:(t + 1) * n_seq, ns:2 * ns]
        hr, hi = ar * hr - ai * hi + br, ar * hi + ai * hr + bi
        hs.append(jnp.concatenate([hr, hi], axis=1))
    hcat = jnp.concatenate(hs, axis=0)
    val = jnp.dot(hcat, cmat_ref[...], precision=HIGHEST, preferred_element_type=F32) + d_ref[...] * u
    y_ref[...] = _gelu_exact(val)
    hre_ref[...] = hr
    him_ref[...] = hi


def _s5_sample(u, h0r, h0i, bmat, cmat, a1, d, n_seq, n_tok):
    ns = BLOCK_STATE
    rows = n_seq * n_tok
    w3 = lambda s1, s2: pl.BlockSpec((None, s1, s2), lambda o: (o, 0, 0))
    col = lambda r, c: pl.BlockSpec((r, c), lambda o: (0, o))
    return pl.pallas_call(
        functools.partial(_s5_sample_kernel, n_seq=n_seq, n_tok=n_tok),
        out_shape=(jax.ShapeDtypeStruct((rows, D_SSM), F32),
                   jax.ShapeDtypeStruct((n_seq, N_GROUPS * SSM_STATE), F32),
                   jax.ShapeDtypeStruct((n_seq, N_GROUPS * SSM_STATE), F32)),
        grid=(N_BLOCKS,),
        in_specs=[col(rows, LANES), col(n_seq, ns), col(n_seq, ns),
                  w3(LANES, 2 * ns), w3(2 * ns, LANES), w3(1, 2 * ns), col(1, LANES)],
        out_specs=(col(rows, LANES), col(n_seq, ns), col(n_seq, ns)),
        compiler_params=_params(("arbitrary",)),
        name="s5_sample",
    )(u, h0r, h0i, bmat, cmat, a1, d)


PAGES_PER_STEP = 8


def _decode_kernel(pt_ref, q_ref, lfa_ref, kn_ref, vn_ref, *refs, n_tok, n_steps):
    g = PAGES_PER_STEP
    k_refs = refs[0:g]
    v_refs = refs[g:2 * g]
    lf_refs = refs[2 * g:3 * g]
    o_ref = refs[3 * g]
    qbd_sc, m_sc, l_sc, acc_sc, carry_sc, ccol_sc = refs[3 * g + 1:]
    j = pl.program_id(1)
    rows = n_tok * N_HEADS
    lane_head = lax.broadcasted_iota(jnp.int32, (N_HEADS, D_ATT), 1) // HEAD_DIM
    sub_head = lax.broadcasted_iota(jnp.int32, (N_HEADS, D_ATT), 0)
    own = lane_head == sub_head

    @pl.when(j == 0)
    def _():
        q = q_ref[...]
        for t in range(n_tok):
            qt = jnp.broadcast_to(q[t:t + 1, :], (N_HEADS, D_ATT))
            qbd_sc[t * N_HEADS:(t + 1) * N_HEADS, :] = jnp.where(own, qt, 0.0).astype(BF16)
        m_sc[...] = jnp.full(m_sc.shape, NEG_INF, F32)
        l_sc[...] = jnp.zeros_like(l_sc)
        acc_sc[...] = jnp.zeros_like(acc_sc)
        carry_sc[...] = jnp.zeros_like(carry_sc)
        lfa = lfa_ref[...]
        lane = lax.broadcasted_iota(jnp.int32, (rows, LANES), 1)
        tok = lax.broadcasted_iota(jnp.int32, (rows, LANES), 0) // N_HEADS
        run = jnp.zeros((rows, 1), F32)
        cmat = jnp.zeros((rows, LANES), F32)
        for t in range(n_tok):
            run = run + lfa[:, t:t + 1]
            cmat = jnp.where(lane == t, run, cmat)
        ccol_sc[...] = cmat
        own_c = jnp.sum(jnp.where(lane == tok, cmat, 0.0), axis=-1, keepdims=True)
        ccol_sc[:, LANES - 1:LANES] = own_c

    qbd = qbd_sc[...]
    c_own = ccol_sc[:, LANES - 1:LANES]
    lane8 = lax.broadcasted_iota(jnp.int32, (N_HEADS, LANES), 1)

    def online(s, vblk):
        m = m_sc[...]
        m_new = jnp.maximum(m, jnp.max(s, axis=-1, keepdims=True))
        alpha = jnp.exp(m - m_new)
        p = jnp.exp(s - m_new)
        l_sc[...] = alpha * l_sc[...] + jnp.sum(p, axis=-1, keepdims=True)
        acc_sc[...] = alpha * acc_sc[...] + jnp.dot(p.astype(BF16), vblk, preferred_element_type=F32)
        m_sc[...] = m_new

    for pg in range(g):
        lf = lf_refs[pg][...]
        x = lf
        sh = 1
        while sh < LANES:
            x = x + jnp.where(lane8 + sh < LANES, pltpu.roll(x, LANES - sh, axis=1), 0.0)
            sh *= 2
        r = x - lf + carry_sc[...]
        carry_sc[...] = carry_sc[...] + x[:, 0:1]
        s = lax.dot_general(qbd, k_refs[pg][...].astype(BF16), _NT, preferred_element_type=F32)
        s = s + jnp.concatenate([r] * n_tok, axis=0) + c_own
        online(s, v_refs[pg][...].astype(BF16))

    @pl.when(j == n_steps - 1)
    def _():
        lane = lax.broadcasted_iota(jnp.int32, (rows, LANES), 1)
        tok = lax.broadcasted_iota(jnp.int32, (rows, LANES), 0) // N_HEADS
        s = lax.dot_general(qbd, kn_ref[...], _NT, preferred_element_type=F32)
        s = s + c_own - ccol_sc[...]
        s = jnp.where(lane <= tok, s, NEG_INF)
        online(s, vn_ref[...])
        out = acc_sc[...] / l_sc[...]
        picked = []
        for t in range(n_tok):
            blk = out[t * N_HEADS:(t + 1) * N_HEADS, :]
            picked.append(jnp.sum(jnp.where(own, blk, 0.0), axis=0, keepdims=True))
        o_ref[...] = jnp.concatenate(picked, axis=0)


def _decode_attention(page_table, qb, lfa, kn, vn, cache_k, cache_v, cache_lft, n_tok):
    n_seq, n_pages = page_table.shape
    g = PAGES_PER_STEP
    n_steps = n_pages // g
    rows = n_tok * N_HEADS

    def page(pg):
        return lambda b, j, pt: (pt[b * n_pages + (n_pages - 1 - (j * g + pg))], 0, 0)

    seq3 = lambda s1, s2: pl.BlockSpec((None, s1, s2), lambda b, j, pt: (b, 0, 0))
    in_specs = [seq3(n_tok, D_ATT), seq3(rows, LANES), seq3(PAGE_SIZE, D_ATT), seq3(PAGE_SIZE, D_ATT)]
    in_specs += [pl.BlockSpec((None, PAGE_SIZE, D_ATT), page(pg)) for pg in range(g)]
    in_specs += [pl.BlockSpec((None, PAGE_SIZE, D_ATT), page(pg)) for pg in range(g)]
    in_specs += [pl.BlockSpec((None, N_HEADS, PAGE_SIZE), page(pg)) for pg in range(g)]
    grid_spec = pltpu.PrefetchScalarGridSpec(
        num_scalar_prefetch=1,
        grid=(n_seq, n_steps),
        in_specs=in_specs,
        out_specs=pl.BlockSpec((None, n_tok, D_ATT), lambda b, j, pt: (b, 0, 0)),
        scratch_shapes=[pltpu.VMEM((rows, D_ATT), BF16),
                        pltpu.VMEM((rows, 1), F32),
                        pltpu.VMEM((rows, 1), F32),
                        pltpu.VMEM((rows, D_ATT), F32),
                        pltpu.VMEM((N_HEADS, 1), F32),
                        pltpu.VMEM((rows, LANES), F32)],
    )
    return pl.pallas_call(
        functools.partial(_decode_kernel, n_tok=n_tok, n_steps=n_steps),
        out_shape=jax.ShapeDtypeStruct((n_seq, n_tok, D_ATT), F32),
        grid_spec=grid_spec,
        compiler_params=_params(("arbitrary", "arbitrary")),
        name="fox_sample",
    )(page_table.reshape(-1), qb, lfa, kn, vn,
      *([cache_k] * g), *([cache_v] * g), *([cache_lft] * g))


def _merge_kernel(y_ref, ya_ref, g_ref, x_ref, wglu_ref, bglu_ref, wbs_ref, wba_ref, wo_ref, ng_ref,
                  x1_ref, xn_ref):
    y = y_ref[...]
    gl = jnp.dot(y.astype(BF16), wglu_ref[...], preferred_element_type=F32) + bglu_ref[...]
    ys = (y * _sigmoid(gl)).astype(BF16)
    ms = jnp.dot(ys, wbs_ref[...], preferred_element_type=F32)
    ma = jnp.dot(ya_ref[...], wba_ref[...], preferred_element_type=F32)
    merged = (g_ref[:, 0:D_MODEL].astype(F32) * ms + g_ref[:, D_MODEL:2 * D_MODEL].astype(F32) * ma)
    x1 = x_ref[...] + jnp.dot(merged.astype(BF16), wo_ref[...], preferred_element_type=F32)
    x1_ref[...] = x1
    ms1 = jnp.mean(x1 * x1, axis=-1, keepdims=True)
    xn_ref[...] = (x1 * lax.rsqrt(ms1 + RMS_EPS) * ng_ref[...]).astype(BF16)


def _merge(y, ya, gates, x, w_glu, b_glu, w_bs, w_ba, w_o, n_g, tm):
    t = x.shape[0]
    row = lambda c: pl.BlockSpec((tm, c), lambda i: (i, 0))
    const = lambda r, c: pl.BlockSpec((r, c), lambda i: (0, 0), pipeline_mode=pl.Buffered(1))
    return pl.pallas_call(
        _merge_kernel,
        out_shape=(jax.ShapeDtypeStruct((t, D_MODEL), F32), jax.ShapeDtypeStruct((t, D_MODEL), BF16)),
        grid=(t // tm,),
        in_specs=[row(D_SSM), row(D_ATT), row(2 * D_MODEL), row(D_MODEL),
                  const(D_SSM, D_SSM), const(1, D_SSM), const(D_SSM, D_MODEL), const(D_ATT, D_MODEL),
                  const(D_MODEL, D_MODEL), const(1, D_MODEL)],
        out_specs=(row(D_MODEL), row(D_MODEL)),
        compiler_params=_params(("arbitrary",)),
        name="merge_out",
    )(y, ya, gates, x, w_glu, b_glu, w_bs, w_ba, w_o, n_g)


FFN_TF = 512


def _ffn_kernel(x1_ref, xn_ref, wg_ref, wu_ref, wd_ref, o_ref, acc_sc):
    f = pl.program_id(1)

    @pl.when(f == 0)
    def _():
        acc_sc[...] = x1_ref[...]

    xn = xn_ref[...]
    a = jnp.dot(xn, wg_ref[...], preferred_element_type=F32)
    b = jnp.dot(xn, wu_ref[...], preferred_element_type=F32)
    h = (a * _sigmoid(a) * b).astype(BF16)
    acc_sc[...] += jnp.dot(h, wd_ref[...], preferred_element_type=F32)

    @pl.when(f == pl.num_programs(1) - 1)
    def _():
        o_ref[...] = acc_sc[...]


def _ffn(x1, xn, w_g, w_u, w_d, tm):
    t = x1.shape[0]
    return pl.pallas_call(
        _ffn_kernel,
        out_shape=jax.ShapeDtypeStruct((t, D_MODEL), F32),
        grid=(t // tm, D_FF // FFN_TF),
        in_specs=[pl.BlockSpec((tm, D_MODEL), lambda i, f: (i, 0)),
                  pl.BlockSpec((tm, D_MODEL), lambda i, f: (i, 0)),
                  pl.BlockSpec((D_MODEL, FFN_TF), lambda i, f: (0, f)),
                  pl.BlockSpec((D_MODEL, FFN_TF), lambda i, f: (0, f)),
                  pl.BlockSpec((FFN_TF, D_MODEL), lambda i, f: (f, 0))],
        out_specs=pl.BlockSpec((tm, D_MODEL), lambda i, f: (i, 0)),
        scratch_shapes=[pltpu.VMEM((tm, D_MODEL), F32)],
        compiler_params=_params(("arbitrary", "arbitrary")),
        name="ffn",
    )(x1, xn, w_g, w_u, w_d)


def _block_diag_lanes(p):
    p4 = p.reshape(N_BLOCKS, GROUPS_PER_BLOCK, SSM_STATE, SSM_GROUP)
    eye = jnp.eye(GROUPS_PER_BLOCK, dtype=p.dtype)
    out = jnp.einsum("ohsc,gh->ogchs", p4, eye)
    return out.reshape(N_BLOCKS, LANES, BLOCK_STATE)


def kernel(x_prompt, x_sample, cache_k, cache_v, cache_logf, state_ssm_re, state_ssm_im, page_table,
           norm_mix_g, w_in, b_f, q_norm_g, k_norm_g, ssm_a_re, ssm_a_im, ssm_log_dt, ssm_b_re, ssm_b_im,
           ssm_c_re, ssm_c_im, ssm_d, w_glu, b_glu, w_br_ssm, w_br_att, w_out, norm_ffn_g,
           w_ffn_gate, w_ffn_up, w_ffn_down):
    batch, seq, _ = x_prompt.shape
    n_seq, n_tok, _ = x_sample.shape
    l = 0
    n_qkv = D_SSM + 3 * D_ATT

    w_in_l = w_in[l]
    w_main = jnp.concatenate([w_in_l[:, :n_qkv], w_in_l[:, n_qkv + N_HEADS:]], axis=1).astype(BF16)
    w_f = jnp.pad(w_in_l[:, n_qkv:n_qkv + N_HEADS], ((0, 0), (0, LANES - N_HEADS))).astype(BF16)
    b_f_pad = jnp.pad(b_f[l], (0, LANES - N_HEADS)).reshape(1, LANES)
    norm_g = norm_mix_g[l].reshape(1, D_MODEL)
    q_g = q_norm_g[l].reshape(1, HEAD_DIM)
    k_g = k_norm_g[l].reshape(1, HEAD_DIM)
    ns = BLOCK_STATE
    ar_row = ssm_a_re[l].reshape(N_BLOCKS, 1, ns)
    ai_row = ssm_a_im[l].reshape(N_BLOCKS, 1, ns)
    dt_row = jnp.repeat(ssm_log_dt[l], SSM_STATE).reshape(N_BLOCKS, 1, ns)
    ar_col = ar_row.reshape(N_BLOCKS, ns, 1)
    ai_col = ai_row.reshape(N_BLOCKS, ns, 1)
    dt_col = dt_row.reshape(N_BLOCKS, ns, 1)
    bd_bre = _block_diag_lanes(ssm_b_re[l])
    bd_bim = _block_diag_lanes(ssm_b_im[l])
    bd_cre = _block_diag_lanes(ssm_c_re[l].transpose(0, 2, 1)).transpose(0, 2, 1)
    bd_cim = _block_diag_lanes(ssm_c_im[l].transpose(0, 2, 1)).transpose(0, 2, 1)
    d_row = ssm_d[l].reshape(1, D_SSM)
    w_glu_b = w_glu[l].astype(BF16)
    b_glu_r = b_glu[l].reshape(1, D_SSM)
    w_bs = w_br_ssm[l].astype(BF16)
    w_ba = w_br_att[l].astype(BF16)
    w_o = w_out[l].astype(BF16)
    n_g = norm_ffn_g[l].reshape(1, D_MODEL)
    w_g = w_ffn_gate[l].astype(BF16)
    w_u = w_ffn_up[l].astype(BF16)
    w_d = w_ffn_down[l].astype(BF16)

    vrev, bsum, m_op, bmat, cmat, a1, a16 = _s5gen(ar_row, ai_row, dt_row, ar_col, ai_col, dt_col,
                                                    bd_bre, bd_bim, bd_cre, bd_cim)

    xp = x_prompt.reshape(batch * seq, D_MODEL)
    u, qb, k, kb, v, vb, gates, logf = _inproj(xp, norm_g, w_main, w_f, b_f_pad, q_g, k_g, tm=512)
    caq, cak = _cumaug(logf, batch, seq)
    y_ssm, hre, him = _s5_prompt(u, vrev, bsum, m_op, a16, d_row, batch, seq)
    y_att = _attention(qb, caq, kb, cak, vb, batch, seq)
    x1, xn1 = _merge(y_ssm, y_att, gates, xp, w_glu_b, b_glu_r, w_bs, w_ba, w_o, n_g, tm=256)
    y_p = _ffn(x1, xn1, w_g, w_u, w_d, tm=512).reshape(batch, seq, D_MODEL)

    rows = n_seq * n_tok
    xs = x_sample.transpose(1, 0, 2).reshape(rows, D_MODEL)
    us, qbs, ks, kbs, vs, vbs, gates_s, logf_s = _inproj(xs, norm_g, w_main, w_f, b_f_pad, q_g, k_g, tm=rows)
    ys_ssm, hre_s, him_s = _s5_sample(us, state_ssm_re[l].reshape(n_seq, -1), state_ssm_im[l].reshape(n_seq, -1),
                                      bmat, cmat, a1, d_row, n_seq, n_tok)
    by_seq = lambda a: a.reshape(n_tok, n_seq, -1).transpose(1, 0, 2)
    lf_s = by_seq(logf_s[:, :N_HEADS])
    lfa = jnp.broadcast_to(lf_s.transpose(0, 2, 1)[:, None, :, :], (n_seq, n_tok, N_HEADS, n_tok))
    lfa = jnp.pad(lfa.reshape(n_seq, n_tok * N_HEADS, n_tok), ((0, 0), (0, 0), (0, LANES - n_tok)))
    pad_keys = lambda a: jnp.pad(by_seq(a), ((0, 0), (0, PAGE_SIZE - n_tok), (0, 0)))
    n_pool = cache_k.shape[1]
    ya_s = _decode_attention(page_table, by_seq(qbs).astype(F32), lfa, pad_keys(kbs), pad_keys(vbs),
                             cache_k[l].reshape(n_pool, PAGE_SIZE, D_ATT),
                             cache_v[l].reshape(n_pool, PAGE_SIZE, D_ATT),
                             cache_logf[l].transpose(0, 2, 1), n_tok)
    ya_s = ya_s.transpose(1, 0, 2).reshape(rows, D_ATT).astype(BF16)
    x1s, xn1s = _merge(ys_ssm, ya_s, gates_s, xs, w_glu_b, b_glu_r, w_bs, w_ba, w_o, n_g, tm=rows)
    y_s = _ffn(x1s, xn1s, w_g, w_u, w_d, tm=rows).reshape(n_tok, n_seq, D_MODEL).transpose(1, 0, 2)

    heads = lambda a, b_, t_: a.reshape(1, b_, t_, N_HEADS, HEAD_DIM)
    tok_major = lambda a: a.reshape(n_tok, n_seq, -1).transpose(1, 0, 2)
    return (
        y_p, y_s,
        heads(k, batch, seq), heads(v, batch, seq),
        logf[:, :N_HEADS].reshape(1, batch, seq, N_HEADS),
        hre.reshape(1, batch, N_GROUPS, SSM_STATE), him.reshape(1, batch, N_GROUPS, SSM_STATE),
        heads(tok_major(ks), n_seq, n_tok), heads(tok_major(vs), n_seq, n_tok),
        tok_major(logf_s[:, :N_HEADS]).reshape(1, n_seq, n_tok, N_HEADS),
        hre_s.reshape(1, n_seq, N_GROUPS, SSM_STATE), him_s.reshape(1, n_seq, N_GROUPS, SSM_STATE),
    )
```

```python
import functools
import math

import jax
import jax.numpy as jnp
from jax import lax
from jax.experimental import pallas as pl
from jax.experimental.pallas import tpu as pltpu

F32 = jnp.float32
BF16 = jnp.bfloat16
HIGHEST = lax.Precision.HIGHEST

D_MODEL = 2048
D_SSM = D_MODEL // 2
SSM_GROUP = 16
N_GROUPS = D_SSM // SSM_GROUP
SSM_STATE = 64
HEAD_DIM = 128
N_HEADS = (D_MODEL // 2) // HEAD_DIM
D_ATT = N_HEADS * HEAD_DIM
D_FF = ((8 * D_MODEL + 3 * 256 - 1) // (3 * 256)) * 256
PAGE_SIZE = 128
RMS_EPS = 1e-6
NEG_INF = -1e30

LANES = 128
GROUPS_PER_BLOCK = LANES // SSM_GROUP
N_BLOCKS = N_GROUPS // GROUPS_PER_BLOCK
BLOCK_STATE = GROUPS_PER_BLOCK * SSM_STATE
CHUNK = 16
ATT_TQ = 512
ATT_TK = 512
VMEM_LIMIT = 56 * 1024 * 1024

LOG2E = math.log2(math.e)
QK_SCALE_LOG2 = (HEAD_DIM ** -0.5) * LOG2E

_NT = (((1,), (1,)), ((), ()))


def _params(sem):
    return pltpu.CompilerParams(dimension_semantics=sem, vmem_limit_bytes=VMEM_LIMIT)


def _gelu_exact(x):
    return 0.5 * x * (1.0 + lax.erf(x * math.sqrt(0.5)))


def _sigmoid(x):
    return 1.0 / (1.0 + jnp.exp(-x))


def _fold_lanes(x, op):
    acc = x[:, 0:LANES]
    for c in range(1, x.shape[1] // LANES):
        acc = op(acc, x[:, c * LANES:(c + 1) * LANES])
    return acc


def _row_max(x):
    return jnp.max(_fold_lanes(x, jnp.maximum), axis=-1, keepdims=True)


def _row_sum(x):
    return jnp.sum(_fold_lanes(x, jnp.add), axis=-1, keepdims=True)


def _log_sigmoid(x):
    return jnp.minimum(x, 0.0) - jnp.log1p(jnp.exp(-jnp.abs(x)))


def _inproj_kernel(x_ref, g_ref, w_ref, wf_ref, bf_ref, qg_ref, kg_ref,
                   u_ref, qb_ref, k_ref, kb_ref, v_ref, vb_ref, gate_ref, logf_ref, xn_sc, *, v_transposed):
    j = pl.program_id(1)

    @pl.when(j == 0)
    def _():
        x = x_ref[...]
        ms = jnp.mean(x * x, axis=-1, keepdims=True)
        xn_sc[...] = (x * lax.rsqrt(ms + RMS_EPS) * g_ref[...]).astype(BF16)
        f = jnp.dot(xn_sc[...], wf_ref[...], preferred_element_type=F32) + bf_ref[...]
        logf_ref[...] = _log_sigmoid(f)

    z = jnp.dot(xn_sc[...], w_ref[...], preferred_element_type=F32)

    def head_norm(gain):
        outs = []
        for h in range(N_HEADS):
            blk = z[:, h * HEAD_DIM:(h + 1) * HEAD_DIM]
            ms = jnp.mean(blk * blk, axis=-1, keepdims=True)
            outs.append(blk * lax.rsqrt(ms + RMS_EPS) * gain)
        return jnp.concatenate(outs, axis=1)

    @pl.when(j == 0)
    def _():
        u_ref[...] = z

    @pl.when(j == 1)
    def _():
        qn = head_norm(qg_ref[...])
        qb_ref[...] = (qn * QK_SCALE_LOG2).astype(BF16)

    @pl.when(j == 2)
    def _():
        kn = head_norm(kg_ref[...])
        k_ref[...] = kn
        kb_ref[...] = kn.astype(BF16)

    @pl.when(j == 3)
    def _():
        v_ref[...] = z
        if v_transposed:
            zt = z.T
            for c in range(vb_ref.shape[0]):
                vb_ref[c] = zt[:, c * ATT_TK:(c + 1) * ATT_TK].astype(BF16)
        else:
            vb_ref[...] = z.astype(BF16)

    @pl.when(j >= 4)
    def _():
        gate_ref[...] = _sigmoid(z).astype(BF16)


def _inproj(x, norm_g, w_main, w_f, b_f, q_g, k_g, tm, v_transposed):
    t = x.shape[0]
    nt = t // tm
    row = lambda i, j: (i, 0)
    const = lambda i, j: (0, 0)
    if v_transposed:
        vb_shape = jax.ShapeDtypeStruct((t // ATT_TK, D_ATT, ATT_TK), BF16)
        vb_spec = pl.BlockSpec((tm // ATT_TK, D_ATT, ATT_TK), lambda i, j: (i, 0, 0))
    else:
        vb_shape = jax.ShapeDtypeStruct((t, D_ATT), BF16)
        vb_spec = pl.BlockSpec((tm, D_ATT), row)
    out_shape = (
        jax.ShapeDtypeStruct((t, D_SSM), F32),
        jax.ShapeDtypeStruct((t, D_ATT), BF16),
        jax.ShapeDtypeStruct((t, D_ATT), F32),
        jax.ShapeDtypeStruct((t, D_ATT), BF16),
        jax.ShapeDtypeStruct((t, D_ATT), F32),
        vb_shape,
        jax.ShapeDtypeStruct((t, 2 * D_MODEL), BF16),
        jax.ShapeDtypeStruct((t, LANES), F32),
    )
    blk = lambda: pl.BlockSpec((tm, D_ATT), row)
    return pl.pallas_call(
        functools.partial(_inproj_kernel, v_transposed=v_transposed),
        out_shape=out_shape,
        grid=(nt, 8),
        in_specs=[
            pl.BlockSpec((tm, D_MODEL), row),
            pl.BlockSpec((1, D_MODEL), const),
            pl.BlockSpec((D_MODEL, D_ATT), lambda i, j: (0, j)),
            pl.BlockSpec((D_MODEL, LANES), const),
            pl.BlockSpec((1, LANES), const),
            pl.BlockSpec((1, HEAD_DIM), const),
            pl.BlockSpec((1, HEAD_DIM), const),
        ],
        out_specs=(blk(), blk(), blk(), blk(), blk(), vb_spec,
                   pl.BlockSpec((tm, D_ATT), lambda i, j: (i, jnp.maximum(j - 4, 0))),
                   pl.BlockSpec((tm, LANES), row)),
        scratch_shapes=[pltpu.VMEM((tm, D_MODEL), BF16)],
        compiler_params=_params(("arbitrary", "arbitrary")),
        name="inproj",
    )(x, norm_g, w_main, w_f, b_f, q_g, k_g)


CUM_ROWS = 256


def _cumaug_kernel(lf_ref, caq_ref, cak_ref, carry_sc):
    @pl.when(pl.program_id(1) == 0)
    def _():
        carry_sc[...] = jnp.zeros_like(carry_sc)

    lf = lf_ref[...]
    r = lax.broadcasted_iota(jnp.int32, (CUM_ROWS, CUM_ROWS), 0)
    c = lax.broadcasted_iota(jnp.int32, (CUM_ROWS, CUM_ROWS), 1)
    tri = (c <= r).astype(F32)
    cs = jnp.dot(tri, lf, precision=HIGHEST, preferred_element_type=F32) + carry_sc[...]
    carry_sc[...] = cs[CUM_ROWS - 1:CUM_ROWS, :]
    cs2 = cs * LOG2E
    hi = cs2.astype(BF16).astype(F32)
    r1 = cs2 - hi
    mid = r1.astype(BF16).astype(F32)
    lo = (r1 - mid).astype(BF16).astype(F32)
    lane = lax.broadcasted_iota(jnp.int32, (CUM_ROWS, LANES), 1)
    ones = jnp.where(lane < 6, 1.0, 0.0).astype(F32)
    for h in range(N_HEADS):
        a = hi[:, h:h + 1]
        b = mid[:, h:h + 1]
        d = lo[:, h:h + 1]
        aq = jnp.where(lane == 0, a, jnp.where(lane == 1, b, jnp.where(lane == 2, d, ones)))
        ak = jnp.where(lane == 3, -a, jnp.where(lane == 4, -b, jnp.where(lane == 5, -d, ones)))
        caq_ref[:, h * LANES:(h + 1) * LANES] = aq.astype(BF16)
        cak_ref[:, h * LANES:(h + 1) * LANES] = ak.astype(BF16)


def _cumaug(logf, batch, seq):
    nb = seq // CUM_ROWS
    return pl.pallas_call(
        _cumaug_kernel,
        out_shape=(jax.ShapeDtypeStruct((batch * seq, D_ATT), BF16),
                   jax.ShapeDtypeStruct((batch * seq, D_ATT), BF16)),
        grid=(batch, nb),
        in_specs=[pl.BlockSpec((CUM_ROWS, LANES), lambda b, i: (b * nb + i, 0))],
        out_specs=(pl.BlockSpec((CUM_ROWS, D_ATT), lambda b, i: (b * nb + i, 0)),
                   pl.BlockSpec((CUM_ROWS, D_ATT), lambda b, i: (b * nb + i, 0))),
        scratch_shapes=[pltpu.VMEM((1, LANES), F32)],
        compiler_params=_params(("arbitrary", "arbitrary")),
        name="cumaug",
    )(logf)


ACC_ROWS = HEAD_DIM + 16


def _attn_kernel(q_ref, cq_ref, k_ref, ck_ref, vt_ref, o_ref, m_sc, acc_sc, sa_sc, sb_sc):
    i = pl.program_id(2)
    tk = ATT_TK
    m_sc[...] = jnp.full(m_sc.shape, NEG_INF, F32)
    acc_sc[...] = jnp.zeros_like(acc_sc)
    sub = lax.broadcasted_iota(jnp.int32, (ACC_ROWS - HEAD_DIM, tk), 0)
    ones_rows = jnp.where(sub == 0, 1.0, 0.0).astype(BF16)

    def logits(j, s_ref):
        ks = pl.multiple_of(j * tk, tk)
        kk = jnp.concatenate([k_ref[pl.ds(ks, tk), :], ck_ref[pl.ds(ks, tk), :]], axis=1)
        qq = jnp.concatenate([q_ref[...], cq_ref[...]], axis=1)
        s_ref[...] = lax.dot_general(kk, qq, _NT, preferred_element_type=F32)

    def consume(j, s_ref, masked):
        s = s_ref[...]
        if masked:
            key = lax.broadcasted_iota(jnp.int32, s.shape, 0)
            qry = lax.broadcasted_iota(jnp.int32, s.shape, 1)
            s = jnp.where(key <= qry, s, NEG_INF)
        vt = jnp.concatenate([vt_ref[j], ones_rows], axis=0)
        m_old = m_sc[...]
        m_new = jnp.maximum(m_old, jnp.max(s, axis=0, keepdims=True))
        alpha = jnp.exp2(m_old - m_new)
        p = jnp.exp2(s - m_new).astype(BF16)
        acc_sc[...] = alpha * acc_sc[...] + jnp.dot(vt, p, preferred_element_type=F32)
        m_sc[...] = m_new

    logits(0, sa_sc)

    def pair(jj, carry):
        logits(2 * jj + 1, sb_sc)
        consume(2 * jj, sa_sc, False)
        logits(2 * jj + 2, sa_sc)
        consume(2 * jj + 1, sb_sc, False)
        return carry

    lax.fori_loop(0, i // 2, pair, 0)

    @pl.when(i % 2 == 0)
    def _():
        consume(i, sa_sc, True)

    @pl.when(i % 2 == 1)
    def _():
        logits(i, sb_sc)
        consume(i - 1, sa_sc, False)
        consume(i, sb_sc, True)

    acc = acc_sc[...]
    out_t = acc[0:HEAD_DIM, :] / acc[HEAD_DIM:HEAD_DIM + 1, :]
    o_ref[...] = out_t.T.astype(BF16)


def _attention(qb, caq, kb, cak, vt, batch, seq):
    nq = seq // ATT_TQ
    nkb = seq // ATT_TK
    qspec = lambda: pl.BlockSpec((ATT_TQ, HEAD_DIM), lambda b, h, i: (b * nq + i, h))
    kspec = lambda: pl.BlockSpec((seq, HEAD_DIM), lambda b, h, i: (b, h))
    return pl.pallas_call(
        _attn_kernel,
        out_shape=jax.ShapeDtypeStruct((batch * seq, D_ATT), BF16),
        grid=(batch, N_HEADS, nq),
        in_specs=[qspec(), qspec(), kspec(), kspec(),
                  pl.BlockSpec((nkb, HEAD_DIM, ATT_TK), lambda b, h, i: (b, h, 0))],
        out_specs=qspec(),
        scratch_shapes=[pltpu.VMEM((1, ATT_TQ), F32), pltpu.VMEM((ACC_ROWS, ATT_TQ), F32),
                        pltpu.VMEM((ATT_TK, ATT_TQ), F32), pltpu.VMEM((ATT_TK, ATT_TQ), F32)],
        compiler_params=_params(("arbitrary", "arbitrary", "arbitrary")),
        name="fox_prompt",
    )(qb, caq, kb, cak, vt)


def _discretise(a_re, a_im, log_dt):
    dt = jnp.exp(log_dt)
    mag = jnp.exp(dt * a_re)
    ang = dt * a_im
    abr = mag * jnp.cos(ang)
    abi = mag * jnp.sin(ang)
    e_re = abr - 1.0
    e_im = abi
    inv_den = 1.0 / (a_re * a_re + a_im * a_im)
    f_re = (e_re * a_re + e_im * a_im) * inv_den
    f_im = (e_im * a_re - e_re * a_im) * inv_den
    return abr, abi, f_re, f_im


def _powers(abr, abi, n):
    pr, pi = [jnp.ones_like(abr)], [jnp.zeros_like(abi)]
    for _ in range(n):
        r, i = pr[-1], pi[-1]
        pr.append(r * abr - i * abi)
        pi.append(r * abi + i * abr)
    return pr, pi


def _s5gen_kernel(ar_row, ai_row, dt_row, ar_col, ai_col, dt_col, bre_ref, bim_ref, cre_ref, cim_ref,
                  vrev_ref, bsum_ref, m_ref, bmat_ref, cmat_ref, a1_ref, a16_ref):
    ns = BLOCK_STATE
    abr, abi, f_re, f_im = _discretise(ar_row[...], ai_row[...], dt_row[...])
    bre, bim = bre_ref[...], bim_ref[...]
    bbr = f_re * bre - f_im * bim
    bbi = f_re * bim + f_im * bre
    bmat = jnp.concatenate([bbr, bbi], axis=1)
    bmat_ref[...] = bmat
    pr, pi = _powers(abr, abi, CHUNK)
    a1_ref[...] = jnp.concatenate([pr[1], pi[1]], axis=1)
    a16_ref[...] = jnp.concatenate([pr[CHUNK], pi[CHUNK]], axis=1)
    for i in range(CHUNK):
        r, im = pr[CHUNK - 1 - i], pi[CHUNK - 1 - i]
        blk = jnp.concatenate([r * bbr - im * bbi, r * bbi + im * bbr], axis=1)
        bsum_ref[i * LANES:(i + 1) * LANES, :] = blk.astype(BF16)
    cbr, cbi, _, _ = _discretise(ar_col[...], ai_col[...], dt_col[...])
    qr, qi = _powers(cbr, cbi, CHUNK)
    cre, cim = cre_ref[...], cim_ref[...]
    vrev_ref[(CHUNK - 1) * LANES:CHUNK * LANES, 0:LANES] = jnp.zeros((LANES, LANES), BF16)
    bmat_hi = bmat.astype(BF16)
    bmat_lo = (bmat - bmat_hi.astype(F32)).astype(BF16)
    for tau in range(CHUNK + 1):
        blk = jnp.concatenate([cre * qr[tau] - cim * qi[tau],
                               -(cre * qi[tau] + cim * qr[tau])], axis=0)
        if tau == 0:
            cmat_ref[...] = blk
        else:
            m_ref[:, (tau - 1) * LANES:tau * LANES] = blk.astype(BF16)
        if tau < CHUNK:
            blk_hi = blk.astype(BF16)
            blk_lo = (blk - blk_hi.astype(F32)).astype(BF16)
            w = (jnp.dot(bmat_hi, blk_hi, preferred_element_type=F32)
                 + jnp.dot(bmat_hi, blk_lo, preferred_element_type=F32)
                 + jnp.dot(bmat_lo, blk_hi, preferred_element_type=F32)).astype(BF16)
            k = CHUNK - 1 - tau
            vrev_ref[k * LANES:(k + 1) * LANES, LANES:2 * LANES] = w
            if k >= 1:
                vrev_ref[(k - 1) * LANES:k * LANES, 0:LANES] = w


def _s5gen(ar_row, ai_row, dt_row, ar_col, ai_col, dt_col, bd_bre, bd_bim, bd_cre, bd_cim):
    ns = BLOCK_STATE
    b3 = lambda s1, s2: pl.BlockSpec((None, s1, s2), lambda o: (o, 0, 0))
    out_shape = (
        jax.ShapeDtypeStruct((N_BLOCKS, CHUNK * LANES, 2 * LANES), BF16),
        jax.ShapeDtypeStruct((N_BLOCKS, CHUNK * LANES, 2 * ns), BF16),
        jax.ShapeDtypeStruct((N_BLOCKS, 2 * ns, CHUNK * LANES), BF16),
        jax.ShapeDtypeStruct((N_BLOCKS, LANES, 2 * ns), F32),
        jax.ShapeDtypeStruct((N_BLOCKS, 2 * ns, LANES), F32),
        jax.ShapeDtypeStruct((N_BLOCKS, 1, 2 * ns), F32),
        jax.ShapeDtypeStruct((N_BLOCKS, 1, 2 * ns), F32),
    )
    return pl.pallas_call(
        _s5gen_kernel,
        out_shape=out_shape,
        grid=(N_BLOCKS,),
        in_specs=[b3(1, ns), b3(1, ns), b3(1, ns), b3(ns, 1), b3(ns, 1), b3(ns, 1),
                  b3(LANES, ns), b3(LANES, ns), b3(ns, LANES), b3(ns, LANES)],
        out_specs=(b3(CHUNK * LANES, 2 * LANES), b3(CHUNK * LANES, 2 * ns), b3(2 * ns, CHUNK * LANES),
                   b3(LANES, 2 * ns), b3(2 * ns, LANES), b3(1, 2 * ns), b3(1, 2 * ns)),
        compiler_params=_params(("arbitrary",)),
        name="s5_operators",
    )(ar_row, ai_row, dt_row, ar_col, ai_col, dt_col, bd_bre, bd_bim, bd_cre, bd_cim)


def _s5_prompt_kernel(u_ref, vrev_ref, bsum_ref, m_ref, a16_ref, d_ref,
                      y_ref, hre_ref, him_ref, ucat_sc, s_sc, hin_sc, *, n_chunks):
    ns = BLOCK_STATE
    for i in range(CHUNK):
        ucat_sc[:, i * LANES:(i + 1) * LANES] = u_ref[pl.ds(i, n_chunks, stride=CHUNK), :].astype(BF16)
    s_sc[...] = jnp.dot(ucat_sc[...], bsum_ref[...], preferred_element_type=F32)
    ar = a16_ref[:, 0:ns]
    ai = a16_ref[:, ns:2 * ns]

    def step(k, carry):
        hr, hi = carry
        hin_sc[pl.ds(k, 1), 0:ns] = hr
        hin_sc[pl.ds(k, 1), ns:2 * ns] = hi
        sr = s_sc[pl.ds(k, 1), 0:ns]
        si = s_sc[pl.ds(k, 1), ns:2 * ns]
        return ar * hr - ai * hi + sr, ar * hi + ai * hr + si

    zero = jnp.zeros((1, ns), F32)
    hr, hi = lax.fori_loop(0, n_chunks, step, (zero, zero))
    hre_ref[...] = hr
    him_ref[...] = hi
    hin = hin_sc[...].astype(BF16)
    d = d_ref[...]
    for jp in range(CHUNK // 2):
        j = 2 * jp
        kk = (j + 2) * LANES
        acc = jnp.dot(ucat_sc[:, 0:kk], vrev_ref[(CHUNK - 2 - j) * LANES:, :],
                      preferred_element_type=F32)
        acc = acc + jnp.dot(hin, m_ref[:, j * LANES:(j + 2) * LANES], preferred_element_type=F32)
        for jj in range(2):
            uj = u_ref[pl.ds(j + jj, n_chunks, stride=CHUNK), :]
            val = acc[:, jj * LANES:(jj + 1) * LANES] + d * uj
            y_ref[pl.ds(j + jj, n_chunks, stride=CHUNK), :] = _gelu_exact(val)


def _s5_prompt(u, vrev, bsum, m, a16, d, batch, seq):
    ns = BLOCK_STATE
    n_chunks = seq // CHUNK
    w3 = lambda s1, s2: pl.BlockSpec((None, s1, s2), lambda o, b: (o, 0, 0))
    return pl.pallas_call(
        functools.partial(_s5_prompt_kernel, n_chunks=n_chunks),
        out_shape=(jax.ShapeDtypeStruct((batch * seq, D_SSM), F32),
                   jax.ShapeDtypeStruct((batch, 1, N_GROUPS * SSM_STATE), F32),
                   jax.ShapeDtypeStruct((batch, 1, N_GROUPS * SSM_STATE), F32)),
        grid=(N_BLOCKS, batch),
        in_specs=[pl.BlockSpec((seq, LANES), lambda o, b: (b, o)),
                  w3(CHUNK * LANES, 2 * LANES), w3(CHUNK * LANES, 2 * ns), w3(2 * ns, CHUNK * LANES),
                  w3(1, 2 * ns),
                  pl.BlockSpec((1, LANES), lambda o, b: (0, o))],
        out_specs=(pl.BlockSpec((seq, LANES), lambda o, b: (b, o)),
                   pl.BlockSpec((None, 1, ns), lambda o, b: (b, 0, o)),
                   pl.BlockSpec((None, 1, ns), lambda o, b: (b, 0, o))),
        scratch_shapes=[pltpu.VMEM((n_chunks, CHUNK * LANES), BF16),
                        pltpu.VMEM((n_chunks, 2 * ns), F32),
                        pltpu.VMEM((n_chunks, 2 * ns), F32)],
        compiler_params=_params(("arbitrary", "arbitrary")),
        name="s5_prompt",
    )(u, vrev, bsum, m, a16, d)


def _s5_sample_kernel(u_ref, h0r_ref, h0i_ref, bmat_ref, cmat_ref, a1_ref, d_ref,
                      y_ref, hre_ref, him_ref, *, n_seq, n_tok):
    ns = BLOCK_STATE
    u = u_ref[...]
    bu = jnp.dot(u, bmat_ref[...], precision=HIGHEST, preferred_element_type=F32)
    ar = a1_ref[:, 0:ns]
    ai = a1_ref[:, ns:2 * ns]
    hr, hi = h0r_ref[...], h0i_ref[...]
    hs = []
    for t in range(n_tok):
        br = bu[t * n_seq:(t + 1) * n_seq, 0:ns]
        bi = bu[t * n_seq:(t + 1) * n_seq, ns:2 * ns]
        hr, hi = ar * hr - ai * hi + br, ar * hi + ai * hr + bi
        hs.append(jnp.concatenate([hr, hi], axis=1))
    hcat = jnp.concatenate(hs, axis=0)
    val = jnp.dot(hcat, cmat_ref[...], precision=HIGHEST, preferred_element_type=F32) + d_ref[...] * u
    y_ref[...] = _gelu_exact(val)
    hre_ref[...] = hr
    him_ref[...] = hi


def _s5_sample(u, h0r, h0i, bmat, cmat, a1, d, n_seq, n_tok):
    ns = BLOCK_STATE
    rows = n_seq * n_tok
    w3 = lambda s1, s2: pl.BlockSpec((None, s1, s2), lambda o: (o, 0, 0))
    col = lambda r, c: pl.BlockSpec((r, c), lambda o: (0, o))
    return pl.pallas_call(
        functools.partial(_s5_sample_kernel, n_seq=n_seq, n_tok=n_tok),
        out_shape=(jax.ShapeDtypeStruct((rows, D_SSM), F32),
                   jax.ShapeDtypeStruct((n_seq, N_GROUPS * SSM_STATE), F32),
                   jax.ShapeDtypeStruct((n_seq, N_GROUPS * SSM_STATE), F32)),
        grid=(N_BLOCKS,),
        in_specs=[col(rows, LANES), col(n_seq, ns), col(n_seq, ns),
                  w3(LANES, 2 * ns), w3(2 * ns, LANES), w3(1, 2 * ns), col(1, LANES)],
        out_specs=(col(rows, LANES), col(n_seq, ns), col(n_seq, ns)),
        compiler_params=_params(("arbitrary",)),
        name="s5_sample",
    )(u, h0r, h0i, bmat, cmat, a1, d)


PAGES_PER_STEP = 8


PAGE_KEYS = PAGE_SIZE * N_HEADS


def _decode_kernel(pt_ref, q_ref, lfrow_ref, lfcol_ref, kn_ref, vn_ref, *refs, n_tok, n_steps):
    g = PAGES_PER_STEP
    k_refs = refs[0:g]
    v_refs = refs[g:2 * g]
    lf_refs = refs[2 * g:3 * g]
    o_ref = refs[3 * g]
    m_sc, l_sc, acc_sc, carry_sc = refs[3 * g + 1:]
    j = pl.program_id(1)
    rows = n_tok * N_HEADS
    head_mask = N_HEADS - 1
    head_shift = N_HEADS.bit_length() - 1

    @pl.when(j == 0)
    def _():
        m_sc[...] = jnp.full(m_sc.shape, NEG_INF, F32)
        l_sc[...] = jnp.zeros_like(l_sc)
        acc_sc[...] = jnp.zeros_like(acc_sc)
        carry_sc[...] = jnp.zeros_like(carry_sc)

    q = q_ref[...]
    lfcol = lfcol_ref[...]
    run = jnp.zeros((N_HEADS, 1), F32)
    pieces = []
    for t in range(n_tok):
        run = run + lfcol[t * N_HEADS:(t + 1) * N_HEADS, :]
        pieces.append(run)
    c_col = jnp.concatenate(pieces, axis=0)

    def online(s_blocks, v_blocks):
        m = m_sc[...]
        smax = s_blocks[0]
        for s in s_blocks[1:]:
            smax = jnp.maximum(smax, s)
        m_new = jnp.maximum(m, _row_max(smax))
        alpha = jnp.exp2(m - m_new)
        psum = None
        pv = None
        for s, vblk in zip(s_blocks, v_blocks):
            p = jnp.exp2(s - m_new)
            psum = p if psum is None else psum + p
            d = jnp.dot(p.astype(BF16), vblk, preferred_element_type=F32)
            pv = d if pv is None else pv + d
        l_sc[...] = alpha * l_sc[...] + _row_sum(psum)
        acc_sc[...] = alpha * acc_sc[...] + pv
        m_sc[...] = m_new

    col = lax.broadcasted_iota(jnp.int32, (rows, PAGE_KEYS), 1)
    row = lax.broadcasted_iota(jnp.int32, (rows, PAGE_KEYS), 0)
    own = (col & head_mask) == (row & head_mask)

    lf = jnp.concatenate([lf_refs[pg][...] for pg in range(g)], axis=0)
    lane = lax.broadcasted_iota(jnp.int32, (g, PAGE_KEYS), 1)
    suffix = lf
    total = lf
    sh = N_HEADS
    while sh < PAGE_KEYS:
        suffix = suffix + jnp.where(lane + sh < PAGE_KEYS, pltpu.roll(suffix, PAGE_KEYS - sh, axis=1), 0.0)
        total = total + pltpu.roll(total, PAGE_KEYS - sh, axis=1)
        sh *= 2
    carry = carry_sc[...]
    s_blocks = []
    for pg in range(g):
        r = suffix[pg:pg + 1, :] - lf[pg:pg + 1, :] + carry
        carry = carry + total[pg:pg + 1, :]
        kp = k_refs[pg][...].reshape(PAGE_KEYS, HEAD_DIM).astype(BF16)
        s = lax.dot_general(q, kp, _NT, preferred_element_type=F32)
        s_blocks.append(jnp.where(own, s + (r + c_col) * LOG2E, NEG_INF))
    carry_sc[...] = carry
    online(s_blocks, [v_refs[pg][...].reshape(PAGE_KEYS, HEAD_DIM).astype(BF16) for pg in range(g)])

    @pl.when(j == n_steps - 1)
    def _():
        ln = lax.broadcasted_iota(jnp.int32, (1, LANES), 1)
        c_row = lfrow_ref[...]
        sh2 = N_HEADS
        while sh2 < rows:
            c_row = c_row + jnp.where(ln >= sh2, pltpu.roll(c_row, sh2, axis=1), 0.0)
            sh2 *= 2
        cl = lax.broadcasted_iota(jnp.int32, (rows, LANES), 1)
        rw = lax.broadcasted_iota(jnp.int32, (rows, LANES), 0)
        valid = ((cl < rows) & ((cl & head_mask) == (rw & head_mask))
                 & ((cl >> head_shift) <= (rw >> head_shift)))
        s = lax.dot_general(q, kn_ref[...], _NT, preferred_element_type=F32)
        s = jnp.where(valid, s + (c_col - c_row) * LOG2E, NEG_INF)
        online([s], [vn_ref[...]])
        o_ref[...] = acc_sc[...] / l_sc[...]


def _decode_attention(page_table, q, lfrow, lfcol, kn, vn, cache_k, cache_v, cache_lf, n_tok):
    n_seq, n_pages = page_table.shape
    g = PAGES_PER_STEP
    n_steps = n_pages // g
    rows = n_tok * N_HEADS

    def page5(pg):
        return lambda b, j, pt: (0, pt[b * n_pages + (n_pages - 1 - (j * g + pg))], 0, 0, 0)

    def page3(pg):
        return lambda b, j, pt: (pt[b * n_pages + (n_pages - 1 - (j * g + pg))], 0, 0)

    seq3 = lambda s1, s2: pl.BlockSpec((None, s1, s2), lambda b, j, pt: (b, 0, 0))
    kv_block = (None, None, PAGE_SIZE, N_HEADS, HEAD_DIM)
    in_specs = [seq3(rows, HEAD_DIM), seq3(1, LANES), seq3(rows, 1), seq3(LANES, HEAD_DIM), seq3(LANES, HEAD_DIM)]
    in_specs += [pl.BlockSpec(kv_block, page5(pg)) for pg in range(g)]
    in_specs += [pl.BlockSpec(kv_block, page5(pg)) for pg in range(g)]
    in_specs += [pl.BlockSpec((None, 1, PAGE_KEYS), page3(pg)) for pg in range(g)]
    grid_spec = pltpu.PrefetchScalarGridSpec(
        num_scalar_prefetch=1,
        grid=(n_seq, n_steps),
        in_specs=in_specs,
        out_specs=pl.BlockSpec((None, rows, HEAD_DIM), lambda b, j, pt: (b, 0, 0)),
        scratch_shapes=[pltpu.VMEM((rows, 1), F32),
                        pltpu.VMEM((rows, 1), F32),
                        pltpu.VMEM((rows, HEAD_DIM), F32),
                        pltpu.VMEM((1, PAGE_KEYS), F32)],
    )
    return pl.pallas_call(
        functools.partial(_decode_kernel, n_tok=n_tok, n_steps=n_steps),
        out_shape=jax.ShapeDtypeStruct((n_seq, rows, HEAD_DIM), F32),
        grid_spec=grid_spec,
        compiler_params=_params(("arbitrary", "arbitrary")),
        name="fox_sample",
    )(page_table.reshape(-1), q, lfrow, lfcol, kn, vn,
      *([cache_k] * g), *([cache_v] * g), *([cache_lf] * g))


def _merge_kernel(y_ref, ya_ref, g_ref, x_ref, wglu_ref, bglu_ref, wbs_ref, wba_ref, wo_ref, ng_ref,
                  x1_ref, xn_ref):
    y = y_ref[...]
    gl = jnp.dot(y.astype(BF16), wglu_ref[...], preferred_element_type=F32) + bglu_ref[...]
    ys = (y * _sigmoid(gl)).astype(BF16)
    ms = jnp.dot(ys, wbs_ref[...], preferred_element_type=F32)
    ma = jnp.dot(ya_ref[...], wba_ref[...], preferred_element_type=F32)
    merged = (g_ref[:, 0:D_MODEL].astype(F32) * ms + g_ref[:, D_MODEL:2 * D_MODEL].astype(F32) * ma)
    x1 = x_ref[...] + jnp.dot(merged.astype(BF16), wo_ref[...], preferred_element_type=F32)
    x1_ref[...] = x1
    ms1 = jnp.mean(x1 * x1, axis=-1, keepdims=True)
    xn_ref[...] = (x1 * lax.rsqrt(ms1 + RMS_EPS) * ng_ref[...]).astype(BF16)


def _merge(y, ya, gates, x, w_glu, b_glu, w_bs, w_ba, w_o, n_g, tm):
    t = x.shape[0]
    row = lambda c: pl.BlockSpec((tm, c), lambda i: (i, 0))
    const = lambda r, c: pl.BlockSpec((r, c), lambda i: (0, 0), pipeline_mode=pl.Buffered(1))
    return pl.pallas_call(
        _merge_kernel,
        out_shape=(jax.ShapeDtypeStruct((t, D_MODEL), F32), jax.ShapeDtypeStruct((t, D_MODEL), BF16)),
        grid=(t // tm,),
        in_specs=[row(D_SSM), row(D_ATT), row(2 * D_MODEL), row(D_MODEL),
                  const(D_SSM, D_SSM), const(1, D_SSM), const(D_SSM, D_MODEL), const(D_ATT, D_MODEL),
                  const(D_MODEL, D_MODEL), const(1, D_MODEL)],
        out_specs=(row(D_MODEL), row(D_MODEL)),
        compiler_params=_params(("arbitrary",)),
        name="merge_out",
    )(y, ya, gates, x, w_glu, b_glu, w_bs, w_ba, w_o, n_g)


FFN_TF = 512


def _ffn_kernel(x1_ref, xn_ref, wg_ref, wu_ref, wd_ref, o_ref, acc_sc):
    f = pl.program_id(1)

    @pl.when(f == 0)
    def _():
        acc_sc[...] = x1_ref[...]

    xn = xn_ref[...]
    a = jnp.dot(xn, wg_ref[...], preferred_element_type=F32)
    b = jnp.dot(xn, wu_ref[...], preferred_element_type=F32)
    h = (a * _sigmoid(a) * b).astype(BF16)
    acc_sc[...] += jnp.dot(h, wd_ref[...], preferred_element_type=F32)

    @pl.when(f == pl.num_programs(1) - 1)
    def _():
        o_ref[...] = acc_sc[...]


def _ffn(x1, xn, w_g, w_u, w_d, tm):
    t = x1.shape[0]
    return pl.pallas_call(
        _ffn_kernel,
        out_shape=jax.ShapeDtypeStruct((t, D_MODEL), F32),
        grid=(t // tm, D_FF // FFN_TF),
        in_specs=[pl.BlockSpec((tm, D_MODEL), lambda i, f: (i, 0)),
                  pl.BlockSpec((tm, D_MODEL), lambda i, f: (i, 0)),
                  pl.BlockSpec((D_MODEL, FFN_TF), lambda i, f: (0, f)),
                  pl.BlockSpec((D_MODEL, FFN_TF), lambda i, f: (0, f)),
                  pl.BlockSpec((FFN_TF, D_MODEL), lambda i, f: (f, 0))],
        out_specs=pl.BlockSpec((tm, D_MODEL), lambda i, f: (i, 0)),
        scratch_shapes=[pltpu.VMEM((tm, D_MODEL), F32)],
        compiler_params=_params(("arbitrary", "arbitrary")),
        name="ffn",
    )(x1, xn, w_g, w_u, w_d)


def _block_diag_lanes(p):
    p4 = p.reshape(N_BLOCKS, GROUPS_PER_BLOCK, SSM_STATE, SSM_GROUP)
    eye = jnp.eye(GROUPS_PER_BLOCK, dtype=p.dtype)
    out = jnp.einsum("ohsc,gh->ogchs", p4, eye)
    return out.reshape(N_BLOCKS, LANES, BLOCK_STATE)


def kernel(x_prompt, x_sample, cache_k, cache_v, cache_logf, state_ssm_re, state_ssm_im, page_table,
           norm_mix_g, w_in, b_f, q_norm_g, k_norm_g, ssm_a_re, ssm_a_im, ssm_log_dt, ssm_b_re, ssm_b_im,
           ssm_c_re, ssm_c_im, ssm_d, w_glu, b_glu, w_br_ssm, w_br_att, w_out, norm_ffn_g,
           w_ffn_gate, w_ffn_up, w_ffn_down):
    batch, seq, _ = x_prompt.shape
    n_seq, n_tok, _ = x_sample.shape
    l = 0
    n_qkv = D_SSM + 3 * D_ATT

    w_in_l = w_in[l]
    w_main = jnp.concatenate([w_in_l[:, :n_qkv], w_in_l[:, n_qkv + N_HEADS:]], axis=1).astype(BF16)
    w_f = jnp.pad(w_in_l[:, n_qkv:n_qkv + N_HEADS], ((0, 0), (0, LANES - N_HEADS))).astype(BF16)
    b_f_pad = jnp.pad(b_f[l], (0, LANES - N_HEADS)).reshape(1, LANES)
    norm_g = norm_mix_g[l].reshape(1, D_MODEL)
    q_g = q_norm_g[l].reshape(1, HEAD_DIM)
    k_g = k_norm_g[l].reshape(1, HEAD_DIM)
    ns = BLOCK_STATE
    ar_row = ssm_a_re[l].reshape(N_BLOCKS, 1, ns)
    ai_row = ssm_a_im[l].reshape(N_BLOCKS, 1, ns)
    dt_row = jnp.repeat(ssm_log_dt[l], SSM_STATE).reshape(N_BLOCKS, 1, ns)
    ar_col = ar_row.reshape(N_BLOCKS, ns, 1)
    ai_col = ai_row.reshape(N_BLOCKS, ns, 1)
    dt_col = dt_row.reshape(N_BLOCKS, ns, 1)
    bd_bre = _block_diag_lanes(ssm_b_re[l])
    bd_bim = _block_diag_lanes(ssm_b_im[l])
    bd_cre = _block_diag_lanes(ssm_c_re[l].transpose(0, 2, 1)).transpose(0, 2, 1)
    bd_cim = _block_diag_lanes(ssm_c_im[l].transpose(0, 2, 1)).transpose(0, 2, 1)
    d_row = ssm_d[l].reshape(1, D_SSM)
    w_glu_b = w_glu[l].astype(BF16)
    b_glu_r = b_glu[l].reshape(1, D_SSM)
    w_bs = w_br_ssm[l].astype(BF16)
    w_ba = w_br_att[l].astype(BF16)
    w_o = w_out[l].astype(BF16)
    n_g = norm_ffn_g[l].reshape(1, D_MODEL)
    w_g = w_ffn_gate[l].astype(BF16)
    w_u = w_ffn_up[l].astype(BF16)
    w_d = w_ffn_down[l].astype(BF16)

    vrev, bsum, m_op, bmat, cmat, a1, a16 = _s5gen(ar_row, ai_row, dt_row, ar_col, ai_col, dt_col,
                                                    bd_bre, bd_bim, bd_cre, bd_cim)

    xp = x_prompt.reshape(batch * seq, D_MODEL)
    u, qb, k, kb, v, vt, gates, logf = _inproj(xp, norm_g, w_main, w_f, b_f_pad, q_g, k_g, tm=512,
                                               v_transposed=True)
    caq, cak = _cumaug(logf, batch, seq)
    y_ssm, hre, him = _s5_prompt(u, vrev, bsum, m_op, a16, d_row, batch, seq)
    y_att = _attention(qb, caq, kb, cak, vt, batch, seq)
    x1, xn1 = _merge(y_ssm, y_att, gates, xp, w_glu_b, b_glu_r, w_bs, w_ba, w_o, n_g, tm=256)
    y_p = _ffn(x1, xn1, w_g, w_u, w_d, tm=512).reshape(batch, seq, D_MODEL)

    rows = n_seq * n_tok
    xs = x_sample.transpose(1, 0, 2).reshape(rows, D_MODEL)
    us, qbs, ks, kbs, vs, vbs, gates_s, logf_s = _inproj(xs, norm_g, w_main, w_f, b_f_pad, q_g, k_g, tm=rows,
                                                         v_transposed=False)
    ys_ssm, hre_s, him_s = _s5_sample(us, state_ssm_re[l].reshape(n_seq, -1), state_ssm_im[l].reshape(n_seq, -1),
                                      bmat, cmat, a1, d_row, n_seq, n_tok)
    by_seq = lambda a: a.reshape(n_tok, n_seq, -1).transpose(1, 0, 2)
    th = n_tok * N_HEADS
    th_rows = lambda a: by_seq(a).reshape(n_seq, th, HEAD_DIM)
    pad_keys = lambda a: jnp.pad(th_rows(a), ((0, 0), (0, LANES - th), (0, 0)))
    lf_s = by_seq(logf_s[:, :N_HEADS]).reshape(n_seq, th)
    lfrow = jnp.pad(lf_s, ((0, 0), (0, LANES - th))).reshape(n_seq, 1, LANES)
    lfcol = lf_s.reshape(n_seq, th, 1)
    n_pool = cache_k.shape[1]
    ya_s = _decode_attention(page_table, th_rows(qbs), lfrow, lfcol, pad_keys(kbs), pad_keys(vbs),
                             cache_k, cache_v, cache_logf[l].reshape(n_pool, 1, PAGE_KEYS), n_tok)
    ya_s = ya_s.reshape(n_seq, n_tok, D_ATT).transpose(1, 0, 2).reshape(rows, D_ATT).astype(BF16)
    x1s, xn1s = _merge(ys_ssm, ya_s, gates_s, xs, w_glu_b, b_glu_r, w_bs, w_ba, w_o, n_g, tm=rows)
    y_s = _ffn(x1s, xn1s, w_g, w_u, w_d, tm=rows).reshape(n_tok, n_seq, D_MODEL).transpose(1, 0, 2)

    heads = lambda a, b_, t_: a.reshape(1, b_, t_, N_HEADS, HEAD_DIM)
    tok_major = lambda a: a.reshape(n_tok, n_seq, -1).transpose(1, 0, 2)
    return (
        y_p, y_s,
        heads(k, batch, seq), heads(v, batch, seq),
        logf[:, :N_HEADS].reshape(1, batch, seq, N_HEADS),
        hre.reshape(1, batch, N_GROUPS, SSM_STATE), him.reshape(1, batch, N_GROUPS, SSM_STATE),
        heads(tok_major(ks), n_seq, n_tok), heads(tok_major(vs), n_seq, n_tok),
        tok_major(logf_s[:, :N_HEADS]).reshape(1, n_seq, n_tok, N_HEADS),
        hre_s.reshape(1, n_seq, N_GROUPS, SSM_STATE), him_s.reshape(1, n_seq, N_GROUPS, SSM_STATE),
    )
```

```python
import functools
import math

import jax
import jax.numpy as jnp
import numpy as np
from jax import lax
from jax.experimental import pallas as pl
from jax.experimental.pallas import tpu as pltpu

F32 = jnp.float32
BF16 = jnp.bfloat16
HIGHEST = lax.Precision.HIGHEST

D_MODEL = 2048
D_SSM = D_MODEL // 2
SSM_GROUP = 16
N_GROUPS = D_SSM // SSM_GROUP
SSM_STATE = 64
HEAD_DIM = 128
N_HEADS = (D_MODEL // 2) // HEAD_DIM
D_ATT = N_HEADS * HEAD_DIM
D_FF = ((8 * D_MODEL + 3 * 256 - 1) // (3 * 256)) * 256
PAGE_SIZE = 128
RMS_EPS = 1e-6
NEG_INF = -1e30

LANES = 128
GROUPS_PER_BLOCK = LANES // SSM_GROUP
N_BLOCKS = N_GROUPS // GROUPS_PER_BLOCK
BLOCK_STATE = GROUPS_PER_BLOCK * SSM_STATE
CHUNK = 16
ATT_TQ = 512
ATT_TK = 512
VMEM_LIMIT = 56 * 1024 * 1024

LOG2E = math.log2(math.e)
QK_SCALE_LOG2 = (HEAD_DIM ** -0.5) * LOG2E

_NT = (((1,), (1,)), ((), ()))


def _params(sem):
    return pltpu.CompilerParams(dimension_semantics=sem, vmem_limit_bytes=VMEM_LIMIT)


def _gelu_exact(x):
    return 0.5 * x * (1.0 + lax.erf(x * math.sqrt(0.5)))


def _sigmoid(x):
    return 1.0 / (1.0 + jnp.exp(-x))


def _fold_lanes(x, op):
    acc = x[:, 0:LANES]
    for c in range(1, x.shape[1] // LANES):
        acc = op(acc, x[:, c * LANES:(c + 1) * LANES])
    return acc


def _row_max(x):
    return jnp.max(_fold_lanes(x, jnp.maximum), axis=-1, keepdims=True)


def _row_sum(x):
    return jnp.sum(_fold_lanes(x, jnp.add), axis=-1, keepdims=True)


def _log_sigmoid(x):
    return jnp.minimum(x, 0.0) - jnp.log1p(jnp.exp(-jnp.abs(x)))


def _inproj_kernel(x_ref, g_ref, w_ref, wf_ref, bf_ref, qg_ref, kg_ref,
                   u_ref, qb_ref, k_ref, kb_ref, v_ref, vb_ref, gate_ref, logf_ref, xn_sc, *, v_transposed):
    j = pl.program_id(1)

    @pl.when(j == 0)
    def _():
        x = x_ref[...]
        ms = jnp.mean(x * x, axis=-1, keepdims=True)
        xn_sc[...] = (x * lax.rsqrt(ms + RMS_EPS) * g_ref[...]).astype(BF16)
        f = jnp.dot(xn_sc[...], wf_ref[...], preferred_element_type=F32) + bf_ref[...]
        logf_ref[...] = _log_sigmoid(f)

    z = jnp.dot(xn_sc[...], w_ref[...], preferred_element_type=F32)

    def head_norm(gain):
        outs = []
        for h in range(N_HEADS):
            blk = z[:, h * HEAD_DIM:(h + 1) * HEAD_DIM]
            ms = jnp.mean(blk * blk, axis=-1, keepdims=True)
            outs.append(blk * lax.rsqrt(ms + RMS_EPS) * gain)
        return jnp.concatenate(outs, axis=1)

    @pl.when(j == 0)
    def _():
        u_ref[...] = z

    @pl.when(j == 1)
    def _():
        qn = head_norm(qg_ref[...])
        qb_ref[...] = (qn * QK_SCALE_LOG2).astype(BF16)

    @pl.when(j == 2)
    def _():
        kn = head_norm(kg_ref[...])
        k_ref[...] = kn
        kb_ref[...] = kn.astype(BF16)

    @pl.when(j == 3)
    def _():
        v_ref[...] = z
        if v_transposed:
            zt = z.T
            for c in range(vb_ref.shape[0]):
                vb_ref[c] = zt[:, c * ATT_TK:(c + 1) * ATT_TK].astype(BF16)
        else:
            vb_ref[...] = z.astype(BF16)

    @pl.when(j >= 4)
    def _():
        gate_ref[...] = _sigmoid(z).astype(BF16)


def _inproj(x, norm_g, w_main, w_f, b_f, q_g, k_g, tm, v_transposed):
    t = x.shape[0]
    nt = t // tm
    row = lambda i, j: (i, 0)
    const = lambda i, j: (0, 0)
    if v_transposed:
        vb_shape = jax.ShapeDtypeStruct((t // ATT_TK, D_ATT, ATT_TK), BF16)
        vb_spec = pl.BlockSpec((tm // ATT_TK, D_ATT, ATT_TK), lambda i, j: (i, 0, 0))
    else:
        vb_shape = jax.ShapeDtypeStruct((t, D_ATT), BF16)
        vb_spec = pl.BlockSpec((tm, D_ATT), row)
    out_shape = (
        jax.ShapeDtypeStruct((t, D_SSM), F32),
        jax.ShapeDtypeStruct((t, D_ATT), BF16),
        jax.ShapeDtypeStruct((t, D_ATT), F32),
        jax.ShapeDtypeStruct((t, D_ATT), BF16),
        jax.ShapeDtypeStruct((t, D_ATT), F32),
        vb_shape,
        jax.ShapeDtypeStruct((t, 2 * D_MODEL), BF16),
        jax.ShapeDtypeStruct((t, LANES), F32),
    )
    blk = lambda: pl.BlockSpec((tm, D_ATT), row)
    return pl.pallas_call(
        functools.partial(_inproj_kernel, v_transposed=v_transposed),
        out_shape=out_shape,
        grid=(nt, 8),
        in_specs=[
            pl.BlockSpec((tm, D_MODEL), row),
            pl.BlockSpec((1, D_MODEL), const),
            pl.BlockSpec((D_MODEL, D_ATT), lambda i, j: (0, j)),
            pl.BlockSpec((D_MODEL, LANES), const),
            pl.BlockSpec((1, LANES), const),
            pl.BlockSpec((1, HEAD_DIM), const),
            pl.BlockSpec((1, HEAD_DIM), const),
        ],
        out_specs=(blk(), blk(), blk(), blk(), blk(), vb_spec,
                   pl.BlockSpec((tm, D_ATT), lambda i, j: (i, jnp.maximum(j - 4, 0))),
                   pl.BlockSpec((tm, LANES), row)),
        scratch_shapes=[pltpu.VMEM((tm, D_MODEL), BF16)],
        compiler_params=_params(("arbitrary", "arbitrary")),
        name="inproj",
    )(x, norm_g, w_main, w_f, b_f, q_g, k_g)


CUM_ROWS = 256


def _cumaug_kernel(lf_ref, eq_ref, ek_ref, caq_ref, cak_ref, carry_sc):
    @pl.when(pl.program_id(1) == 0)
    def _():
        carry_sc[...] = jnp.zeros_like(carry_sc)

    lf = lf_ref[...]
    r = lax.broadcasted_iota(jnp.int32, (CUM_ROWS, CUM_ROWS), 0)
    c = lax.broadcasted_iota(jnp.int32, (CUM_ROWS, CUM_ROWS), 1)
    tri = (c <= r).astype(BF16)
    lf_hi = lf.astype(BF16)
    lf_r = lf - lf_hi.astype(F32)
    lf_mid = lf_r.astype(BF16)
    lf_lo = (lf_r - lf_mid.astype(F32)).astype(BF16)
    parts = jnp.dot(tri, jnp.concatenate([lf_hi, lf_mid, lf_lo], axis=1), preferred_element_type=F32)
    cs = (parts[:, 0:LANES] + parts[:, LANES:2 * LANES] + parts[:, 2 * LANES:3 * LANES]) + carry_sc[...]
    carry_sc[...] = cs[CUM_ROWS - 1:CUM_ROWS, :]
    cs2 = cs * LOG2E
    hi = cs2.astype(BF16).astype(F32)
    r1 = cs2 - hi
    mid = r1.astype(BF16).astype(F32)
    lo = (r1 - mid).astype(BF16).astype(F32)
    lane = lax.broadcasted_iota(jnp.int32, (CUM_ROWS, LANES), 1)
    packed = jnp.where(lane < N_HEADS, hi,
                       jnp.where(lane < 2 * N_HEADS, pltpu.roll(mid, N_HEADS, axis=1),
                                 jnp.where(lane < 3 * N_HEADS, pltpu.roll(lo, 2 * N_HEADS, axis=1),
                                           jnp.where(lane == 3 * N_HEADS, 1.0, 0.0)))).astype(BF16)
    caq_ref[...] = jnp.dot(packed, eq_ref[...], preferred_element_type=F32).astype(BF16)
    cak_ref[...] = jnp.dot(packed, ek_ref[...], preferred_element_type=F32).astype(BF16)


def _placement_matrices():
    eq = np.zeros((LANES, D_ATT), np.float32)
    ek = np.zeros((LANES, D_ATT), np.float32)
    for h in range(N_HEADS):
        base = h * HEAD_DIM
        for piece in range(3):
            eq[piece * N_HEADS + h, base + piece] = 1.0
            ek[piece * N_HEADS + h, base + 3 + piece] = -1.0
            eq[3 * N_HEADS, base + 3 + piece] = 1.0
            ek[3 * N_HEADS, base + piece] = 1.0
    return jnp.asarray(eq, BF16), jnp.asarray(ek, BF16)


def _cumaug(logf, batch, seq):
    nb = seq // CUM_ROWS
    eq, ek = _placement_matrices()
    const = lambda: pl.BlockSpec((LANES, D_ATT), lambda b, i: (0, 0))
    return pl.pallas_call(
        _cumaug_kernel,
        out_shape=(jax.ShapeDtypeStruct((batch * seq, D_ATT), BF16),
                   jax.ShapeDtypeStruct((batch * seq, D_ATT), BF16)),
        grid=(batch, nb),
        in_specs=[pl.BlockSpec((CUM_ROWS, LANES), lambda b, i: (b * nb + i, 0)), const(), const()],
        out_specs=(pl.BlockSpec((CUM_ROWS, D_ATT), lambda b, i: (b * nb + i, 0)),
                   pl.BlockSpec((CUM_ROWS, D_ATT), lambda b, i: (b * nb + i, 0))),
        scratch_shapes=[pltpu.VMEM((1, LANES), F32)],
        compiler_params=_params(("arbitrary", "arbitrary")),
        name="cumaug",
    )(logf, eq, ek)


ACC_ROWS = HEAD_DIM + 16


def _attn_kernel(q_ref, cq_ref, k_ref, ck_ref, vt_ref, o_ref, m_sc, acc_sc, sa_sc, sb_sc):
    i = pl.program_id(2)
    tk = ATT_TK
    m_sc[...] = jnp.full(m_sc.shape, NEG_INF, F32)
    acc_sc[...] = jnp.zeros_like(acc_sc)
    sub = lax.broadcasted_iota(jnp.int32, (ACC_ROWS - HEAD_DIM, tk), 0)
    ones_rows = jnp.where(sub == 0, 1.0, 0.0).astype(BF16)

    def logits(j, s_ref):
        ks = pl.multiple_of(j * tk, tk)
        kk = jnp.concatenate([k_ref[pl.ds(ks, tk), :], ck_ref[pl.ds(ks, tk), :]], axis=1)
        qq = jnp.concatenate([q_ref[...], cq_ref[...]], axis=1)
        s_ref[...] = lax.dot_general(kk, qq, _NT, preferred_element_type=F32)

    def consume(j, s_ref, masked):
        s = s_ref[...]
        if masked:
            key = lax.broadcasted_iota(jnp.int32, s.shape, 0)
            qry = lax.broadcasted_iota(jnp.int32, s.shape, 1)
            s = jnp.where(key <= qry, s, NEG_INF)
        vt = jnp.concatenate([vt_ref[j], ones_rows], axis=0)
        m_old = m_sc[...]
        m_new = jnp.maximum(m_old, jnp.max(s, axis=0, keepdims=True))
        alpha = jnp.exp2(m_old - m_new)
        p = jnp.exp2(s - m_new).astype(BF16)
        acc_sc[...] = alpha * acc_sc[...] + jnp.dot(vt, p, preferred_element_type=F32)
        m_sc[...] = m_new

    logits(0, sa_sc)

    def pair(jj, carry):
        logits(2 * jj + 1, sb_sc)
        consume(2 * jj, sa_sc, False)
        logits(2 * jj + 2, sa_sc)
        consume(2 * jj + 1, sb_sc, False)
        return carry

    lax.fori_loop(0, i // 2, pair, 0)

    @pl.when(i % 2 == 0)
    def _():
        consume(i, sa_sc, True)

    @pl.when(i % 2 == 1)
    def _():
        logits(i, sb_sc)
        consume(i - 1, sa_sc, False)
        consume(i, sb_sc, True)

    acc = acc_sc[...]
    out_t = acc[0:HEAD_DIM, :] / acc[HEAD_DIM:HEAD_DIM + 1, :]
    o_ref[...] = out_t.T.astype(BF16)


def _attention(qb, caq, kb, cak, vt, batch, seq):
    nq = seq // ATT_TQ
    nkb = seq // ATT_TK
    qspec = lambda: pl.BlockSpec((ATT_TQ, HEAD_DIM), lambda b, h, i: (b * nq + i, h))
    kspec = lambda: pl.BlockSpec((seq, HEAD_DIM), lambda b, h, i: (b, h))
    return pl.pallas_call(
        _attn_kernel,
        out_shape=jax.ShapeDtypeStruct((batch * seq, D_ATT), BF16),
        grid=(batch, N_HEADS, nq),
        in_specs=[qspec(), qspec(), kspec(), kspec(),
                  pl.BlockSpec((nkb, HEAD_DIM, ATT_TK), lambda b, h, i: (b, h, 0))],
        out_specs=qspec(),
        scratch_shapes=[pltpu.VMEM((1, ATT_TQ), F32), pltpu.VMEM((ACC_ROWS, ATT_TQ), F32),
                        pltpu.VMEM((ATT_TK, ATT_TQ), F32), pltpu.VMEM((ATT_TK, ATT_TQ), F32)],
        compiler_params=_params(("arbitrary", "arbitrary", "arbitrary")),
        name="fox_prompt",
    )(qb, caq, kb, cak, vt)


def _discretise(a_re, a_im, log_dt):
    dt = jnp.exp(log_dt)
    mag = jnp.exp(dt * a_re)
    ang = dt * a_im
    abr = mag * jnp.cos(ang)
    abi = mag * jnp.sin(ang)
    e_re = abr - 1.0
    e_im = abi
    inv_den = 1.0 / (a_re * a_re + a_im * a_im)
    f_re = (e_re * a_re + e_im * a_im) * inv_den
    f_im = (e_im * a_re - e_re * a_im) * inv_den
    return abr, abi, f_re, f_im


def _powers(abr, abi, n):
    pr, pi = [jnp.ones_like(abr)], [jnp.zeros_like(abi)]
    for _ in range(n):
        r, i = pr[-1], pi[-1]
        pr.append(r * abr - i * abi)
        pi.append(r * abi + i * abr)
    return pr, pi


def _s5gen_kernel(ar_row, ai_row, dt_row, bre_ref, bim_ref, cre_ref, cim_ref,
                  vrev_ref, bsum_ref, m_ref, bmat_ref, cmat_ref, a1_ref, a16_ref):
    ns = BLOCK_STATE
    abr, abi, f_re, f_im = _discretise(ar_row[...], ai_row[...], dt_row[...])
    bre, bim = bre_ref[...], bim_ref[...]
    bbr = f_re * bre - f_im * bim
    bbi = f_re * bim + f_im * bre
    bmat = jnp.concatenate([bbr, bbi], axis=1)
    bmat_ref[...] = bmat
    pr, pi = _powers(abr, abi, CHUNK)
    a1_ref[...] = jnp.concatenate([pr[1], pi[1]], axis=1)
    a16_ref[...] = jnp.concatenate([pr[CHUNK], pi[CHUNK]], axis=1)
    for i in range(CHUNK):
        r, im = pr[CHUNK - 1 - i], pi[CHUNK - 1 - i]
        blk = jnp.concatenate([r * bbr - im * bbi, r * bbi + im * bbr], axis=1)
        bsum_ref[i * LANES:(i + 1) * LANES, :] = blk.astype(BF16)
    npow = CHUNK + 1
    stacked = jnp.concatenate(pr + pi + [jnp.zeros((LANES - 2 * npow, ns), F32)], axis=0)
    pt = stacked.T
    qr = [pt[:, t:t + 1] for t in range(npow)]
    qi = [pt[:, npow + t:npow + t + 1] for t in range(npow)]
    cre, cim = cre_ref[...], cim_ref[...]
    vrev_ref[(CHUNK - 1) * LANES:CHUNK * LANES, 0:LANES] = jnp.zeros((LANES, LANES), BF16)
    bmat_hi = bmat.astype(BF16)
    for tau in range(CHUNK + 1):
        blk = jnp.concatenate([cre * qr[tau] - cim * qi[tau],
                               -(cre * qi[tau] + cim * qr[tau])], axis=0)
        if tau == 0:
            cmat_ref[...] = blk
        else:
            m_ref[:, (tau - 1) * LANES:tau * LANES] = blk.astype(BF16)
        if tau < CHUNK:
            w = jnp.dot(bmat_hi, blk.astype(BF16), preferred_element_type=F32).astype(BF16)
            k = CHUNK - 1 - tau
            vrev_ref[k * LANES:(k + 1) * LANES, LANES:2 * LANES] = w
            if k >= 1:
                vrev_ref[(k - 1) * LANES:k * LANES, 0:LANES] = w


def _s5gen(ar_row, ai_row, dt_row, bd_bre, bd_bim, bd_cre, bd_cim):
    ns = BLOCK_STATE
    b3 = lambda s1, s2: pl.BlockSpec((None, s1, s2), lambda o: (o, 0, 0))
    out_shape = (
        jax.ShapeDtypeStruct((N_BLOCKS, CHUNK * LANES, 2 * LANES), BF16),
        jax.ShapeDtypeStruct((N_BLOCKS, CHUNK * LANES, 2 * ns), BF16),
        jax.ShapeDtypeStruct((N_BLOCKS, 2 * ns, CHUNK * LANES), BF16),
        jax.ShapeDtypeStruct((N_BLOCKS, LANES, 2 * ns), F32),
        jax.ShapeDtypeStruct((N_BLOCKS, 2 * ns, LANES), F32),
        jax.ShapeDtypeStruct((N_BLOCKS, 1, 2 * ns), F32),
        jax.ShapeDtypeStruct((N_BLOCKS, 1, 2 * ns), F32),
    )
    return pl.pallas_call(
        _s5gen_kernel,
        out_shape=out_shape,
        grid=(N_BLOCKS,),
        in_specs=[b3(1, ns), b3(1, ns), b3(1, ns),
                  b3(LANES, ns), b3(LANES, ns), b3(ns, LANES), b3(ns, LANES)],
        out_specs=(b3(CHUNK * LANES, 2 * LANES), b3(CHUNK * LANES, 2 * ns), b3(2 * ns, CHUNK * LANES),
                   b3(LANES, 2 * ns), b3(2 * ns, LANES), b3(1, 2 * ns), b3(1, 2 * ns)),
        compiler_params=_params(("arbitrary",)),
        name="s5_operators",
    )(ar_row, ai_row, dt_row, bd_bre, bd_bim, bd_cre, bd_cim)


def _s5_prompt_kernel(u_ref, vrev_ref, bsum_ref, m_ref, a16_ref, d_ref,
                      y_ref, hre_ref, him_ref, ucat_sc, s_sc, hin_sc, *, n_chunks):
    ns = BLOCK_STATE
    for i in range(CHUNK):
        ucat_sc[:, i * LANES:(i + 1) * LANES] = u_ref[pl.ds(i, n_chunks, stride=CHUNK), :].astype(BF16)
    s_sc[...] = jnp.dot(ucat_sc[...], bsum_ref[...], preferred_element_type=F32)
    ar = a16_ref[:, 0:ns]
    ai = a16_ref[:, ns:2 * ns]

    def step(k, carry):
        hr, hi = carry
        hin_sc[pl.ds(k, 1), 0:ns] = hr
        hin_sc[pl.ds(k, 1), ns:2 * ns] = hi
        sr = s_sc[pl.ds(k, 1), 0:ns]
        si = s_sc[pl.ds(k, 1), ns:2 * ns]
        return ar * hr - ai * hi + sr, ar * hi + ai * hr + si

    zero = jnp.zeros((1, ns), F32)
    hr, hi = lax.fori_loop(0, n_chunks, step, (zero, zero))
    hre_ref[...] = hr
    him_ref[...] = hi
    hin = hin_sc[...].astype(BF16)
    d = d_ref[...]
    for jp in range(CHUNK // 2):
        j = 2 * jp
        kk = (j + 2) * LANES
        acc = jnp.dot(ucat_sc[:, 0:kk], vrev_ref[(CHUNK - 2 - j) * LANES:, :],
                      preferred_element_type=F32)
        acc = acc + jnp.dot(hin, m_ref[:, j * LANES:(j + 2) * LANES], preferred_element_type=F32)
        for jj in range(2):
            uj = u_ref[pl.ds(j + jj, n_chunks, stride=CHUNK), :]
            val = acc[:, jj * LANES:(jj + 1) * LANES] + d * uj
            y_ref[pl.ds(j + jj, n_chunks, stride=CHUNK), :] = _gelu_exact(val)


def _s5_prompt(u, vrev, bsum, m, a16, d, batch, seq):
    ns = BLOCK_STATE
    n_chunks = seq // CHUNK
    w3 = lambda s1, s2: pl.BlockSpec((None, s1, s2), lambda o, b: (o, 0, 0))
    return pl.pallas_call(
        functools.partial(_s5_prompt_kernel, n_chunks=n_chunks),
        out_shape=(jax.ShapeDtypeStruct((batch * seq, D_SSM), F32),
                   jax.ShapeDtypeStruct((batch, 1, N_GROUPS * SSM_STATE), F32),
                   jax.ShapeDtypeStruct((batch, 1, N_GROUPS * SSM_STATE), F32)),
        grid=(N_BLOCKS, batch),
        in_specs=[pl.BlockSpec((seq, LANES), lambda o, b: (b, o)),
                  w3(CHUNK * LANES, 2 * LANES), w3(CHUNK * LANES, 2 * ns), w3(2 * ns, CHUNK * LANES),
                  w3(1, 2 * ns),
                  pl.BlockSpec((1, LANES), lambda o, b: (0, o))],
        out_specs=(pl.BlockSpec((seq, LANES), lambda o, b: (b, o)),
                   pl.BlockSpec((None, 1, ns), lambda o, b: (b, 0, o)),
                   pl.BlockSpec((None, 1, ns), lambda o, b: (b, 0, o))),
        scratch_shapes=[pltpu.VMEM((n_chunks, CHUNK * LANES), BF16),
                        pltpu.VMEM((n_chunks, 2 * ns), F32),
                        pltpu.VMEM((n_chunks, 2 * ns), F32)],
        compiler_params=_params(("arbitrary", "arbitrary")),
        name="s5_prompt",
    )(u, vrev, bsum, m, a16, d)


def _s5_sample_kernel(u_ref, h0r_ref, h0i_ref, bmat_ref, cmat_ref, a1_ref, d_ref,
                      y_ref, hre_ref, him_ref, *, n_seq, n_tok):
    ns = BLOCK_STATE
    u = u_ref[...]
    bu = jnp.dot(u, bmat_ref[...], precision=HIGHEST, preferred_element_type=F32)
    ar = a1_ref[:, 0:ns]
    ai = a1_ref[:, ns:2 * ns]
    hr, hi = h0r_ref[...], h0i_ref[...]
    hs = []
    for t in range(n_tok):
        br = bu[t * n_seq:(t + 1) * n_seq, 0:ns]
        bi = bu[t * n_seq:(t + 1) * n_seq, ns:2 * ns]
        hr, hi = ar * hr - ai * hi + br, ar * hi + ai * hr + bi
        hs.append(jnp.concatenate([hr, hi], axis=1))
    hcat = jnp.concatenate(hs, axis=0)
    val = jnp.dot(hcat, cmat_ref[...], precision=HIGHEST, preferred_element_type=F32) + d_ref[...] * u
    y_ref[...] = _gelu_exact(val)
    hre_ref[...] = hr
    him_ref[...] = hi


def _s5_sample(u, h0r, h0i, bmat, cmat, a1, d, n_seq, n_tok):
    ns = BLOCK_STATE
    rows = n_seq * n_tok
    w3 = lambda s1, s2: pl.BlockSpec((None, s1, s2), lambda o: (o, 0, 0))
    col = lambda r, c: pl.BlockSpec((r, c), lambda o: (0, o))
    return pl.pallas_call(
        functools.partial(_s5_sample_kernel, n_seq=n_seq, n_tok=n_tok),
        out_shape=(jax.ShapeDtypeStruct((rows, D_SSM), F32),
                   jax.ShapeDtypeStruct((n_seq, N_GROUPS * SSM_STATE), F32),
                   jax.ShapeDtypeStruct((n_seq, N_GROUPS * SSM_STATE), F32)),
        grid=(N_BLOCKS,),
        in_specs=[col(rows, LANES), col(n_seq, ns), col(n_seq, ns),
                  w3(LANES, 2 * ns), w3(2 * ns, LANES), w3(1, 2 * ns), col(1, LANES)],
        out_specs=(col(rows, LANES), col(n_seq, ns), col(n_seq, ns)),
        compiler_params=_params(("arbitrary",)),
        name="s5_sample",
    )(u, h0r, h0i, bmat, cmat, a1, d)


PAGES_PER_STEP = 16
PAGE_GROUP = 4

PAGE_KEYS = PAGE_SIZE * N_HEADS


def _decode_kernel(pt_ref, q_ref, lfrow_ref, lfcol_ref, kn_ref, vn_ref, *refs, n_tok, n_steps):
    g = PAGES_PER_STEP
    k_refs = refs[0:g]
    v_refs = refs[g:2 * g]
    lf_refs = refs[2 * g:3 * g]
    o_ref = refs[3 * g]
    m_sc, l_sc, acc_sc, carry_sc = refs[3 * g + 1:]
    j = pl.program_id(1)
    rows = n_tok * N_HEADS
    head_mask = N_HEADS - 1
    head_shift = N_HEADS.bit_length() - 1

    @pl.when(j == 0)
    def _():
        m_sc[...] = jnp.full(m_sc.shape, NEG_INF, F32)
        l_sc[...] = jnp.zeros_like(l_sc)
        acc_sc[...] = jnp.zeros_like(acc_sc)
        carry_sc[...] = jnp.zeros_like(carry_sc)

    q = q_ref[...]
    lfcol = lfcol_ref[...]
    run = jnp.zeros((N_HEADS, 1), F32)
    pieces = []
    for t in range(n_tok):
        run = run + lfcol[t * N_HEADS:(t + 1) * N_HEADS, :]
        pieces.append(run)
    c_col = jnp.concatenate(pieces, axis=0)

    def online(s_blocks, v_blocks):
        m = m_sc[...]
        smax = s_blocks[0]
        for s in s_blocks[1:]:
            smax = jnp.maximum(smax, s)
        m_new = jnp.maximum(m, _row_max(smax))
        alpha = jnp.exp2(m - m_new)
        psum = None
        pv = None
        for s, vblk in zip(s_blocks, v_blocks):
            p = jnp.exp2(s - m_new)
            psum = p if psum is None else psum + p
            d = jnp.dot(p.astype(BF16), vblk, preferred_element_type=F32)
            pv = d if pv is None else pv + d
        l_sc[...] = alpha * l_sc[...] + _row_sum(psum)
        acc_sc[...] = alpha * acc_sc[...] + pv
        m_sc[...] = m_new

    col = lax.broadcasted_iota(jnp.int32, (rows, PAGE_KEYS), 1)
    row = lax.broadcasted_iota(jnp.int32, (rows, PAGE_KEYS), 0)
    own = (col & head_mask) == (row & head_mask)
    fixed = jnp.where(own, c_col * LOG2E, NEG_INF)

    lf = jnp.concatenate([lf_refs[pg][...] for pg in range(g)], axis=0)
    lane = lax.broadcasted_iota(jnp.int32, (g, PAGE_KEYS), 1)
    suffix = lf
    total = lf
    sh = N_HEADS
    while sh < PAGE_KEYS:
        suffix = suffix + jnp.where(lane + sh < PAGE_KEYS, pltpu.roll(suffix, PAGE_KEYS - sh, axis=1), 0.0)
        total = total + pltpu.roll(total, PAGE_KEYS - sh, axis=1)
        sh *= 2
    carry = carry_sc[...]
    past = []
    for pg in range(g):
        past.append((suffix[pg:pg + 1, :] - lf[pg:pg + 1, :] + carry) * LOG2E)
        carry = carry + total[pg:pg + 1, :]
    carry_sc[...] = carry

    def logits(pages):
        out = []
        for pg in pages:
            kp = k_refs[pg][...].reshape(PAGE_KEYS, HEAD_DIM).astype(BF16)
            s = lax.dot_general(q, kp, _NT, preferred_element_type=F32)
            out.append(s + fixed + past[pg])
        return out

    def values(pages):
        return [v_refs[pg][...].reshape(PAGE_KEYS, HEAD_DIM).astype(BF16) for pg in pages]

    groups = [list(range(a, a + PAGE_GROUP)) for a in range(0, g, PAGE_GROUP)]
    s_next = logits(groups[0])
    for gi, pages in enumerate(groups):
        s_cur = s_next
        if gi + 1 < len(groups):
            s_next = logits(groups[gi + 1])
        online(s_cur, values(pages))

    @pl.when(j == n_steps - 1)
    def _():
        ln = lax.broadcasted_iota(jnp.int32, (1, LANES), 1)
        c_row = lfrow_ref[...]
        sh2 = N_HEADS
        while sh2 < rows:
            c_row = c_row + jnp.where(ln >= sh2, pltpu.roll(c_row, sh2, axis=1), 0.0)
            sh2 *= 2
        cl = lax.broadcasted_iota(jnp.int32, (rows, LANES), 1)
        rw = lax.broadcasted_iota(jnp.int32, (rows, LANES), 0)
        valid = ((cl < rows) & ((cl & head_mask) == (rw & head_mask))
                 & ((cl >> head_shift) <= (rw >> head_shift)))
        s = lax.dot_general(q, kn_ref[...], _NT, preferred_element_type=F32)
        s = jnp.where(valid, s + (c_col - c_row) * LOG2E, NEG_INF)
        online([s], [vn_ref[...]])
        o_ref[...] = acc_sc[...] / l_sc[...]


def _decode_attention(page_table, q, lfrow, lfcol, kn, vn, cache_k, cache_v, cache_lf, n_tok):
    n_seq, n_pages = page_table.shape
    g = PAGES_PER_STEP
    n_steps = n_pages // g
    rows = n_tok * N_HEADS

    def page5(pg):
        return lambda b, j, pt: (0, pt[b * n_pages + (n_pages - 1 - (j * g + pg))], 0, 0, 0)

    def page3(pg):
        return lambda b, j, pt: (pt[b * n_pages + (n_pages - 1 - (j * g + pg))], 0, 0)

    seq3 = lambda s1, s2: pl.BlockSpec((None, s1, s2), lambda b, j, pt: (b, 0, 0))
    kv_block = (None, None, PAGE_SIZE, N_HEADS, HEAD_DIM)
    in_specs = [seq3(rows, HEAD_DIM), seq3(1, LANES), seq3(rows, 1), seq3(LANES, HEAD_DIM), seq3(LANES, HEAD_DIM)]
    in_specs += [pl.BlockSpec(kv_block, page5(pg)) for pg in range(g)]
    in_specs += [pl.BlockSpec(kv_block, page5(pg)) for pg in range(g)]
    in_specs += [pl.BlockSpec((None, 1, PAGE_KEYS), page3(pg)) for pg in range(g)]
    grid_spec = pltpu.PrefetchScalarGridSpec(
        num_scalar_prefetch=1,
        grid=(n_seq, n_steps),
        in_specs=in_specs,
        out_specs=pl.BlockSpec((None, rows, HEAD_DIM), lambda b, j, pt: (b, 0, 0)),
        scratch_shapes=[pltpu.VMEM((rows, 1), F32),
                        pltpu.VMEM((rows, 1), F32),
                        pltpu.VMEM((rows, HEAD_DIM), F32),
                        pltpu.VMEM((1, PAGE_KEYS), F32)],
    )
    return pl.pallas_call(
        functools.partial(_decode_kernel, n_tok=n_tok, n_steps=n_steps),
        out_shape=jax.ShapeDtypeStruct((n_seq, rows, HEAD_DIM), F32),
        grid_spec=grid_spec,
        compiler_params=_params(("arbitrary", "arbitrary")),
        name="fox_sample",
    )(page_table.reshape(-1), q, lfrow, lfcol, kn, vn,
      *([cache_k] * g), *([cache_v] * g), *([cache_lf] * g))


def _merge_kernel(y_ref, ya_ref, g_ref, x_ref, wglu_ref, bglu_ref, wbs_ref, wba_ref, wo_ref, ng_ref,
                  x1_ref, xn_ref):
    y = y_ref[...]
    gl = jnp.dot(y.astype(BF16), wglu_ref[...], preferred_element_type=F32) + bglu_ref[...]
    ys = (y * _sigmoid(gl)).astype(BF16)
    ms = jnp.dot(ys, wbs_ref[...], preferred_element_type=F32)
    ma = jnp.dot(ya_ref[...], wba_ref[...], preferred_element_type=F32)
    merged = (g_ref[:, 0:D_MODEL].astype(F32) * ms + g_ref[:, D_MODEL:2 * D_MODEL].astype(F32) * ma)
    x1 = x_ref[...] + jnp.dot(merged.astype(BF16), wo_ref[...], preferred_element_type=F32)
    x1_ref[...] = x1
    ms1 = jnp.mean(x1 * x1, axis=-1, keepdims=True)
    xn_ref[...] = (x1 * lax.rsqrt(ms1 + RMS_EPS) * ng_ref[...]).astype(BF16)


def _merge(y, ya, gates, x, w_glu, b_glu, w_bs, w_ba, w_o, n_g, tm):
    t = x.shape[0]
    row = lambda c: pl.BlockSpec((tm, c), lambda i: (i, 0))
    const = lambda r, c: pl.BlockSpec((r, c), lambda i: (0, 0), pipeline_mode=pl.Buffered(1))
    return pl.pallas_call(
        _merge_kernel,
        out_shape=(jax.ShapeDtypeStruct((t, D_MODEL), F32), jax.ShapeDtypeStruct((t, D_MODEL), BF16)),
        grid=(t // tm,),
        in_specs=[row(D_SSM), row(D_ATT), row(2 * D_MODEL), row(D_MODEL),
                  const(D_SSM, D_SSM), const(1, D_SSM), const(D_SSM, D_MODEL), const(D_ATT, D_MODEL),
                  const(D_MODEL, D_MODEL), const(1, D_MODEL)],
        out_specs=(row(D_MODEL), row(D_MODEL)),
        compiler_params=_params(("arbitrary",)),
        name="merge_out",
    )(y, ya, gates, x, w_glu, b_glu, w_bs, w_ba, w_o, n_g)


FFN_TF = 512


def _ffn_kernel(x1_ref, xn_ref, wg_ref, wu_ref, wd_ref, o_ref, acc_sc):
    f = pl.program_id(1)

    @pl.when(f == 0)
    def _():
        acc_sc[...] = x1_ref[...]

    xn = xn_ref[...]
    a = jnp.dot(xn, wg_ref[...], preferred_element_type=F32)
    b = jnp.dot(xn, wu_ref[...], preferred_element_type=F32)
    h = (a * _sigmoid(a) * b).astype(BF16)
    acc_sc[...] += jnp.dot(h, wd_ref[...], preferred_element_type=F32)

    @pl.when(f == pl.num_programs(1) - 1)
    def _():
        o_ref[...] = acc_sc[...]


def _ffn(x1, xn, w_g, w_u, w_d, tm):
    t = x1.shape[0]
    return pl.pallas_call(
        _ffn_kernel,
        out_shape=jax.ShapeDtypeStruct((t, D_MODEL), F32),
        grid=(t // tm, D_FF // FFN_TF),
        in_specs=[pl.BlockSpec((tm, D_MODEL), lambda i, f: (i, 0)),
                  pl.BlockSpec((tm, D_MODEL), lambda i, f: (i, 0)),
                  pl.BlockSpec((D_MODEL, FFN_TF), lambda i, f: (0, f)),
                  pl.BlockSpec((D_MODEL, FFN_TF), lambda i, f: (0, f)),
                  pl.BlockSpec((FFN_TF, D_MODEL), lambda i, f: (f, 0))],
        out_specs=pl.BlockSpec((tm, D_MODEL), lambda i, f: (i, 0)),
        scratch_shapes=[pltpu.VMEM((tm, D_MODEL), F32)],
        compiler_params=_params(("arbitrary", "arbitrary")),
        name="ffn",
    )(x1, xn, w_g, w_u, w_d)


def _block_diag_lanes(p):
    p4 = p.reshape(N_BLOCKS, GROUPS_PER_BLOCK, SSM_STATE, SSM_GROUP)
    eye = jnp.eye(GROUPS_PER_BLOCK, dtype=p.dtype)
    out = jnp.einsum("ohsc,gh->ogchs", p4, eye)
    return out.reshape(N_BLOCKS, LANES, BLOCK_STATE)


def kernel(x_prompt, x_sample, cache_k, cache_v, cache_logf, state_ssm_re, state_ssm_im, page_table,
           norm_mix_g, w_in, b_f, q_norm_g, k_norm_g, ssm_a_re, ssm_a_im, ssm_log_dt, ssm_b_re, ssm_b_im,
           ssm_c_re, ssm_c_im, ssm_d, w_glu, b_glu, w_br_ssm, w_br_att, w_out, norm_ffn_g,
           w_ffn_gate, w_ffn_up, w_ffn_down):
    batch, seq, _ = x_prompt.shape
    n_seq, n_tok, _ = x_sample.shape
    l = 0
    n_qkv = D_SSM + 3 * D_ATT

    w_in_l = w_in[l].astype(BF16)
    w_main = jnp.concatenate([w_in_l[:, :n_qkv], w_in_l[:, n_qkv + N_HEADS:]], axis=1)
    w_f = jnp.pad(w_in_l[:, n_qkv:n_qkv + N_HEADS], ((0, 0), (0, LANES - N_HEADS)))
    b_f_pad = jnp.pad(b_f[l], (0, LANES - N_HEADS)).reshape(1, LANES)
    norm_g = norm_mix_g[l].reshape(1, D_MODEL)
    q_g = q_norm_g[l].reshape(1, HEAD_DIM)
    k_g = k_norm_g[l].reshape(1, HEAD_DIM)
    ns = BLOCK_STATE
    ar_row = ssm_a_re[l].reshape(N_BLOCKS, 1, ns)
    ai_row = ssm_a_im[l].reshape(N_BLOCKS, 1, ns)
    dt_row = jnp.repeat(ssm_log_dt[l], SSM_STATE).reshape(N_BLOCKS, 1, ns)
    bd_bre = _block_diag_lanes(ssm_b_re[l])
    bd_bim = _block_diag_lanes(ssm_b_im[l])
    bd_cre = _block_diag_lanes(ssm_c_re[l].transpose(0, 2, 1)).transpose(0, 2, 1)
    bd_cim = _block_diag_lanes(ssm_c_im[l].transpose(0, 2, 1)).transpose(0, 2, 1)
    d_row = ssm_d[l].reshape(1, D_SSM)
    w_glu_b = w_glu[l].astype(BF16)
    b_glu_r = b_glu[l].reshape(1, D_SSM)
    w_bs = w_br_ssm[l].astype(BF16)
    w_ba = w_br_att[l].astype(BF16)
    w_o = w_out[l].astype(BF16)
    n_g = norm_ffn_g[l].reshape(1, D_MODEL)
    w_g = w_ffn_gate[l].astype(BF16)
    w_u = w_ffn_up[l].astype(BF16)
    w_d = w_ffn_down[l].astype(BF16)

    vrev, bsum, m_op, bmat, cmat, a1, a16 = _s5gen(ar_row, ai_row, dt_row, bd_bre, bd_bim, bd_cre, bd_cim)

    xp = x_prompt.reshape(batch * seq, D_MODEL)
    u, qb, k, kb, v, vt, gates, logf = _inproj(xp, norm_g, w_main, w_f, b_f_pad, q_g, k_g, tm=512,
                                               v_transposed=True)
    caq, cak = _cumaug(logf, batch, seq)
    y_ssm, hre, him = _s5_prompt(u, vrev, bsum, m_op, a16, d_row, batch, seq)
    y_att = _attention(qb, caq, kb, cak, vt, batch, seq)
    x1, xn1 = _merge(y_ssm, y_att, gates, xp, w_glu_b, b_glu_r, w_bs, w_ba, w_o, n_g, tm=256)
    y_p = _ffn(x1, xn1, w_g, w_u, w_d, tm=512).reshape(batch, seq, D_MODEL)

    rows = n_seq * n_tok
    xs = x_sample.transpose(1, 0, 2).reshape(rows, D_MODEL)
    us, qbs, ks, kbs, vs, vbs, gates_s, logf_s = _inproj(xs, norm_g, w_main, w_f, b_f_pad, q_g, k_g, tm=rows,
                                                         v_transposed=False)
    ys_ssm, hre_s, him_s = _s5_sample(us, state_ssm_re[l].reshape(n_seq, -1), state_ssm_im[l].reshape(n_seq, -1),
                                      bmat, cmat, a1, d_row, n_seq, n_tok)
    by_seq = lambda a: a.reshape(n_tok, n_seq, -1).transpose(1, 0, 2)
    th = n_tok * N_HEADS
    th_rows = lambda a: by_seq(a).reshape(n_seq, th, HEAD_DIM)
    pad_keys = lambda a: jnp.pad(th_rows(a), ((0, 0), (0, LANES - th), (0, 0)))
    lf_s = by_seq(logf_s[:, :N_HEADS]).reshape(n_seq, th)
    lfrow = jnp.pad(lf_s, ((0, 0), (0, LANES - th))).reshape(n_seq, 1, LANES)
    lfcol = lf_s.reshape(n_seq, th, 1)
    n_pool = cache_k.shape[1]
    ya_s = _decode_attention(page_table, th_rows(qbs), lfrow, lfcol, pad_keys(kbs), pad_keys(vbs),
                             cache_k, cache_v, cache_logf[l].reshape(n_pool, 1, PAGE_KEYS), n_tok)
    ya_s = ya_s.reshape(n_seq, n_tok, D_ATT).transpose(1, 0, 2).reshape(rows, D_ATT).astype(BF16)
    x1s, xn1s = _merge(ys_ssm, ya_s, gates_s, xs, w_glu_b, b_glu_r, w_bs, w_ba, w_o, n_g, tm=rows)
    y_s = _ffn(x1s, xn1s, w_g, w_u, w_d, tm=rows).reshape(n_tok, n_seq, D_MODEL).transpose(1, 0, 2)

    heads = lambda a, b_, t_: a.reshape(1, b_, t_, N_HEADS, HEAD_DIM)
    tok_major = lambda a: a.reshape(n_tok, n_seq, -1).transpose(1, 0, 2)
    return (
        y_p, y_s,
        heads(k, batch, seq), heads(v, batch, seq),
        logf[:, :N_HEADS].reshape(1, batch, seq, N_HEADS),
        hre.reshape(1, batch, N_GROUPS, SSM_STATE), him.reshape(1, batch, N_GROUPS, SSM_STATE),
        heads(tok_major(ks), n_seq, n_tok), heads(tok_major(vs), n_seq, n_tok),
        tok_major(logf_s[:, :N_HEADS]).reshape(1, n_seq, n_tok, N_HEADS),
        hre_s.reshape(1, n_seq, N_GROUPS, SSM_STATE), him_s.reshape(1, n_seq, N_GROUPS, SSM_STATE),
    )
```

```python
import functools
import math

import jax
import jax.numpy as jnp
import numpy as np
from jax import lax
from jax.experimental import pallas as pl
from jax.experimental.pallas import tpu as pltpu

F32 = jnp.float32
BF16 = jnp.bfloat16
HIGHEST = lax.Precision.HIGHEST

D_MODEL = 2048
D_SSM = D_MODEL // 2
SSM_GROUP = 16
N_GROUPS = D_SSM // SSM_GROUP
SSM_STATE = 64
HEAD_DIM = 128
N_HEADS = (D_MODEL // 2) // HEAD_DIM
D_ATT = N_HEADS * HEAD_DIM
D_FF = ((8 * D_MODEL + 3 * 256 - 1) // (3 * 256)) * 256
PAGE_SIZE = 128
RMS_EPS = 1e-6
NEG_INF = -1e30

LANES = 128
GROUPS_PER_BLOCK = LANES // SSM_GROUP
N_BLOCKS = N_GROUPS // GROUPS_PER_BLOCK
BLOCK_STATE = GROUPS_PER_BLOCK * SSM_STATE
CHUNK = 16
PROJ_COLS = 2048
GATE_SHIFT = N_HEADS
ATT_TQ = 512
ATT_TK = 512
VMEM_LIMIT = 56 * 1024 * 1024

LOG2E = math.log2(math.e)
QK_SCALE_LOG2 = (HEAD_DIM ** -0.5) * LOG2E

_NT = (((1,), (1,)), ((), ()))


def _params(sem):
    return pltpu.CompilerParams(dimension_semantics=sem, vmem_limit_bytes=VMEM_LIMIT)


def _gelu_exact(x):
    return 0.5 * x * (1.0 + lax.erf(x * math.sqrt(0.5)))


def _sigmoid(x):
    return 1.0 / (1.0 + jnp.exp(-x))


def _fold_lanes(x, op):
    acc = x[:, 0:LANES]
    for c in range(1, x.shape[1] // LANES):
        acc = op(acc, x[:, c * LANES:(c + 1) * LANES])
    return acc


def _row_max(x):
    return jnp.max(_fold_lanes(x, jnp.maximum), axis=-1, keepdims=True)


def _row_sum(x):
    return jnp.sum(_fold_lanes(x, jnp.add), axis=-1, keepdims=True)


def _log_sigmoid(x):
    return jnp.minimum(x, 0.0) - jnp.log1p(jnp.exp(-jnp.abs(x)))


def _head_norm(z, gain):
    outs = []
    for h in range(N_HEADS):
        blk = z[:, h * HEAD_DIM:(h + 1) * HEAD_DIM]
        ms = jnp.mean(blk * blk, axis=-1, keepdims=True)
        outs.append(blk * lax.rsqrt(ms + RMS_EPS) * gain)
    return jnp.concatenate(outs, axis=1)


def _proj_kernel(*refs, kind, nt, v_transposed):
    za, zb = refs[-2:]
    s = pl.program_id(0)
    half = PROJ_COLS // 2

    if kind == "uq":
        x_ref, g_ref, w_ref, qg_ref, xn_ref, u_ref, qb_ref = refs[:-2]

        def lhs():
            x = x_ref[...]
            ms = jnp.mean(x * x, axis=-1, keepdims=True)
            xn = (x * lax.rsqrt(ms + RMS_EPS) * g_ref[...]).astype(BF16)
            xn_ref[...] = xn
            return xn

        def epilogue(z):
            u_ref[...] = z[:, 0:half]
            qb_ref[...] = (_head_norm(z[:, half:], qg_ref[...]) * QK_SCALE_LOG2).astype(BF16)
    elif kind == "kv":
        xn_ref, w_ref, kg_ref, k_ref, kb_ref, v_ref, vb_ref = refs[:-2]
        lhs = lambda: xn_ref[...]

        def epilogue(z):
            kn = _head_norm(z[:, 0:half], kg_ref[...])
            k_ref[...] = kn
            kb_ref[...] = kn.astype(BF16)
            v = z[:, half:]
            v_ref[...] = v
            if v_transposed:
                vt = v.T
                for c in range(vb_ref.shape[0]):
                    vb_ref[c] = vt[:, c * ATT_TK:(c + 1) * ATT_TK].astype(BF16)
            else:
                vb_ref[...] = v.astype(BF16)
    elif kind == "gate_ssm":
        xn_ref, w_ref, bf_ref, gate_ref, logf_ref = refs[:-2]
        lhs = lambda: xn_ref[...]

        def epilogue(z):
            gate_ref[...] = _sigmoid(z).astype(BF16)
            logf_ref[...] = _log_sigmoid(z[:, 0:LANES] + bf_ref[...])
    else:
        xn_ref, w_ref, wt_ref, gate_ref, tail_ref = refs[:-2]

        def lhs():
            xn = xn_ref[...]
            tail_ref[...] = _sigmoid(jnp.dot(xn, wt_ref[...], preferred_element_type=F32)).astype(BF16)
            return xn

        def epilogue(z):
            gate_ref[...] = _sigmoid(z).astype(BF16)

    @pl.when(s == 0)
    def _():
        zb[...] = jnp.zeros_like(zb)

    def body(z_write, z_read):
        epilogue(z_read[...])
        if z_write is not None:
            z_write[...] = jnp.dot(lhs(), w_ref[...], preferred_element_type=F32)

    pl.when((s < nt) & (s % 2 == 0))(lambda: body(za, zb))
    pl.when((s < nt) & (s % 2 == 1))(lambda: body(zb, za))
    pl.when(s == nt)(lambda: body(None, za if (nt - 1) % 2 == 0 else zb))


def _proj_call(kind, inputs, in_specs, out_shape, out_specs, tm, nt, v_transposed=False):
    return pl.pallas_call(
        functools.partial(_proj_kernel, kind=kind, nt=nt, v_transposed=v_transposed),
        out_shape=out_shape,
        grid=(nt + 1,),
        in_specs=in_specs,
        out_specs=out_specs,
        scratch_shapes=[pltpu.VMEM((tm, PROJ_COLS), F32), pltpu.VMEM((tm, PROJ_COLS), F32)],
        compiler_params=_params(("arbitrary",)),
        name="proj_" + kind,
    )(*inputs)


def _inproj(x, norm_g, w_in_b, w_tail, b_f, q_g, k_g, tm, v_transposed):
    t = x.shape[0]
    nt = t // tm
    cur = lambda s: (jnp.minimum(s, nt - 1), 0)
    lag = lambda s: (jnp.maximum(s - 1, 0), 0)
    const = lambda s: (0, 0)
    slab = lambda c: pl.BlockSpec((D_MODEL, PROJ_COLS), lambda s: (0, c), pipeline_mode=pl.Buffered(1))
    rows = lambda width, index: pl.BlockSpec((tm, width), index)
    sds = jax.ShapeDtypeStruct

    xn, u, qb = _proj_call(
        "uq", (x, norm_g, w_in_b, q_g),
        [rows(D_MODEL, cur), pl.BlockSpec((1, D_MODEL), const), slab(0), pl.BlockSpec((1, HEAD_DIM), const)],
        (sds((t, D_MODEL), BF16), sds((t, D_SSM), F32), sds((t, D_ATT), BF16)),
        (rows(D_MODEL, cur), rows(D_SSM, lag), rows(D_ATT, lag)), tm, nt)

    if v_transposed:
        vb_shape = sds((t // ATT_TK, D_ATT, ATT_TK), BF16)
        vb_spec = pl.BlockSpec((tm // ATT_TK, D_ATT, ATT_TK), lambda s: (jnp.maximum(s - 1, 0), 0, 0))
    else:
        vb_shape = sds((t, D_ATT), BF16)
        vb_spec = rows(D_ATT, lag)
    k, kb, v, vb = _proj_call(
        "kv", (xn, w_in_b, k_g),
        [rows(D_MODEL, cur), slab(1), pl.BlockSpec((1, HEAD_DIM), const)],
        (sds((t, D_ATT), F32), sds((t, D_ATT), BF16), sds((t, D_ATT), F32), vb_shape),
        (rows(D_ATT, lag), rows(D_ATT, lag), rows(D_ATT, lag), vb_spec), tm, nt, v_transposed)

    gate_s, logf = _proj_call(
        "gate_ssm", (xn, w_in_b, b_f),
        [rows(D_MODEL, cur), slab(2), pl.BlockSpec((1, LANES), const)],
        (sds((t, PROJ_COLS), BF16), sds((t, LANES), F32)),
        (rows(PROJ_COLS, lag), rows(LANES, lag)), tm, nt)

    gate_a, gate_t = _proj_call(
        "gate_att", (xn, w_in_b, w_tail),
        [rows(D_MODEL, cur), slab(3), pl.BlockSpec((D_MODEL, LANES), const)],
        (sds((t, PROJ_COLS), BF16), sds((t, LANES), BF16)),
        (rows(PROJ_COLS, lag), rows(LANES, cur)), tm, nt)
    return u, qb, k, kb, v, vb, (gate_s, gate_a, gate_t), logf


CUM_ROWS = 256


def _cumaug_kernel(lf_ref, eq_ref, ek_ref, caq_ref, cak_ref, carry_sc):
    @pl.when(pl.program_id(1) == 0)
    def _():
        carry_sc[...] = jnp.zeros_like(carry_sc)

    lf = lf_ref[...]
    r = lax.broadcasted_iota(jnp.int32, (CUM_ROWS, CUM_ROWS), 0)
    c = lax.broadcasted_iota(jnp.int32, (CUM_ROWS, CUM_ROWS), 1)
    tri = (c <= r).astype(BF16)
    lf_hi = lf.astype(BF16)
    lf_r = lf - lf_hi.astype(F32)
    lf_mid = lf_r.astype(BF16)
    lf_lo = (lf_r - lf_mid.astype(F32)).astype(BF16)
    parts = jnp.dot(tri, jnp.concatenate([lf_hi, lf_mid, lf_lo], axis=1), preferred_element_type=F32)
    cs = (parts[:, 0:LANES] + parts[:, LANES:2 * LANES] + parts[:, 2 * LANES:3 * LANES]) + carry_sc[...]
    carry_sc[...] = cs[CUM_ROWS - 1:CUM_ROWS, :]
    cs2 = cs * LOG2E
    hi = cs2.astype(BF16).astype(F32)
    r1 = cs2 - hi
    mid = r1.astype(BF16).astype(F32)
    lo = (r1 - mid).astype(BF16).astype(F32)
    lane = lax.broadcasted_iota(jnp.int32, (CUM_ROWS, LANES), 1)
    packed = jnp.where(lane < N_HEADS, hi,
                       jnp.where(lane < 2 * N_HEADS, pltpu.roll(mid, N_HEADS, axis=1),
                                 jnp.where(lane < 3 * N_HEADS, pltpu.roll(lo, 2 * N_HEADS, axis=1),
                                           jnp.where(lane == 3 * N_HEADS, 1.0, 0.0)))).astype(BF16)
    caq_ref[...] = jnp.dot(packed, eq_ref[...], preferred_element_type=F32).astype(BF16)
    cak_ref[...] = jnp.dot(packed, ek_ref[...], preferred_element_type=F32).astype(BF16)


def _placement_matrices():
    eq = np.zeros((LANES, D_ATT), np.float32)
    ek = np.zeros((LANES, D_ATT), np.float32)
    for h in range(N_HEADS):
        base = h * HEAD_DIM
        for piece in range(3):
            eq[piece * N_HEADS + h, base + piece] = 1.0
            ek[piece * N_HEADS + h, base + 3 + piece] = -1.0
            eq[3 * N_HEADS, base + 3 + piece] = 1.0
            ek[3 * N_HEADS, base + piece] = 1.0
    return jnp.asarray(eq, BF16), jnp.asarray(ek, BF16)


def _cumaug(logf, batch, seq):
    nb = seq // CUM_ROWS
    eq, ek = _placement_matrices()
    const = lambda: pl.BlockSpec((LANES, D_ATT), lambda b, i: (0, 0))
    return pl.pallas_call(
        _cumaug_kernel,
        out_shape=(jax.ShapeDtypeStruct((batch * seq, D_ATT), BF16),
                   jax.ShapeDtypeStruct((batch * seq, D_ATT), BF16)),
        grid=(batch, nb),
        in_specs=[pl.BlockSpec((CUM_ROWS, LANES), lambda b, i: (b * nb + i, 0)), const(), const()],
        out_specs=(pl.BlockSpec((CUM_ROWS, D_ATT), lambda b, i: (b * nb + i, 0)),
                   pl.BlockSpec((CUM_ROWS, D_ATT), lambda b, i: (b * nb + i, 0))),
        scratch_shapes=[pltpu.VMEM((1, LANES), F32)],
        compiler_params=_params(("arbitrary", "arbitrary")),
        name="cumaug",
    )(logf, eq, ek)


ACC_ROWS = HEAD_DIM + 16


def _attn_kernel(q_ref, cq_ref, k_ref, ck_ref, vt_ref, o_ref, m_sc, acc_sc, sa_sc, sb_sc):
    i = pl.program_id(2)
    tk = ATT_TK
    m_sc[...] = jnp.full(m_sc.shape, NEG_INF, F32)
    acc_sc[...] = jnp.zeros_like(acc_sc)
    sub = lax.broadcasted_iota(jnp.int32, (ACC_ROWS - HEAD_DIM, tk), 0)
    ones_rows = jnp.where(sub == 0, 1.0, 0.0).astype(BF16)

    def logits(j, s_ref):
        ks = pl.multiple_of(j * tk, tk)
        kk = jnp.concatenate([k_ref[pl.ds(ks, tk), :], ck_ref[pl.ds(ks, tk), :]], axis=1)
        qq = jnp.concatenate([q_ref[...], cq_ref[...]], axis=1)
        s_ref[...] = lax.dot_general(kk, qq, _NT, preferred_element_type=F32)

    def consume(j, s_ref, masked):
        s = s_ref[...]
        if masked:
            key = lax.broadcasted_iota(jnp.int32, s.shape, 0)
            qry = lax.broadcasted_iota(jnp.int32, s.shape, 1)
            s = jnp.where(key <= qry, s, NEG_INF)
        vt = jnp.concatenate([vt_ref[j], ones_rows], axis=0)
        m_old = m_sc[...]
        m_new = jnp.maximum(m_old, jnp.max(s, axis=0, keepdims=True))
        alpha = jnp.exp2(m_old - m_new)
        p = jnp.exp2(s - m_new).astype(BF16)
        acc_sc[...] = alpha * acc_sc[...] + jnp.dot(vt, p, preferred_element_type=F32)
        m_sc[...] = m_new

    logits(0, sa_sc)

    def pair(jj, carry):
        logits(2 * jj + 1, sb_sc)
        consume(2 * jj, sa_sc, False)
        logits(2 * jj + 2, sa_sc)
        consume(2 * jj + 1, sb_sc, False)
        return carry

    lax.fori_loop(0, i // 2, pair, 0)

    @pl.when(i % 2 == 0)
    def _():
        consume(i, sa_sc, True)

    @pl.when(i % 2 == 1)
    def _():
        logits(i, sb_sc)
        consume(i - 1, sa_sc, False)
        consume(i, sb_sc, True)

    acc = acc_sc[...]
    out_t = acc[0:HEAD_DIM, :] / acc[HEAD_DIM:HEAD_DIM + 1, :]
    o_ref[...] = out_t.T.astype(BF16)


def _attention(qb, caq, kb, cak, vt, batch, seq):
    nq = seq // ATT_TQ
    nkb = seq // ATT_TK
    qspec = lambda: pl.BlockSpec((ATT_TQ, HEAD_DIM), lambda b, h, i: (b * nq + i, h))
    kspec = lambda: pl.BlockSpec((seq, HEAD_DIM), lambda b, h, i: (b, h))
    return pl.pallas_call(
        _attn_kernel,
        out_shape=jax.ShapeDtypeStruct((batch * seq, D_ATT), BF16),
        grid=(batch, N_HEADS, nq),
        in_specs=[qspec(), qspec(), kspec(), kspec(),
                  pl.BlockSpec((nkb, HEAD_DIM, ATT_TK), lambda b, h, i: (b, h, 0))],
        out_specs=qspec(),
        scratch_shapes=[pltpu.VMEM((1, ATT_TQ), F32), pltpu.VMEM((ACC_ROWS, ATT_TQ), F32),
                        pltpu.VMEM((ATT_TK, ATT_TQ), F32), pltpu.VMEM((ATT_TK, ATT_TQ), F32)],
        compiler_params=_params(("arbitrary", "arbitrary", "arbitrary")),
        name="fox_prompt",
    )(qb, caq, kb, cak, vt)


def _discretise(a_re, a_im, log_dt):
    dt = jnp.exp(log_dt)
    mag = jnp.exp(dt * a_re)
    ang = dt * a_im
    abr = mag * jnp.cos(ang)
    abi = mag * jnp.sin(ang)
    e_re = abr - 1.0
    e_im = abi
    inv_den = 1.0 / (a_re * a_re + a_im * a_im)
    f_re = (e_re * a_re + e_im * a_im) * inv_den
    f_im = (e_im * a_re - e_re * a_im) * inv_den
    return abr, abi, f_re, f_im


def _powers(abr, abi, n):
    pr, pi = [jnp.ones_like(abr)], [jnp.zeros_like(abi)]
    for _ in range(n):
        r, i = pr[-1], pi[-1]
        pr.append(r * abr - i * abi)
        pi.append(r * abi + i * abr)
    return pr, pi


def _s5gen_kernel(ar_row, ai_row, dt_row, bre_ref, bim_ref, cre_ref, cim_ref,
                  vrev_ref, bsum_ref, m_ref, bmat_ref, cmat_ref, a1_ref, a16_ref):
    ns = BLOCK_STATE
    abr, abi, f_re, f_im = _discretise(ar_row[...], ai_row[...], dt_row[...])
    bre, bim = bre_ref[...], bim_ref[...]
    bbr = f_re * bre - f_im * bim
    bbi = f_re * bim + f_im * bre
    bmat = jnp.concatenate([bbr, bbi], axis=1)
    bmat_ref[...] = bmat
    pr, pi = _powers(abr, abi, CHUNK)
    a1_ref[...] = jnp.concatenate([pr[1], pi[1]], axis=1)
    a16_ref[...] = jnp.concatenate([pr[CHUNK], pi[CHUNK]], axis=1)
    for i in range(CHUNK):
        r, im = pr[CHUNK - 1 - i], pi[CHUNK - 1 - i]
        blk = jnp.concatenate([r * bbr - im * bbi, r * bbi + im * bbr], axis=1)
        bsum_ref[i * LANES:(i + 1) * LANES, :] = blk.astype(BF16)
    npow = CHUNK + 1
    stacked = jnp.concatenate(pr + pi + [jnp.zeros((LANES - 2 * npow, ns), F32)], axis=0)
    pt = stacked.T
    qr = [pt[:, t:t + 1] for t in range(npow)]
    qi = [pt[:, npow + t:npow + t + 1] for t in range(npow)]
    cre, cim = cre_ref[...], cim_ref[...]
    vrev_ref[(CHUNK - 1) * LANES:CHUNK * LANES, 0:LANES] = jnp.zeros((LANES, LANES), BF16)
    bmat_hi = bmat.astype(BF16)
    for tau in range(CHUNK + 1):
        blk = jnp.concatenate([cre * qr[tau] - cim * qi[tau],
                               -(cre * qi[tau] + cim * qr[tau])], axis=0)
        if tau == 0:
            cmat_ref[...] = blk
        else:
            m_ref[:, (tau - 1) * LANES:tau * LANES] = blk.astype(BF16)
        if tau < CHUNK:
            w = jnp.dot(bmat_hi, blk.astype(BF16), preferred_element_type=F32).astype(BF16)
            k = CHUNK - 1 - tau
            vrev_ref[k * LANES:(k + 1) * LANES, LANES:2 * LANES] = w
            if k >= 1:
                vrev_ref[(k - 1) * LANES:k * LANES, 0:LANES] = w


def _s5gen(ar_row, ai_row, dt_row, bd_bre, bd_bim, bd_cre, bd_cim):
    ns = BLOCK_STATE
    b3 = lambda s1, s2: pl.BlockSpec((None, s1, s2), lambda o: (o, 0, 0))
    out_shape = (
        jax.ShapeDtypeStruct((N_BLOCKS, CHUNK * LANES, 2 * LANES), BF16),
        jax.ShapeDtypeStruct((N_BLOCKS, CHUNK * LANES, 2 * ns), BF16),
        jax.ShapeDtypeStruct((N_BLOCKS, 2 * ns, CHUNK * LANES), BF16),
        jax.ShapeDtypeStruct((N_BLOCKS, LANES, 2 * ns), F32),
        jax.ShapeDtypeStruct((N_BLOCKS, 2 * ns, LANES), F32),
        jax.ShapeDtypeStruct((N_BLOCKS, 1, 2 * ns), F32),
        jax.ShapeDtypeStruct((N_BLOCKS, 1, 2 * ns), F32),
    )
    return pl.pallas_call(
        _s5gen_kernel,
        out_shape=out_shape,
        grid=(N_BLOCKS,),
        in_specs=[b3(1, ns), b3(1, ns), b3(1, ns),
                  b3(LANES, ns), b3(LANES, ns), b3(ns, LANES), b3(ns, LANES)],
        out_specs=(b3(CHUNK * LANES, 2 * LANES), b3(CHUNK * LANES, 2 * ns), b3(2 * ns, CHUNK * LANES),
                   b3(LANES, 2 * ns), b3(2 * ns, LANES), b3(1, 2 * ns), b3(1, 2 * ns)),
        compiler_params=_params(("arbitrary",)),
        name="s5_operators",
    )(ar_row, ai_row, dt_row, bd_bre, bd_bim, bd_cre, bd_cim)


def _s5_prompt_kernel(u_ref, vrev_ref, bsum_ref, m_ref, a16_ref, d_ref,
                      y_ref, hre_ref, him_ref, ucat_sc, s_sc, hin_sc, *, n_chunks):
    ns = BLOCK_STATE
    for i in range(CHUNK):
        ucat_sc[:, i * LANES:(i + 1) * LANES] = u_ref[pl.ds(i, n_chunks, stride=CHUNK), :].astype(BF16)
    s_sc[...] = jnp.dot(ucat_sc[...], bsum_ref[...], preferred_element_type=F32)
    ar = a16_ref[:, 0:ns]
    ai = a16_ref[:, ns:2 * ns]

    def step(k, carry):
        hr, hi = carry
        hin_sc[pl.ds(k, 1), 0:ns] = hr
        hin_sc[pl.ds(k, 1), ns:2 * ns] = hi
        sr = s_sc[pl.ds(k, 1), 0:ns]
        si = s_sc[pl.ds(k, 1), ns:2 * ns]
        return ar * hr - ai * hi + sr, ar * hi + ai * hr + si

    zero = jnp.zeros((1, ns), F32)
    hr, hi = lax.fori_loop(0, n_chunks, step, (zero, zero))
    hre_ref[...] = hr
    him_ref[...] = hi
    hin = hin_sc[...].astype(BF16)
    d = d_ref[...]
    for jp in range(CHUNK // 2):
        j = 2 * jp
        kk = (j + 2) * LANES
        acc = jnp.dot(ucat_sc[:, 0:kk], vrev_ref[(CHUNK - 2 - j) * LANES:, :],
                      preferred_element_type=F32)
        acc = acc + jnp.dot(hin, m_ref[:, j * LANES:(j + 2) * LANES], preferred_element_type=F32)
        for jj in range(2):
            uj = u_ref[pl.ds(j + jj, n_chunks, stride=CHUNK), :]
            val = acc[:, jj * LANES:(jj + 1) * LANES] + d * uj
            y_ref[pl.ds(j + jj, n_chunks, stride=CHUNK), :] = _gelu_exact(val)


def _s5_prompt(u, vrev, bsum, m, a16, d, batch, seq):
    ns = BLOCK_STATE
    n_chunks = seq // CHUNK
    w3 = lambda s1, s2: pl.BlockSpec((None, s1, s2), lambda o, b: (o, 0, 0))
    return pl.pallas_call(
        functools.partial(_s5_prompt_kernel, n_chunks=n_chunks),
        out_shape=(jax.ShapeDtypeStruct((batch * seq, D_SSM), F32),
                   jax.ShapeDtypeStruct((batch, 1, N_GROUPS * SSM_STATE), F32),
                   jax.ShapeDtypeStruct((batch, 1, N_GROUPS * SSM_STATE), F32)),
        grid=(N_BLOCKS, batch),
        in_specs=[pl.BlockSpec((seq, LANES), lambda o, b: (b, o)),
                  w3(CHUNK * LANES, 2 * LANES), w3(CHUNK * LANES, 2 * ns), w3(2 * ns, CHUNK * LANES),
                  w3(1, 2 * ns),
                  pl.BlockSpec((1, LANES), lambda o, b: (0, o))],
        out_specs=(pl.BlockSpec((seq, LANES), lambda o, b: (b, o)),
                   pl.BlockSpec((None, 1, ns), lambda o, b: (b, 0, o)),
                   pl.BlockSpec((None, 1, ns), lambda o, b: (b, 0, o))),
        scratch_shapes=[pltpu.VMEM((n_chunks, CHUNK * LANES), BF16),
                        pltpu.VMEM((n_chunks, 2 * ns), F32),
                        pltpu.VMEM((n_chunks, 2 * ns), F32)],
        compiler_params=_params(("arbitrary", "arbitrary")),
        name="s5_prompt",
    )(u, vrev, bsum, m, a16, d)


def _s5_sample_kernel(u_ref, h0r_ref, h0i_ref, bmat_ref, cmat_ref, a1_ref, d_ref,
                      y_ref, hre_ref, him_ref, *, n_seq, n_tok):
    ns = BLOCK_STATE
    u = u_ref[...]
    bu = jnp.dot(u, bmat_ref[...], precision=HIGHEST, preferred_element_type=F32)
    ar = a1_ref[:, 0:ns]
    ai = a1_ref[:, ns:2 * ns]
    hr, hi = h0r_ref[...], h0i_ref[...]
    hs = []
    for t in range(n_tok):
        br = bu[t * n_seq:(t + 1) * n_seq, 0:ns]
        bi = bu[t * n_seq:(t + 1) * n_seq, ns:2 * ns]
        hr, hi = ar * hr - ai * hi + br, ar * hi + ai * hr + bi
        hs.append(jnp.concatenate([hr, hi], axis=1))
    hcat = jnp.concatenate(hs, axis=0)
    val = jnp.dot(hcat, cmat_ref[...], precision=HIGHEST, preferred_element_type=F32) + d_ref[...] * u
    y_ref[...] = _gelu_exact(val)
    hre_ref[...] = hr
    him_ref[...] = hi


def _s5_sample(u, h0r, h0i, bmat, cmat, a1, d, n_seq, n_tok):
    ns = BLOCK_STATE
    rows = n_seq * n_tok
    w3 = lambda s1, s2: pl.BlockSpec((None, s1, s2), lambda o: (o, 0, 0))
    col = lambda r, c: pl.BlockSpec((r, c), lambda o: (0, o))
    return pl.pallas_call(
        functools.partial(_s5_sample_kernel, n_seq=n_seq, n_tok=n_tok),
        out_shape=(jax.ShapeDtypeStruct((rows, D_SSM), F32),
                   jax.ShapeDtypeStruct((n_seq, N_GROUPS * SSM_STATE), F32),
                   jax.ShapeDtypeStruct((n_seq, N_GROUPS * SSM_STATE), F32)),
        grid=(N_BLOCKS,),
        in_specs=[col(rows, LANES), col(n_seq, ns), col(n_seq, ns),
                  w3(LANES, 2 * ns), w3(2 * ns, LANES), w3(1, 2 * ns), col(1, LANES)],
        out_specs=(col(rows, LANES), col(n_seq, ns), col(n_seq, ns)),
        compiler_params=_params(("arbitrary",)),
        name="s5_sample",
    )(u, h0r, h0i, bmat, cmat, a1, d)


PAGES_PER_STEP = 16
PAGE_GROUP = 4

PAGE_KEYS = PAGE_SIZE * N_HEADS


def _decode_kernel(pt_ref, q_ref, lfrow_ref, lfcol_ref, kn_ref, vn_ref, *refs, n_tok, n_steps):
    g = PAGES_PER_STEP
    k_refs = refs[0:g]
    v_refs = refs[g:2 * g]
    lf_refs = refs[2 * g:3 * g]
    o_ref = refs[3 * g]
    m_sc, l_sc, acc_sc, carry_sc = refs[3 * g + 1:]
    j = pl.program_id(1)
    rows = n_tok * N_HEADS
    head_mask = N_HEADS - 1
    head_shift = N_HEADS.bit_length() - 1

    @pl.when(j == 0)
    def _():
        m_sc[...] = jnp.full(m_sc.shape, NEG_INF, F32)
        l_sc[...] = jnp.zeros_like(l_sc)
        acc_sc[...] = jnp.zeros_like(acc_sc)
        carry_sc[...] = jnp.zeros_like(carry_sc)

    q = q_ref[...]
    lfcol = lfcol_ref[...]
    run = jnp.zeros((N_HEADS, 1), F32)
    pieces = []
    for t in range(n_tok):
        run = run + lfcol[t * N_HEADS:(t + 1) * N_HEADS, :]
        pieces.append(run)
    c_col = jnp.concatenate(pieces, axis=0)

    def online(s_blocks, v_blocks):
        m = m_sc[...]
        smax = s_blocks[0]
        for s in s_blocks[1:]:
            smax = jnp.maximum(smax, s)
        m_new = jnp.maximum(m, _row_max(smax))
        alpha = jnp.exp2(m - m_new)
        psum = None
        pv = None
        for s, vblk in zip(s_blocks, v_blocks):
            p = jnp.exp2(s - m_new)
            psum = p if psum is None else psum + p
            d = jnp.dot(p.astype(BF16), vblk, preferred_element_type=F32)
            pv = d if pv is None else pv + d
        l_sc[...] = alpha * l_sc[...] + _row_sum(psum)
        acc_sc[...] = alpha * acc_sc[...] + pv
        m_sc[...] = m_new

    col = lax.broadcasted_iota(jnp.int32, (rows, PAGE_KEYS), 1)
    row = lax.broadcasted_iota(jnp.int32, (rows, PAGE_KEYS), 0)
    own = (col & head_mask) == (row & head_mask)
    fixed = jnp.where(own, c_col * LOG2E, NEG_INF)

    lf = jnp.concatenate([lf_refs[pg][...] for pg in range(g)], axis=0)
    lane = lax.broadcasted_iota(jnp.int32, (g, PAGE_KEYS), 1)
    suffix = lf
    total = lf
    sh = N_HEADS
    while sh < PAGE_KEYS:
        suffix = suffix + jnp.where(lane + sh < PAGE_KEYS, pltpu.roll(suffix, PAGE_KEYS - sh, axis=1), 0.0)
        total = total + pltpu.roll(total, PAGE_KEYS - sh, axis=1)
        sh *= 2
    carry = carry_sc[...]
    past = []
    for pg in range(g):
        past.append((suffix[pg:pg + 1, :] - lf[pg:pg + 1, :] + carry) * LOG2E)
        carry = carry + total[pg:pg + 1, :]
    carry_sc[...] = carry

    def logits(pages):
        out = []
        for pg in pages:
            kp = k_refs[pg][...].reshape(PAGE_KEYS, HEAD_DIM).astype(BF16)
            s = lax.dot_general(q, kp, _NT, preferred_element_type=F32)
            out.append(s + fixed + past[pg])
        return out

    def values(pages):
        return [v_refs[pg][...].reshape(PAGE_KEYS, HEAD_DIM).astype(BF16) for pg in pages]

    groups = [list(range(a, a + PAGE_GROUP)) for a in range(0, g, PAGE_GROUP)]
    s_next = logits(groups[0])
    for gi, pages in enumerate(groups):
        s_cur = s_next
        if gi + 1 < len(groups):
            s_next = logits(groups[gi + 1])
        online(s_cur, values(pages))

    @pl.when(j == n_steps - 1)
    def _():
        ln = lax.broadcasted_iota(jnp.int32, (1, LANES), 1)
        c_row = lfrow_ref[...]
        sh2 = N_HEADS
        while sh2 < rows:
            c_row = c_row + jnp.where(ln >= sh2, pltpu.roll(c_row, sh2, axis=1), 0.0)
            sh2 *= 2
        cl = lax.broadcasted_iota(jnp.int32, (rows, LANES), 1)
        rw = lax.broadcasted_iota(jnp.int32, (rows, LANES), 0)
        valid = ((cl < rows) & ((cl & head_mask) == (rw & head_mask))
                 & ((cl >> head_shift) <= (rw >> head_shift)))
        s = lax.dot_general(q, kn_ref[...], _NT, preferred_element_type=F32)
        s = jnp.where(valid, s + (c_col - c_row) * LOG2E, NEG_INF)
        online([s], [vn_ref[...]])
        o_ref[...] = acc_sc[...] / l_sc[...]


def _decode_attention(page_table, q, lfrow, lfcol, kn, vn, cache_k, cache_v, cache_lf, n_tok):
    n_seq, n_pages = page_table.shape
    g = PAGES_PER_STEP
    n_steps = n_pages // g
    rows = n_tok * N_HEADS

    def page5(pg):
        return lambda b, j, pt: (0, pt[b * n_pages + (n_pages - 1 - (j * g + pg))], 0, 0, 0)

    def page3(pg):
        return lambda b, j, pt: (pt[b * n_pages + (n_pages - 1 - (j * g + pg))], 0, 0)

    seq3 = lambda s1, s2: pl.BlockSpec((None, s1, s2), lambda b, j, pt: (b, 0, 0))
    kv_block = (None, None, PAGE_SIZE, N_HEADS, HEAD_DIM)
    in_specs = [seq3(rows, HEAD_DIM), seq3(1, LANES), seq3(rows, 1), seq3(LANES, HEAD_DIM), seq3(LANES, HEAD_DIM)]
    in_specs += [pl.BlockSpec(kv_block, page5(pg)) for pg in range(g)]
    in_specs += [pl.BlockSpec(kv_block, page5(pg)) for pg in range(g)]
    in_specs += [pl.BlockSpec((None, 1, PAGE_KEYS), page3(pg)) for pg in range(g)]
    grid_spec = pltpu.PrefetchScalarGridSpec(
        num_scalar_prefetch=1,
        grid=(n_seq, n_steps),
        in_specs=in_specs,
        out_specs=pl.BlockSpec((None, rows, HEAD_DIM), lambda b, j, pt: (b, 0, 0)),
        scratch_shapes=[pltpu.VMEM((rows, 1), F32),
                        pltpu.VMEM((rows, 1), F32),
                        pltpu.VMEM((rows, HEAD_DIM), F32),
                        pltpu.VMEM((1, PAGE_KEYS), F32)],
    )
    return pl.pallas_call(
        functools.partial(_decode_kernel, n_tok=n_tok, n_steps=n_steps),
        out_shape=jax.ShapeDtypeStruct((n_seq, rows, HEAD_DIM), F32),
        grid_spec=grid_spec,
        compiler_params=_params(("arbitrary", "arbitrary")),
        name="fox_sample",
    )(page_table.reshape(-1), q, lfrow, lfcol, kn, vn,
      *([cache_k] * g), *([cache_v] * g), *([cache_lf] * g))


def _merge_kernel(y_ref, ya_ref, gs_ref, ga_ref, gt_ref, x_ref, wglu_ref, bglu_ref, wbs_ref, wba_ref, wo_ref,
                  ng_ref, x1_ref, xn_ref):
    y = y_ref[...]
    gl = jnp.dot(y.astype(BF16), wglu_ref[...], preferred_element_type=F32) + bglu_ref[...]
    ys = (y * _sigmoid(gl)).astype(BF16)
    ms = jnp.dot(ys, wbs_ref[...], preferred_element_type=F32)
    ma = jnp.dot(ya_ref[...], wba_ref[...], preferred_element_type=F32)
    ga = ga_ref[...]
    gate_s = jnp.concatenate([gs_ref[...], ga[:, 0:LANES]], axis=1).astype(F32)
    gate_a = jnp.concatenate([ga, gt_ref[...]], axis=1).astype(F32)
    merged = gate_s * ms + gate_a * ma
    x1 = x_ref[...] + jnp.dot(merged.astype(BF16), wo_ref[...], preferred_element_type=F32)
    x1_ref[...] = x1
    ms1 = jnp.mean(x1 * x1, axis=-1, keepdims=True)
    xn_ref[...] = (x1 * lax.rsqrt(ms1 + RMS_EPS) * ng_ref[...]).astype(BF16)


MERGE_COLS = D_MODEL + LANES


def _merge(y, ya, gates, x, w_glu, b_glu, w_bs, w_ba, w_o, n_g, tm):
    t = x.shape[0]
    gate_s, gate_a, gate_t = gates
    row = lambda c: pl.BlockSpec((tm, c), lambda i: (i, 0))
    const = lambda r, c: pl.BlockSpec((r, c), lambda i: (0, 0), pipeline_mode=pl.Buffered(1))
    return pl.pallas_call(
        _merge_kernel,
        out_shape=(jax.ShapeDtypeStruct((t, D_MODEL), F32), jax.ShapeDtypeStruct((t, D_MODEL), BF16)),
        grid=(t // tm,),
        in_specs=[row(D_SSM), row(D_ATT), row(PROJ_COLS), row(PROJ_COLS), row(LANES), row(D_MODEL),
                  const(D_SSM, D_SSM), const(1, D_SSM), const(D_SSM, MERGE_COLS), const(D_ATT, MERGE_COLS),
                  const(MERGE_COLS, D_MODEL), const(1, D_MODEL)],
        out_specs=(row(D_MODEL), row(D_MODEL)),
        compiler_params=_params(("arbitrary",)),
        name="merge_out",
    )(y, ya, gate_s, gate_a, gate_t, x, w_glu, b_glu, w_bs, w_ba, w_o, n_g)


FFN_TF = 512


def _ffn_kernel(x1_ref, xn_ref, wg_ref, wu_ref, wd_ref, o_ref):
    @pl.when(pl.program_id(1) == 0)
    def _():
        o_ref[...] = x1_ref[...]

    xn = xn_ref[...]
    a = jnp.dot(xn, wg_ref[...], preferred_element_type=F32)
    b = jnp.dot(xn, wu_ref[...], preferred_element_type=F32)
    h = (a * _sigmoid(a) * b).astype(BF16)
    o_ref[...] += jnp.dot(h, wd_ref[...], preferred_element_type=F32)


def _ffn(x1, xn, w_g, w_u, w_d, tm):
    t = x1.shape[0]
    return pl.pallas_call(
        _ffn_kernel,
        out_shape=jax.ShapeDtypeStruct((t, D_MODEL), F32),
        grid=(t // tm, D_FF // FFN_TF),
        in_specs=[pl.BlockSpec((tm, D_MODEL), lambda i, f: (i, 0)),
                  pl.BlockSpec((tm, D_MODEL), lambda i, f: (i, 0), pipeline_mode=pl.Buffered(1)),
                  pl.BlockSpec((D_MODEL, FFN_TF), lambda i, f: (0, f)),
                  pl.BlockSpec((D_MODEL, FFN_TF), lambda i, f: (0, f)),
                  pl.BlockSpec((FFN_TF, D_MODEL), lambda i, f: (f, 0))],
        out_specs=pl.BlockSpec((tm, D_MODEL), lambda i, f: (i, 0)),
        compiler_params=_params(("arbitrary", "arbitrary")),
        name="ffn",
    )(x1, xn, w_g, w_u, w_d)


def _block_diag_lanes(p):
    p4 = p.reshape(N_BLOCKS, GROUPS_PER_BLOCK, SSM_STATE, SSM_GROUP)
    eye = jnp.eye(GROUPS_PER_BLOCK, dtype=p.dtype)
    out = jnp.einsum("ohsc,gh->ogchs", p4, eye)
    return out.reshape(N_BLOCKS, LANES, BLOCK_STATE)


def kernel(x_prompt, x_sample, cache_k, cache_v, cache_logf, state_ssm_re, state_ssm_im, page_table,
           norm_mix_g, w_in, b_f, q_norm_g, k_norm_g, ssm_a_re, ssm_a_im, ssm_log_dt, ssm_b_re, ssm_b_im,
           ssm_c_re, ssm_c_im, ssm_d, w_glu, b_glu, w_br_ssm, w_br_att, w_out, norm_ffn_g,
           w_ffn_gate, w_ffn_up, w_ffn_down):
    batch, seq, _ = x_prompt.shape
    n_seq, n_tok, _ = x_sample.shape
    l = 0
    n_qkv = D_SSM + 3 * D_ATT

    w_in_b = w_in[l].astype(BF16)
    n_slabs = w_in_b.shape[1] // PROJ_COLS
    n_tail = w_in_b.shape[1] - n_slabs * PROJ_COLS
    assert n_qkv == 2 * PROJ_COLS and n_tail == GATE_SHIFT == N_HEADS
    w_tail = jnp.pad(w_in_b[:, n_slabs * PROJ_COLS:], ((0, 0), (0, LANES - n_tail)))
    b_f_pad = jnp.pad(b_f[l], (0, LANES - N_HEADS)).reshape(1, LANES)
    norm_g = norm_mix_g[l].reshape(1, D_MODEL)
    q_g = q_norm_g[l].reshape(1, HEAD_DIM)
    k_g = k_norm_g[l].reshape(1, HEAD_DIM)
    ns = BLOCK_STATE
    ar_row = ssm_a_re[l].reshape(N_BLOCKS, 1, ns)
    ai_row = ssm_a_im[l].reshape(N_BLOCKS, 1, ns)
    dt_row = jnp.repeat(ssm_log_dt[l], SSM_STATE).reshape(N_BLOCKS, 1, ns)
    bd_bre = _block_diag_lanes(ssm_b_re[l])
    bd_bim = _block_diag_lanes(ssm_b_im[l])
    bd_cre = _block_diag_lanes(ssm_c_re[l].transpose(0, 2, 1)).transpose(0, 2, 1)
    bd_cim = _block_diag_lanes(ssm_c_im[l].transpose(0, 2, 1)).transpose(0, 2, 1)
    d_row = ssm_d[l].reshape(1, D_SSM)
    w_glu_b = w_glu[l].astype(BF16)
    b_glu_r = b_glu[l].reshape(1, D_SSM)
    shift_pad = (GATE_SHIFT, MERGE_COLS - D_MODEL - GATE_SHIFT)
    w_bs = jnp.pad(w_br_ssm[l].astype(BF16), ((0, 0), shift_pad))
    w_ba = jnp.pad(w_br_att[l].astype(BF16), ((0, 0), shift_pad))
    w_o = jnp.pad(w_out[l].astype(BF16), (shift_pad, (0, 0)))
    n_g = norm_ffn_g[l].reshape(1, D_MODEL)
    w_g = w_ffn_gate[l].astype(BF16)
    w_u = w_ffn_up[l].astype(BF16)
    w_d = w_ffn_down[l].astype(BF16)

    vrev, bsum, m_op, bmat, cmat, a1, a16 = _s5gen(ar_row, ai_row, dt_row, bd_bre, bd_bim, bd_cre, bd_cim)

    xp = x_prompt.reshape(batch * seq, D_MODEL)
    u, qb, k, kb, v, vt, gates, logf = _inproj(xp, norm_g, w_in_b, w_tail, b_f_pad, q_g, k_g, tm=512,
                                               v_transposed=True)
    caq, cak = _cumaug(logf, batch, seq)
    y_ssm, hre, him = _s5_prompt(u, vrev, bsum, m_op, a16, d_row, batch, seq)
    y_att = _attention(qb, caq, kb, cak, vt, batch, seq)
    x1, xn1 = _merge(y_ssm, y_att, gates, xp, w_glu_b, b_glu_r, w_bs, w_ba, w_o, n_g, tm=256)
    y_p = _ffn(x1, xn1, w_g, w_u, w_d, tm=1024).reshape(batch, seq, D_MODEL)

    rows = n_seq * n_tok
    xs = x_sample.transpose(1, 0, 2).reshape(rows, D_MODEL)
    us, qbs, ks, kbs, vs, vbs, gates_s, logf_s = _inproj(xs, norm_g, w_in_b, w_tail, b_f_pad, q_g, k_g, tm=rows,
                                                         v_transposed=False)
    ys_ssm, hre_s, him_s = _s5_sample(us, state_ssm_re[l].reshape(n_seq, -1), state_ssm_im[l].reshape(n_seq, -1),
                                      bmat, cmat, a1, d_row, n_seq, n_tok)
    by_seq = lambda a: a.reshape(n_tok, n_seq, -1).transpose(1, 0, 2)
    th = n_tok * N_HEADS
    th_rows = lambda a: by_seq(a).reshape(n_seq, th, HEAD_DIM)
    pad_keys = lambda a: jnp.pad(th_rows(a), ((0, 0), (0, LANES - th), (0, 0)))
    lf_s = by_seq(logf_s[:, :N_HEADS]).reshape(n_seq, th)
    lfrow = jnp.pad(lf_s, ((0, 0), (0, LANES - th))).reshape(n_seq, 1, LANES)
    lfcol = lf_s.reshape(n_seq, th, 1)
    n_pool = cache_k.shape[1]
    ya_s = _decode_attention(page_table, th_rows(qbs), lfrow, lfcol, pad_keys(kbs), pad_keys(vbs),
                             cache_k, cache_v, cache_logf[l].reshape(n_pool, 1, PAGE_KEYS), n_tok)
    ya_s = ya_s.reshape(n_seq, n_tok, D_ATT).transpose(1, 0, 2).reshape(rows, D_ATT).astype(BF16)
    x1s, xn1s = _merge(ys_ssm, ya_s, gates_s, xs, w_glu_b, b_glu_r, w_bs, w_ba, w_o, n_g, tm=rows)
    y_s = _ffn(x1s, xn1s, w_g, w_u, w_d, tm=rows).reshape(n_tok, n_seq, D_MODEL).transpose(1, 0, 2)

    heads = lambda a, b_, t_: a.reshape(1, b_, t_, N_HEADS, HEAD_DIM)
    tok_major = lambda a: a.reshape(n_tok, n_seq, -1).transpose(1, 0, 2)
    return (
        y_p, y_s,
        heads(k, batch, seq), heads(v, batch, seq),
        logf[:, :N_HEADS].reshape(1, batch, seq, N_HEADS),
        hre.reshape(1, batch, N_GROUPS, SSM_STATE), him.reshape(1, batch, N_GROUPS, SSM_STATE),
        heads(tok_major(ks), n_seq, n_tok), heads(tok_major(vs), n_seq, n_tok),
        tok_major(logf_s[:, :N_HEADS]).reshape(1, n_seq, n_tok, N_HEADS),
        hre_s.reshape(1, n_seq, N_GROUPS, SSM_STATE), him_s.reshape(1, n_seq, N_GROUPS, SSM_STATE),
    )
```

```python
import functools
import math

import jax
import jax.numpy as jnp
import numpy as np
from jax import lax
from jax.experimental import pallas as pl
from jax.experimental.pallas import tpu as pltpu

F32 = jnp.float32
BF16 = jnp.bfloat16
HIGHEST = lax.Precision.HIGHEST

D_MODEL = 2048
D_SSM = D_MODEL // 2
SSM_GROUP = 16
N_GROUPS = D_SSM // SSM_GROUP
SSM_STATE = 64
HEAD_DIM = 128
N_HEADS = (D_MODEL // 2) // HEAD_DIM
D_ATT = N_HEADS * HEAD_DIM
D_FF = ((8 * D_MODEL + 3 * 256 - 1) // (3 * 256)) * 256
PAGE_SIZE = 128
RMS_EPS = 1e-6
NEG_INF = -1e30

LANES = 128
GROUPS_PER_BLOCK = LANES // SSM_GROUP
N_BLOCKS = N_GROUPS // GROUPS_PER_BLOCK
BLOCK_STATE = GROUPS_PER_BLOCK * SSM_STATE
CHUNK = 16
PROJ_COLS = 2048
GATE_SHIFT = N_HEADS
ATT_TK = 512
ATT_TQ = 2 * ATT_TK
VMEM_LIMIT = 56 * 1024 * 1024

LOG2E = math.log2(math.e)
QK_SCALE_LOG2 = (HEAD_DIM ** -0.5) * LOG2E

_NT = (((1,), (1,)), ((), ()))


def _params(sem):
    return pltpu.CompilerParams(dimension_semantics=sem, vmem_limit_bytes=VMEM_LIMIT)


def _gelu_exact(x):
    return 0.5 * x * (1.0 + lax.erf(x * math.sqrt(0.5)))


def _sigmoid(x):
    return 1.0 / (1.0 + jnp.exp(-x))


def _fold_lanes(x, op):
    acc = x[:, 0:LANES]
    for c in range(1, x.shape[1] // LANES):
        acc = op(acc, x[:, c * LANES:(c + 1) * LANES])
    return acc


def _row_max(x):
    return jnp.max(_fold_lanes(x, jnp.maximum), axis=-1, keepdims=True)


def _row_sum(x):
    return jnp.sum(_fold_lanes(x, jnp.add), axis=-1, keepdims=True)


def _log_sigmoid(x):
    return jnp.minimum(x, 0.0) - jnp.log1p(jnp.exp(-jnp.abs(x)))


def _head_norm(z, gain):
    outs = []
    for h in range(N_HEADS):
        blk = z[:, h * HEAD_DIM:(h + 1) * HEAD_DIM]
        ms = jnp.mean(blk * blk, axis=-1, keepdims=True)
        outs.append(blk * lax.rsqrt(ms + RMS_EPS) * gain)
    return jnp.concatenate(outs, axis=1)


def _proj_kernel(*refs, kind, nt, v_transposed):
    za, zb = refs[-2:]
    s = pl.program_id(0)
    half = PROJ_COLS // 2

    if kind == "uq":
        x_ref, g_ref, w_ref, qg_ref, xn_ref, u_ref, qb_ref = refs[:-2]

        def lhs():
            x = x_ref[...]
            ms = jnp.mean(x * x, axis=-1, keepdims=True)
            xn = (x * lax.rsqrt(ms + RMS_EPS) * g_ref[...]).astype(BF16)
            xn_ref[...] = xn
            return xn

        def epilogue(z):
            u_ref[...] = z[:, 0:half]
            qb_ref[...] = (_head_norm(z[:, half:], qg_ref[...]) * QK_SCALE_LOG2).astype(BF16)
    elif kind == "kv":
        xn_ref, w_ref, kg_ref, k_ref, kb_ref, v_ref, vb_ref = refs[:-2]
        lhs = lambda: xn_ref[...]

        def epilogue(z):
            kn = _head_norm(z[:, 0:half], kg_ref[...])
            k_ref[...] = kn
            kb_ref[...] = kn.astype(BF16)
            v = z[:, half:]
            v_ref[...] = v
            if v_transposed:
                vt = v.T
                for c in range(vb_ref.shape[0]):
                    vb_ref[c] = vt[:, c * ATT_TK:(c + 1) * ATT_TK].astype(BF16)
            else:
                vb_ref[...] = v.astype(BF16)
    elif kind == "gate_ssm":
        xn_ref, w_ref, bf_ref, gate_ref, logf_ref = refs[:-2]
        lhs = lambda: xn_ref[...]

        def epilogue(z):
            gate_ref[...] = _sigmoid(z).astype(BF16)
            logf_ref[...] = _log_sigmoid(z[:, 0:LANES] + bf_ref[...])
    else:
        xn_ref, w_ref, wt_ref, gate_ref, tail_ref = refs[:-2]

        def lhs():
            xn = xn_ref[...]
            tail_ref[...] = _sigmoid(jnp.dot(xn, wt_ref[...], preferred_element_type=F32)).astype(BF16)
            return xn

        def epilogue(z):
            gate_ref[...] = _sigmoid(z).astype(BF16)

    @pl.when(s == 0)
    def _():
        zb[...] = jnp.zeros_like(zb)

    def body(z_write, z_read):
        epilogue(z_read[...])
        if z_write is not None:
            z_write[...] = jnp.dot(lhs(), w_ref[...], preferred_element_type=F32)

    pl.when((s < nt) & (s % 2 == 0))(lambda: body(za, zb))
    pl.when((s < nt) & (s % 2 == 1))(lambda: body(zb, za))
    pl.when(s == nt)(lambda: body(None, za if (nt - 1) % 2 == 0 else zb))


def _proj_call(kind, inputs, in_specs, out_shape, out_specs, tm, nt, v_transposed=False):
    return pl.pallas_call(
        functools.partial(_proj_kernel, kind=kind, nt=nt, v_transposed=v_transposed),
        out_shape=out_shape,
        grid=(nt + 1,),
        in_specs=in_specs,
        out_specs=out_specs,
        scratch_shapes=[pltpu.VMEM((tm, PROJ_COLS), F32), pltpu.VMEM((tm, PROJ_COLS), F32)],
        compiler_params=_params(("arbitrary",)),
        name="proj_" + kind,
    )(*inputs)


def _inproj(x, norm_g, w_in_b, w_tail, b_f, q_g, k_g, tm, v_transposed):
    t = x.shape[0]
    nt = t // tm
    cur = lambda s: (jnp.minimum(s, nt - 1), 0)
    lag = lambda s: (jnp.maximum(s - 1, 0), 0)
    const = lambda s: (0, 0)
    slab = lambda c: pl.BlockSpec((D_MODEL, PROJ_COLS), lambda s: (0, c), pipeline_mode=pl.Buffered(1))
    rows = lambda width, index: pl.BlockSpec((tm, width), index)
    sds = jax.ShapeDtypeStruct

    xn, u, qb = _proj_call(
        "uq", (x, norm_g, w_in_b, q_g),
        [rows(D_MODEL, cur), pl.BlockSpec((1, D_MODEL), const), slab(0), pl.BlockSpec((1, HEAD_DIM), const)],
        (sds((t, D_MODEL), BF16), sds((t, D_SSM), F32), sds((t, D_ATT), BF16)),
        (rows(D_MODEL, cur), rows(D_SSM, lag), rows(D_ATT, lag)), tm, nt)

    if v_transposed:
        vb_shape = sds((t // ATT_TK, D_ATT, ATT_TK), BF16)
        vb_spec = pl.BlockSpec((tm // ATT_TK, D_ATT, ATT_TK), lambda s: (jnp.maximum(s - 1, 0), 0, 0))
    else:
        vb_shape = sds((t, D_ATT), BF16)
        vb_spec = rows(D_ATT, lag)
    k, kb, v, vb = _proj_call(
        "kv", (xn, w_in_b, k_g),
        [rows(D_MODEL, cur), slab(1), pl.BlockSpec((1, HEAD_DIM), const)],
        (sds((t, D_ATT), F32), sds((t, D_ATT), BF16), sds((t, D_ATT), F32), vb_shape),
        (rows(D_ATT, lag), rows(D_ATT, lag), rows(D_ATT, lag), vb_spec), tm, nt, v_transposed)

    gate_s, logf = _proj_call(
        "gate_ssm", (xn, w_in_b, b_f),
        [rows(D_MODEL, cur), slab(2), pl.BlockSpec((1, LANES), const)],
        (sds((t, PROJ_COLS), BF16), sds((t, LANES), F32)),
        (rows(PROJ_COLS, lag), rows(LANES, lag)), tm, nt)

    gate_a, gate_t = _proj_call(
        "gate_att", (xn, w_in_b, w_tail),
        [rows(D_MODEL, cur), slab(3), pl.BlockSpec((D_MODEL, LANES), const)],
        (sds((t, PROJ_COLS), BF16), sds((t, LANES), BF16)),
        (rows(PROJ_COLS, lag), rows(LANES, cur)), tm, nt)
    return u, qb, k, kb, v, vb, (gate_s, gate_a, gate_t), logf


CUM_ROWS = 256


def _cumaug_kernel(lf_ref, eq_ref, ek_ref, caq_ref, cak_ref, carry_sc):
    @pl.when(pl.program_id(1) == 0)
    def _():
        carry_sc[...] = jnp.zeros_like(carry_sc)

    lf = lf_ref[...]
    r = lax.broadcasted_iota(jnp.int32, (CUM_ROWS, CUM_ROWS), 0)
    c = lax.broadcasted_iota(jnp.int32, (CUM_ROWS, CUM_ROWS), 1)
    tri = (c <= r).astype(BF16)
    lf_hi = lf.astype(BF16)
    lf_r = lf - lf_hi.astype(F32)
    lf_mid = lf_r.astype(BF16)
    lf_lo = (lf_r - lf_mid.astype(F32)).astype(BF16)
    parts = jnp.dot(tri, jnp.concatenate([lf_hi, lf_mid, lf_lo], axis=1), preferred_element_type=F32)
    cs = (parts[:, 0:LANES] + parts[:, LANES:2 * LANES] + parts[:, 2 * LANES:3 * LANES]) + carry_sc[...]
    carry_sc[...] = cs[CUM_ROWS - 1:CUM_ROWS, :]
    cs2 = cs * LOG2E
    hi = cs2.astype(BF16).astype(F32)
    r1 = cs2 - hi
    mid = r1.astype(BF16).astype(F32)
    lo = (r1 - mid).astype(BF16).astype(F32)
    lane = lax.broadcasted_iota(jnp.int32, (CUM_ROWS, LANES), 1)
    packed = jnp.where(lane < N_HEADS, hi,
                       jnp.where(lane < 2 * N_HEADS, pltpu.roll(mid, N_HEADS, axis=1),
                                 jnp.where(lane < 3 * N_HEADS, pltpu.roll(lo, 2 * N_HEADS, axis=1),
                                           jnp.where(lane == 3 * N_HEADS, 1.0, 0.0)))).astype(BF16)
    caq_ref[...] = jnp.dot(packed, eq_ref[...], preferred_element_type=F32).astype(BF16)
    cak_ref[...] = jnp.dot(packed, ek_ref[...], preferred_element_type=F32).astype(BF16)


def _placement_matrices():
    eq = np.zeros((LANES, D_ATT), np.float32)
    ek = np.zeros((LANES, D_ATT), np.float32)
    for h in range(N_HEADS):
        base = h * HEAD_DIM
        for piece in range(3):
            eq[piece * N_HEADS + h, base + piece] = 1.0
            ek[piece * N_HEADS + h, base + 3 + piece] = -1.0
            eq[3 * N_HEADS, base + 3 + piece] = 1.0
            ek[3 * N_HEADS, base + piece] = 1.0
    return jnp.asarray(eq, BF16), jnp.asarray(ek, BF16)


def _cumaug(logf, batch, seq):
    nb = seq // CUM_ROWS
    eq, ek = _placement_matrices()
    const = lambda: pl.BlockSpec((LANES, D_ATT), lambda b, i: (0, 0))
    return pl.pallas_call(
        _cumaug_kernel,
        out_shape=(jax.ShapeDtypeStruct((batch * seq, D_ATT), BF16),
                   jax.ShapeDtypeStruct((batch * seq, D_ATT), BF16)),
        grid=(batch, nb),
        in_specs=[pl.BlockSpec((CUM_ROWS, LANES), lambda b, i: (b * nb + i, 0)), const(), const()],
        out_specs=(pl.BlockSpec((CUM_ROWS, D_ATT), lambda b, i: (b * nb + i, 0)),
                   pl.BlockSpec((CUM_ROWS, D_ATT), lambda b, i: (b * nb + i, 0))),
        scratch_shapes=[pltpu.VMEM((1, LANES), F32)],
        compiler_params=_params(("arbitrary", "arbitrary")),
        name="cumaug",
    )(logf, eq, ek)


ACC_ROWS = HEAD_DIM + 16


def _attn_kernel(q_ref, cq_ref, k_ref, ck_ref, vt_ref, o_ref, m_sc, acc_sc, sa_sc, sb_sc):
    i = pl.program_id(2)
    tk = ATT_TK
    m_sc[...] = jnp.full(m_sc.shape, NEG_INF, F32)
    acc_sc[...] = jnp.zeros_like(acc_sc)
    sub = lax.broadcasted_iota(jnp.int32, (ACC_ROWS - HEAD_DIM, tk), 0)
    ones_rows = jnp.where(sub == 0, 1.0, 0.0).astype(BF16)

    def logits(j, s_ref, lo):
        ks = pl.multiple_of(j * tk, tk)
        kk = jnp.concatenate([k_ref[pl.ds(ks, tk), :], ck_ref[pl.ds(ks, tk), :]], axis=1)
        qq = jnp.concatenate([q_ref[lo:, :], cq_ref[lo:, :]], axis=1)
        s_ref[:, lo:] = lax.dot_general(kk, qq, _NT, preferred_element_type=F32)

    def consume(j, s_ref, lo, masked):
        s = s_ref[:, lo:]
        if masked:
            key = lax.broadcasted_iota(jnp.int32, s.shape, 0)
            qry = lax.broadcasted_iota(jnp.int32, s.shape, 1)
            s = jnp.where(key <= qry, s, NEG_INF)
        vt = jnp.concatenate([vt_ref[j], ones_rows], axis=0)
        m_old = m_sc[:, lo:]
        m_new = jnp.maximum(m_old, jnp.max(s, axis=0, keepdims=True))
        alpha = jnp.exp2(m_old - m_new)
        p = jnp.exp2(s - m_new).astype(BF16)
        acc_sc[:, lo:] = alpha * acc_sc[:, lo:] + jnp.dot(vt, p, preferred_element_type=F32)
        m_sc[:, lo:] = m_new

    logits(0, sa_sc, 0)

    def pair(jj, carry):
        logits(2 * jj + 1, sb_sc, 0)
        consume(2 * jj, sa_sc, 0, False)
        logits(2 * jj + 2, sa_sc, 0)
        consume(2 * jj + 1, sb_sc, 0, False)
        return carry

    lax.fori_loop(0, i, pair, 0)
    logits(2 * i + 1, sb_sc, tk)
    consume(2 * i, sa_sc, 0, True)
    consume(2 * i + 1, sb_sc, tk, True)

    acc = acc_sc[...]
    out_t = acc[0:HEAD_DIM, :] / acc[HEAD_DIM:HEAD_DIM + 1, :]
    o_ref[...] = out_t.T.astype(BF16)


def _attention(qb, caq, kb, cak, vt, batch, seq):
    nq = seq // ATT_TQ
    nkb = seq // ATT_TK
    qspec = lambda: pl.BlockSpec((ATT_TQ, HEAD_DIM), lambda b, h, i: (b * nq + i, h))
    kspec = lambda: pl.BlockSpec((seq, HEAD_DIM), lambda b, h, i: (b, h))
    return pl.pallas_call(
        _attn_kernel,
        out_shape=jax.ShapeDtypeStruct((batch * seq, D_ATT), BF16),
        grid=(batch, N_HEADS, nq),
        in_specs=[qspec(), qspec(), kspec(), kspec(),
                  pl.BlockSpec((nkb, HEAD_DIM, ATT_TK), lambda b, h, i: (b, h, 0))],
        out_specs=qspec(),
        scratch_shapes=[pltpu.VMEM((1, ATT_TQ), F32), pltpu.VMEM((ACC_ROWS, ATT_TQ), F32),
                        pltpu.VMEM((ATT_TK, ATT_TQ), F32), pltpu.VMEM((ATT_TK, ATT_TQ), F32)],
        compiler_params=_params(("arbitrary", "arbitrary", "arbitrary")),
        name="fox_prompt",
    )(qb, caq, kb, cak, vt)


def _discretise(a_re, a_im, log_dt):
    dt = jnp.exp(log_dt)
    mag = jnp.exp(dt * a_re)
    ang = dt * a_im
    abr = mag * jnp.cos(ang)
    abi = mag * jnp.sin(ang)
    e_re = abr - 1.0
    e_im = abi
    inv_den = 1.0 / (a_re * a_re + a_im * a_im)
    f_re = (e_re * a_re + e_im * a_im) * inv_den
    f_im = (e_im * a_re - e_re * a_im) * inv_den
    return abr, abi, f_re, f_im


def _powers(abr, abi, n):
    pr, pi = [jnp.ones_like(abr)], [jnp.zeros_like(abi)]
    for _ in range(n):
        r, i = pr[-1], pi[-1]
        pr.append(r * abr - i * abi)
        pi.append(r * abi + i * abr)
    return pr, pi


def _s5gen_kernel(ar_row, ai_row, dt_row, bre_ref, bim_ref, cre_ref, cim_ref,
                  vrev_ref, bsum_ref, m_ref, bmat_ref, cmat_ref, a1_ref, a16_ref):
    ns = BLOCK_STATE
    abr, abi, f_re, f_im = _discretise(ar_row[...], ai_row[...], dt_row[...])
    bre, bim = bre_ref[...], bim_ref[...]
    bbr = f_re * bre - f_im * bim
    bbi = f_re * bim + f_im * bre
    bmat = jnp.concatenate([bbr, bbi], axis=1)
    bmat_ref[...] = bmat
    pr, pi = _powers(abr, abi, CHUNK)
    a1_ref[...] = jnp.concatenate([pr[1], pi[1]], axis=1)
    a16_ref[...] = jnp.concatenate([pr[CHUNK], pi[CHUNK]], axis=1)
    for i in range(CHUNK):
        r, im = pr[CHUNK - 1 - i], pi[CHUNK - 1 - i]
        blk = jnp.concatenate([r * bbr - im * bbi, r * bbi + im * bbr], axis=1)
        bsum_ref[i * LANES:(i + 1) * LANES, :] = blk.astype(BF16)
    npow = CHUNK + 1
    stacked = jnp.concatenate(pr + pi + [jnp.zeros((LANES - 2 * npow, ns), F32)], axis=0)
    pt = stacked.T
    qr = [pt[:, t:t + 1] for t in range(npow)]
    qi = [pt[:, npow + t:npow + t + 1] for t in range(npow)]
    cre, cim = cre_ref[...], cim_ref[...]
    vrev_ref[(CHUNK - 1) * LANES:CHUNK * LANES, 0:LANES] = jnp.zeros((LANES, LANES), BF16)
    bmat_hi = bmat.astype(BF16)
    for tau in range(CHUNK + 1):
        blk = jnp.concatenate([cre * qr[tau] - cim * qi[tau],
                               -(cre * qi[tau] + cim * qr[tau])], axis=0)
        if tau == 0:
            cmat_ref[...] = blk
        else:
            m_ref[:, (tau - 1) * LANES:tau * LANES] = blk.astype(BF16)
        if tau < CHUNK:
            w = jnp.dot(bmat_hi, blk.astype(BF16), preferred_element_type=F32).astype(BF16)
            k = CHUNK - 1 - tau
            vrev_ref[k * LANES:(k + 1) * LANES, LANES:2 * LANES] = w
            if k >= 1:
                vrev_ref[(k - 1) * LANES:k * LANES, 0:LANES] = w


def _s5gen(ar_row, ai_row, dt_row, bd_bre, bd_bim, bd_cre, bd_cim):
    ns = BLOCK_STATE
    b3 = lambda s1, s2: pl.BlockSpec((None, s1, s2), lambda o: (o, 0, 0))
    out_shape = (
        jax.ShapeDtypeStruct((N_BLOCKS, CHUNK * LANES, 2 * LANES), BF16),
        jax.ShapeDtypeStruct((N_BLOCKS, CHUNK * LANES, 2 * ns), BF16),
        jax.ShapeDtypeStruct((N_BLOCKS, 2 * ns, CHUNK * LANES), BF16),
        jax.ShapeDtypeStruct((N_BLOCKS, LANES, 2 * ns), F32),
        jax.ShapeDtypeStruct((N_BLOCKS, 2 * ns, LANES), F32),
        jax.ShapeDtypeStruct((N_BLOCKS, 1, 2 * ns), F32),
        jax.ShapeDtypeStruct((N_BLOCKS, 1, 2 * ns), F32),
    )
    return pl.pallas_call(
        _s5gen_kernel,
        out_shape=out_shape,
        grid=(N_BLOCKS,),
        in_specs=[b3(1, ns), b3(1, ns), b3(1, ns),
                  b3(LANES, ns), b3(LANES, ns), b3(ns, LANES), b3(ns, LANES)],
        out_specs=(b3(CHUNK * LANES, 2 * LANES), b3(CHUNK * LANES, 2 * ns), b3(2 * ns, CHUNK * LANES),
                   b3(LANES, 2 * ns), b3(2 * ns, LANES), b3(1, 2 * ns), b3(1, 2 * ns)),
        compiler_params=_params(("arbitrary",)),
        name="s5_operators",
    )(ar_row, ai_row, dt_row, bd_bre, bd_bim, bd_cre, bd_cim)


def _s5_prompt_kernel(u_ref, vrev_ref, bsum_ref, m_ref, a16_ref, d_ref,
                      y_ref, hre_ref, him_ref, ucat_sc, s_sc, hin_sc, *, n_chunks):
    ns = BLOCK_STATE
    for i in range(CHUNK):
        ucat_sc[:, i * LANES:(i + 1) * LANES] = u_ref[pl.ds(i, n_chunks, stride=CHUNK), :].astype(BF16)
    s_sc[...] = jnp.dot(ucat_sc[...], bsum_ref[...], preferred_element_type=F32)
    ar = a16_ref[:, 0:ns]
    ai = a16_ref[:, ns:2 * ns]

    def step(k, carry):
        hr, hi = carry
        hin_sc[pl.ds(k, 1), 0:ns] = hr
        hin_sc[pl.ds(k, 1), ns:2 * ns] = hi
        sr = s_sc[pl.ds(k, 1), 0:ns]
        si = s_sc[pl.ds(k, 1), ns:2 * ns]
        return ar * hr - ai * hi + sr, ar * hi + ai * hr + si

    zero = jnp.zeros((1, ns), F32)
    hr, hi = lax.fori_loop(0, n_chunks, step, (zero, zero))
    hre_ref[...] = hr
    him_ref[...] = hi
    hin = hin_sc[...].astype(BF16)
    d = d_ref[...]
    for jp in range(CHUNK // 2):
        j = 2 * jp
        kk = (j + 2) * LANES
        acc = jnp.dot(ucat_sc[:, 0:kk], vrev_ref[(CHUNK - 2 - j) * LANES:, :],
                      preferred_element_type=F32)
        acc = acc + jnp.dot(hin, m_ref[:, j * LANES:(j + 2) * LANES], preferred_element_type=F32)
        for jj in range(2):
            uj = u_ref[pl.ds(j + jj, n_chunks, stride=CHUNK), :]
            val = acc[:, jj * LANES:(jj + 1) * LANES] + d * uj
            y_ref[pl.ds(j + jj, n_chunks, stride=CHUNK), :] = _gelu_exact(val)


def _s5_prompt(u, vrev, bsum, m, a16, d, batch, seq):
    ns = BLOCK_STATE
    n_chunks = seq // CHUNK
    w3 = lambda s1, s2: pl.BlockSpec((None, s1, s2), lambda o, b: (o, 0, 0))
    return pl.pallas_call(
        functools.partial(_s5_prompt_kernel, n_chunks=n_chunks),
        out_shape=(jax.ShapeDtypeStruct((batch * seq, D_SSM), F32),
                   jax.ShapeDtypeStruct((batch, 1, N_GROUPS * SSM_STATE), F32),
                   jax.ShapeDtypeStruct((batch, 1, N_GROUPS * SSM_STATE), F32)),
        grid=(N_BLOCKS, batch),
        in_specs=[pl.BlockSpec((seq, LANES), lambda o, b: (b, o)),
                  w3(CHUNK * LANES, 2 * LANES), w3(CHUNK * LANES, 2 * ns), w3(2 * ns, CHUNK * LANES),
                  w3(1, 2 * ns),
                  pl.BlockSpec((1, LANES), lambda o, b: (0, o))],
        out_specs=(pl.BlockSpec((seq, LANES), lambda o, b: (b, o)),
                   pl.BlockSpec((None, 1, ns), lambda o, b: (b, 0, o)),
                   pl.BlockSpec((None, 1, ns), lambda o, b: (b, 0, o))),
        scratch_shapes=[pltpu.VMEM((n_chunks, CHUNK * LANES), BF16),
                        pltpu.VMEM((n_chunks, 2 * ns), F32),
                        pltpu.VMEM((n_chunks, 2 * ns), F32)],
        compiler_params=_params(("arbitrary", "arbitrary")),
        name="s5_prompt",
    )(u, vrev, bsum, m, a16, d)


def _s5_sample_kernel(u_ref, h0r_ref, h0i_ref, bmat_ref, cmat_ref, a1_ref, d_ref,
                      y_ref, hre_ref, him_ref, *, n_seq, n_tok):
    ns = BLOCK_STATE
    u = u_ref[...]
    bu = jnp.dot(u, bmat_ref[...], precision=HIGHEST, preferred_element_type=F32)
    ar = a1_ref[:, 0:ns]
    ai = a1_ref[:, ns:2 * ns]
    hr, hi = h0r_ref[...], h0i_ref[...]
    hs = []
    for t in range(n_tok):
        br = bu[t * n_seq:(t + 1) * n_seq, 0:ns]
        bi = bu[t * n_seq:(t + 1) * n_seq, ns:2 * ns]
        hr, hi = ar * hr - ai * hi + br, ar * hi + ai * hr + bi
        hs.append(jnp.concatenate([hr, hi], axis=1))
    hcat = jnp.concatenate(hs, axis=0)
    val = jnp.dot(hcat, cmat_ref[...], precision=HIGHEST, preferred_element_type=F32) + d_ref[...] * u
    y_ref[...] = _gelu_exact(val)
    hre_ref[...] = hr
    him_ref[...] = hi


def _s5_sample(u, h0r, h0i, bmat, cmat, a1, d, n_seq, n_tok):
    ns = BLOCK_STATE
    rows = n_seq * n_tok
    w3 = lambda s1, s2: pl.BlockSpec((None, s1, s2), lambda o: (o, 0, 0))
    col = lambda r, c: pl.BlockSpec((r, c), lambda o: (0, o))
    return pl.pallas_call(
        functools.partial(_s5_sample_kernel, n_seq=n_seq, n_tok=n_tok),
        out_shape=(jax.ShapeDtypeStruct((rows, D_SSM), F32),
                   jax.ShapeDtypeStruct((n_seq, N_GROUPS * SSM_STATE), F32),
                   jax.ShapeDtypeStruct((n_seq, N_GROUPS * SSM_STATE), F32)),
        grid=(N_BLOCKS,),
        in_specs=[col(rows, LANES), col(n_seq, ns), col(n_seq, ns),
                  w3(LANES, 2 * ns), w3(2 * ns, LANES), w3(1, 2 * ns), col(1, LANES)],
        out_specs=(col(rows, LANES), col(n_seq, ns), col(n_seq, ns)),
        compiler_params=_params(("arbitrary",)),
        name="s5_sample",
    )(u, h0r, h0i, bmat, cmat, a1, d)


PAGES_PER_STEP = 16
PAGE_GROUP = 4

PAGE_KEYS = PAGE_SIZE * N_HEADS


def _decode_kernel(pt_ref, q_ref, lfrow_ref, lfcol_ref, kn_ref, vn_ref, *refs, n_tok, n_steps):
    g = PAGES_PER_STEP
    k_refs = refs[0:g]
    v_refs = refs[g:2 * g]
    lf_refs = refs[2 * g:3 * g]
    o_ref = refs[3 * g]
    m_sc, l_sc, acc_sc, carry_sc = refs[3 * g + 1:]
    j = pl.program_id(1)
    rows = n_tok * N_HEADS
    head_mask = N_HEADS - 1
    head_shift = N_HEADS.bit_length() - 1

    @pl.when(j == 0)
    def _():
        m_sc[...] = jnp.full(m_sc.shape, NEG_INF, F32)
        l_sc[...] = jnp.zeros_like(l_sc)
        acc_sc[...] = jnp.zeros_like(acc_sc)
        carry_sc[...] = jnp.zeros_like(carry_sc)

    q = q_ref[...]
    lfcol = lfcol_ref[...]
    run = jnp.zeros((N_HEADS, 1), F32)
    pieces = []
    for t in range(n_tok):
        run = run + lfcol[t * N_HEADS:(t + 1) * N_HEADS, :]
        pieces.append(run)
    c_col = jnp.concatenate(pieces, axis=0)

    def online(s_blocks, v_blocks):
        m = m_sc[...]
        smax = s_blocks[0]
        for s in s_blocks[1:]:
            smax = jnp.maximum(smax, s)
        m_new = jnp.maximum(m, _row_max(smax))
        alpha = jnp.exp2(m - m_new)
        psum = None
        pv = None
        for s, vblk in zip(s_blocks, v_blocks):
            p = jnp.exp2(s - m_new)
            psum = p if psum is None else psum + p
            d = jnp.dot(p.astype(BF16), vblk, preferred_element_type=F32)
            pv = d if pv is None else pv + d
        l_sc[...] = alpha * l_sc[...] + _row_sum(psum)
        acc_sc[...] = alpha * acc_sc[...] + pv
        m_sc[...] = m_new

    col = lax.broadcasted_iota(jnp.int32, (rows, PAGE_KEYS), 1)
    row = lax.broadcasted_iota(jnp.int32, (rows, PAGE_KEYS), 0)
    own = (col & head_mask) == (row & head_mask)
    fixed = jnp.where(own, c_col * LOG2E, NEG_INF)

    lf = jnp.concatenate([lf_refs[pg][...] for pg in range(g)], axis=0)
    lane = lax.broadcasted_iota(jnp.int32, (g, PAGE_KEYS), 1)
    suffix = lf
    total = lf
    sh = N_HEADS
    while sh < PAGE_KEYS:
        suffix = suffix + jnp.where(lane + sh < PAGE_KEYS, pltpu.roll(suffix, PAGE_KEYS - sh, axis=1), 0.0)
        total = total + pltpu.roll(total, PAGE_KEYS - sh, axis=1)
        sh *= 2
    carry = carry_sc[...]
    past = []
    for pg in range(g):
        past.append((suffix[pg:pg + 1, :] - lf[pg:pg + 1, :] + carry) * LOG2E)
        carry = carry + total[pg:pg + 1, :]
    carry_sc[...] = carry

    def logits(pages):
        out = []
        for pg in pages:
            kp = k_refs[pg][...].reshape(PAGE_KEYS, HEAD_DIM).astype(BF16)
            s = lax.dot_general(q, kp, _NT, preferred_element_type=F32)
            out.append(s + fixed + past[pg])
        return out

    def values(pages):
        return [v_refs[pg][...].reshape(PAGE_KEYS, HEAD_DIM).astype(BF16) for pg in pages]

    groups = [list(range(a, a + PAGE_GROUP)) for a in range(0, g, PAGE_GROUP)]
    s_next = logits(groups[0])
    for gi, pages in enumerate(groups):
        s_cur = s_next
        if gi + 1 < len(groups):
            s_next = logits(groups[gi + 1])
        online(s_cur, values(pages))

    @pl.when(j == n_steps - 1)
    def _():
        ln = lax.broadcasted_iota(jnp.int32, (1, LANES), 1)
        c_row = lfrow_ref[...]
        sh2 = N_HEADS
        while sh2 < rows:
            c_row = c_row + jnp.where(ln >= sh2, pltpu.roll(c_row, sh2, axis=1), 0.0)
            sh2 *= 2
        cl = lax.broadcasted_iota(jnp.int32, (rows, LANES), 1)
        rw = lax.broadcasted_iota(jnp.int32, (rows, LANES), 0)
        valid = ((cl < rows) & ((cl & head_mask) == (rw & head_mask))
                 & ((cl >> head_shift) <= (rw >> head_shift)))
        s = lax.dot_general(q, kn_ref[...], _NT, preferred_element_type=F32)
        s = jnp.where(valid, s + (c_col - c_row) * LOG2E, NEG_INF)
        online([s], [vn_ref[...]])
        o_ref[...] = acc_sc[...] / l_sc[...]


def _decode_attention(page_table, q, lfrow, lfcol, kn, vn, cache_k, cache_v, cache_lf, n_tok):
    n_seq, n_pages = page_table.shape
    g = PAGES_PER_STEP
    n_steps = n_pages // g
    rows = n_tok * N_HEADS

    def page5(pg):
        return lambda b, j, pt: (0, pt[b * n_pages + (n_pages - 1 - (j * g + pg))], 0, 0, 0)

    def page3(pg):
        return lambda b, j, pt: (pt[b * n_pages + (n_pages - 1 - (j * g + pg))], 0, 0)

    seq3 = lambda s1, s2: pl.BlockSpec((None, s1, s2), lambda b, j, pt: (b, 0, 0))
    kv_block = (None, None, PAGE_SIZE, N_HEADS, HEAD_DIM)
    in_specs = [seq3(rows, HEAD_DIM), seq3(1, LANES), seq3(rows, 1), seq3(LANES, HEAD_DIM), seq3(LANES, HEAD_DIM)]
    in_specs += [pl.BlockSpec(kv_block, page5(pg)) for pg in range(g)]
    in_specs += [pl.BlockSpec(kv_block, page5(pg)) for pg in range(g)]
    in_specs += [pl.BlockSpec((None, 1, PAGE_KEYS), page3(pg)) for pg in range(g)]
    grid_spec = pltpu.PrefetchScalarGridSpec(
        num_scalar_prefetch=1,
        grid=(n_seq, n_steps),
        in_specs=in_specs,
        out_specs=pl.BlockSpec((None, rows, HEAD_DIM), lambda b, j, pt: (b, 0, 0)),
        scratch_shapes=[pltpu.VMEM((rows, 1), F32),
                        pltpu.VMEM((rows, 1), F32),
                        pltpu.VMEM((rows, HEAD_DIM), F32),
                        pltpu.VMEM((1, PAGE_KEYS), F32)],
    )
    return pl.pallas_call(
        functools.partial(_decode_kernel, n_tok=n_tok, n_steps=n_steps),
        out_shape=jax.ShapeDtypeStruct((n_seq, rows, HEAD_DIM), F32),
        grid_spec=grid_spec,
        compiler_params=_params(("arbitrary", "arbitrary")),
        name="fox_sample",
    )(page_table.reshape(-1), q, lfrow, lfcol, kn, vn,
      *([cache_k] * g), *([cache_v] * g), *([cache_lf] * g))


def _merge_kernel(y_ref, ya_ref, gs_ref, ga_ref, gt_ref, x_ref, wglu_ref, bglu_ref, wbs_ref, wba_ref, wo_ref,
                  ng_ref, x1_ref, xn_ref):
    y = y_ref[...]
    gl = jnp.dot(y.astype(BF16), wglu_ref[...], preferred_element_type=F32) + bglu_ref[...]
    ys = (y * _sigmoid(gl)).astype(BF16)
    ms = jnp.dot(ys, wbs_ref[...], preferred_element_type=F32)
    ma = jnp.dot(ya_ref[...], wba_ref[...], preferred_element_type=F32)
    ga = ga_ref[...]
    gate_s = jnp.concatenate([gs_ref[...], ga[:, 0:LANES]], axis=1).astype(F32)
    gate_a = jnp.concatenate([ga, gt_ref[...]], axis=1).astype(F32)
    merged = gate_s * ms + gate_a * ma
    x1 = x_ref[...] + jnp.dot(merged.astype(BF16), wo_ref[...], preferred_element_type=F32)
    x1_ref[...] = x1
    ms1 = jnp.mean(x1 * x1, axis=-1, keepdims=True)
    xn_ref[...] = (x1 * lax.rsqrt(ms1 + RMS_EPS) * ng_ref[...]).astype(BF16)


MERGE_COLS = D_MODEL + LANES


def _merge(y, ya, gates, x, w_glu, b_glu, w_bs, w_ba, w_o, n_g, tm):
    t = x.shape[0]
    gate_s, gate_a, gate_t = gates
    row = lambda c: pl.BlockSpec((tm, c), lambda i: (i, 0))
    const = lambda r, c: pl.BlockSpec((r, c), lambda i: (0, 0), pipeline_mode=pl.Buffered(1))
    return pl.pallas_call(
        _merge_kernel,
        out_shape=(jax.ShapeDtypeStruct((t, D_MODEL), F32), jax.ShapeDtypeStruct((t, D_MODEL), BF16)),
        grid=(t // tm,),
        in_specs=[row(D_SSM), row(D_ATT), row(PROJ_COLS), row(PROJ_COLS), row(LANES), row(D_MODEL),
                  const(D_SSM, D_SSM), const(1, D_SSM), const(D_SSM, MERGE_COLS), const(D_ATT, MERGE_COLS),
                  const(MERGE_COLS, D_MODEL), const(1, D_MODEL)],
        out_specs=(row(D_MODEL), row(D_MODEL)),
        compiler_params=_params(("arbitrary",)),
        name="merge_out",
    )(y, ya, gate_s, gate_a, gate_t, x, w_glu, b_glu, w_bs, w_ba, w_o, n_g)


FFN_TF = 256


def _ffn_kernel(x1_ref, xn_ref, x1s_ref, xns_ref, wg_ref, wu_ref, wd_ref, o_ref, os_ref, wg_sc, wu_sc, wd_sc):
    i, f = pl.program_id(0), pl.program_id(1)
    @pl.when(f == 0)
    def _():
        o_ref[...] = x1_ref[...]

    xn = xn_ref[...]
    wg_sc[...] = wg_ref[...].astype(BF16)
    a = jnp.dot(xn, wg_sc[...], preferred_element_type=F32)
    wu_sc[...] = wu_ref[...].astype(BF16)
    b = jnp.dot(xn, wu_sc[...], preferred_element_type=F32)
    wd_sc[...] = wd_ref[...].astype(BF16)
    o_ref[...] += jnp.dot((a * _sigmoid(a) * b).astype(BF16), wd_sc[...], preferred_element_type=F32)

    @pl.when(i == pl.num_programs(0) - 1)
    def _():
        @pl.when(f == 0)
        def _():
            os_ref[...] = x1s_ref[...]

        xs = xns_ref[...]
        a_s = jnp.dot(xs, wg_sc[...], preferred_element_type=F32)
        b_s = jnp.dot(xs, wu_sc[...], preferred_element_type=F32)
        os_ref[...] += jnp.dot((a_s * _sigmoid(a_s) * b_s).astype(BF16), wd_sc[...], preferred_element_type=F32)


def _ffn(x1, xn, x1s, xns, w_g, w_u, w_d, tm):
    t = x1.shape[0]
    ts = x1s.shape[0]
    small = lambda: pl.BlockSpec((ts, D_MODEL), lambda i, f: (0, 0))
    return pl.pallas_call(
        _ffn_kernel,
        out_shape=(jax.ShapeDtypeStruct((t, D_MODEL), F32), jax.ShapeDtypeStruct((ts, D_MODEL), F32)),
        grid=(t // tm, D_FF // FFN_TF),
        in_specs=[pl.BlockSpec((tm, D_MODEL), lambda i, f: (i, 0)),
                  pl.BlockSpec((tm, D_MODEL), lambda i, f: (i, 0), pipeline_mode=pl.Buffered(1)),
                  small(), small(),
                  pl.BlockSpec((D_MODEL, FFN_TF), lambda i, f: (0, f)),
                  pl.BlockSpec((D_MODEL, FFN_TF), lambda i, f: (0, f)),
                  pl.BlockSpec((FFN_TF, D_MODEL), lambda i, f: (f, 0))],
        out_specs=(pl.BlockSpec((tm, D_MODEL), lambda i, f: (i, 0)), small()),
        scratch_shapes=[pltpu.VMEM((D_MODEL, FFN_TF), BF16), pltpu.VMEM((D_MODEL, FFN_TF), BF16),
                        pltpu.VMEM((FFN_TF, D_MODEL), BF16)],
        compiler_params=_params(("arbitrary", "arbitrary")),
        name="ffn",
    )(x1, xn, x1s, xns, w_g, w_u, w_d)


def _block_diag_lanes(p):
    p4 = p.reshape(N_BLOCKS, GROUPS_PER_BLOCK, SSM_STATE, SSM_GROUP)
    eye = jnp.eye(GROUPS_PER_BLOCK, dtype=p.dtype)
    out = jnp.einsum("ohsc,gh->ogchs", p4, eye)
    return out.reshape(N_BLOCKS, LANES, BLOCK_STATE)


def kernel(x_prompt, x_sample, cache_k, cache_v, cache_logf, state_ssm_re, state_ssm_im, page_table,
           norm_mix_g, w_in, b_f, q_norm_g, k_norm_g, ssm_a_re, ssm_a_im, ssm_log_dt, ssm_b_re, ssm_b_im,
           ssm_c_re, ssm_c_im, ssm_d, w_glu, b_glu, w_br_ssm, w_br_att, w_out, norm_ffn_g,
           w_ffn_gate, w_ffn_up, w_ffn_down):
    batch, seq, _ = x_prompt.shape
    n_seq, n_tok, _ = x_sample.shape
    l = 0
    n_qkv = D_SSM + 3 * D_ATT

    w_in_b = w_in[l].astype(BF16)
    n_slabs = w_in_b.shape[1] // PROJ_COLS
    n_tail = w_in_b.shape[1] - n_slabs * PROJ_COLS
    assert n_qkv == 2 * PROJ_COLS and n_tail == GATE_SHIFT == N_HEADS
    w_tail = jnp.pad(w_in_b[:, n_slabs * PROJ_COLS:], ((0, 0), (0, LANES - n_tail)))
    b_f_pad = jnp.pad(b_f[l], (0, LANES - N_HEADS)).reshape(1, LANES)
    norm_g = norm_mix_g[l].reshape(1, D_MODEL)
    q_g = q_norm_g[l].reshape(1, HEAD_DIM)
    k_g = k_norm_g[l].reshape(1, HEAD_DIM)
    ns = BLOCK_STATE
    ar_row = ssm_a_re[l].reshape(N_BLOCKS, 1, ns)
    ai_row = ssm_a_im[l].reshape(N_BLOCKS, 1, ns)
    dt_row = jnp.repeat(ssm_log_dt[l], SSM_STATE).reshape(N_BLOCKS, 1, ns)
    bd_bre = _block_diag_lanes(ssm_b_re[l])
    bd_bim = _block_diag_lanes(ssm_b_im[l])
    bd_cre = _block_diag_lanes(ssm_c_re[l].transpose(0, 2, 1)).transpose(0, 2, 1)
    bd_cim = _block_diag_lanes(ssm_c_im[l].transpose(0, 2, 1)).transpose(0, 2, 1)
    d_row = ssm_d[l].reshape(1, D_SSM)
    w_glu_b = w_glu[l].astype(BF16)
    b_glu_r = b_glu[l].reshape(1, D_SSM)
    shift_pad = (GATE_SHIFT, MERGE_COLS - D_MODEL - GATE_SHIFT)
    w_bs = jnp.pad(w_br_ssm[l].astype(BF16), ((0, 0), shift_pad))
    w_ba = jnp.pad(w_br_att[l].astype(BF16), ((0, 0), shift_pad))
    w_o = jnp.pad(w_out[l].astype(BF16), (shift_pad, (0, 0)))
    n_g = norm_ffn_g[l].reshape(1, D_MODEL)
    w_g = w_ffn_gate[l]
    w_u = w_ffn_up[l]
    w_d = w_ffn_down[l]

    vrev, bsum, m_op, bmat, cmat, a1, a16 = _s5gen(ar_row, ai_row, dt_row, bd_bre, bd_bim, bd_cre, bd_cim)

    xp = x_prompt.reshape(batch * seq, D_MODEL)
    u, qb, k, kb, v, vt, gates, logf = _inproj(xp, norm_g, w_in_b, w_tail, b_f_pad, q_g, k_g, tm=512,
                                               v_transposed=True)
    caq, cak = _cumaug(logf, batch, seq)
    y_ssm, hre, him = _s5_prompt(u, vrev, bsum, m_op, a16, d_row, batch, seq)
    y_att = _attention(qb, caq, kb, cak, vt, batch, seq)
    x1, xn1 = _merge(y_ssm, y_att, gates, xp, w_glu_b, b_glu_r, w_bs, w_ba, w_o, n_g, tm=256)

    rows = n_seq * n_tok
    xs = x_sample.transpose(1, 0, 2).reshape(rows, D_MODEL)
    us, qbs, ks, kbs, vs, vbs, gates_s, logf_s = _inproj(xs, norm_g, w_in_b, w_tail, b_f_pad, q_g, k_g, tm=rows,
                                                         v_transposed=False)
    ys_ssm, hre_s, him_s = _s5_sample(us, state_ssm_re[l].reshape(n_seq, -1), state_ssm_im[l].reshape(n_seq, -1),
                                      bmat, cmat, a1, d_row, n_seq, n_tok)
    by_seq = lambda a: a.reshape(n_tok, n_seq, -1).transpose(1, 0, 2)
    th = n_tok * N_HEADS
    th_rows = lambda a: by_seq(a).reshape(n_seq, th, HEAD_DIM)
    pad_keys = lambda a: jnp.pad(th_rows(a), ((0, 0), (0, LANES - th), (0, 0)))
    lf_s = by_seq(logf_s[:, :N_HEADS]).reshape(n_seq, th)
    lfrow = jnp.pad(lf_s, ((0, 0), (0, LANES - th))).reshape(n_seq, 1, LANES)
    lfcol = lf_s.reshape(n_seq, th, 1)
    n_pool = cache_k.shape[1]
    ya_s = _decode_attention(page_table, th_rows(qbs), lfrow, lfcol, pad_keys(kbs), pad_keys(vbs),
                             cache_k, cache_v, cache_logf[l].reshape(n_pool, 1, PAGE_KEYS), n_tok)
    ya_s = ya_s.reshape(n_seq, n_tok, D_ATT).transpose(1, 0, 2).reshape(rows, D_ATT).astype(BF16)
    x1s, xn1s = _merge(ys_ssm, ya_s, gates_s, xs, w_glu_b, b_glu_r, w_bs, w_ba, w_o, n_g, tm=rows)
    y_p, y_s = _ffn(x1, xn1, x1s, xn1s, w_g, w_u, w_d, tm=1024)
    y_p = y_p.reshape(batch, seq, D_MODEL)
    y_s = y_s.reshape(n_tok, n_seq, D_MODEL).transpose(1, 0, 2)

    heads = lambda a, b_, t_: a.reshape(1, b_, t_, N_HEADS, HEAD_DIM)
    tok_major = lambda a: a.reshape(n_tok, n_seq, -1).transpose(1, 0, 2)
    return (
        y_p, y_s,
        heads(k, batch, seq), heads(v, batch, seq),
        logf[:, :N_HEADS].reshape(1, batch, seq, N_HEADS),
        hre.reshape(1, batch, N_GROUPS, SSM_STATE), him.reshape(1, batch, N_GROUPS, SSM_STATE),
        heads(tok_major(ks), n_seq, n_tok), heads(tok_major(vs), n_seq, n_tok),
        tok_major(logf_s[:, :N_HEADS]).reshape(1, n_seq, n_tok, N_HEADS),
        hre_s.reshape(1, n_seq, N_GROUPS, SSM_STATE), him_s.reshape(1, n_seq, N_GROUPS, SSM_STATE),
    )
```

```python
import functools
import math

import jax
import jax.numpy as jnp
import numpy as np
from jax import lax
from jax.experimental import pallas as pl
from jax.experimental.pallas import tpu as pltpu

F32 = jnp.float32
BF16 = jnp.bfloat16
HIGHEST = lax.Precision.HIGHEST

D_MODEL = 2048
D_SSM = D_MODEL // 2
SSM_GROUP = 16
N_GROUPS = D_SSM // SSM_GROUP
SSM_STATE = 64
HEAD_DIM = 128
N_HEADS = (D_MODEL // 2) // HEAD_DIM
D_ATT = N_HEADS * HEAD_DIM
D_FF = ((8 * D_MODEL + 3 * 256 - 1) // (3 * 256)) * 256
PAGE_SIZE = 128
RMS_EPS = 1e-6
NEG_INF = -1e30

LANES = 128
GROUPS_PER_BLOCK = LANES // SSM_GROUP
N_BLOCKS = N_GROUPS // GROUPS_PER_BLOCK
BLOCK_STATE = GROUPS_PER_BLOCK * SSM_STATE
CHUNK = 16
PROJ_COLS = 2048
GATE_SHIFT = N_HEADS
ATT_TK = 512
ATT_TQ = 2 * ATT_TK
VMEM_LIMIT = 56 * 1024 * 1024

LOG2E = math.log2(math.e)
QK_SCALE_LOG2 = (HEAD_DIM ** -0.5) * LOG2E

_NT = (((1,), (1,)), ((), ()))


def _params(sem):
    return pltpu.CompilerParams(dimension_semantics=sem, vmem_limit_bytes=VMEM_LIMIT)


def _gelu_exact(x):
    return 0.5 * x * (1.0 + lax.erf(x * math.sqrt(0.5)))


def _sigmoid(x):
    return 1.0 / (1.0 + jnp.exp(-x))


def _fold_lanes(x, op):
    acc = x[:, 0:LANES]
    for c in range(1, x.shape[1] // LANES):
        acc = op(acc, x[:, c * LANES:(c + 1) * LANES])
    return acc


def _row_max(x):
    return jnp.max(_fold_lanes(x, jnp.maximum), axis=-1, keepdims=True)


def _row_sum(x):
    return jnp.sum(_fold_lanes(x, jnp.add), axis=-1, keepdims=True)


def _log_sigmoid(x):
    return jnp.minimum(x, 0.0) - jnp.log1p(jnp.exp(-jnp.abs(x)))


def _head_norm(z, gain):
    outs = []
    for h in range(N_HEADS):
        blk = z[:, h * HEAD_DIM:(h + 1) * HEAD_DIM]
        ms = jnp.mean(blk * blk, axis=-1, keepdims=True)
        outs.append(blk * lax.rsqrt(ms + RMS_EPS) * gain)
    return jnp.concatenate(outs, axis=1)


def _proj_kernel(*refs, kind, nt, v_transposed):
    za, zb, zs, w_sc = refs[-4:]
    n_out = _PROJ_OUTPUTS[kind]
    in_p, in_s, w_ref = refs[0], refs[1], refs[2]
    extra = refs[3:len(refs) - 4 - 2 * n_out]
    out_p = refs[len(refs) - 4 - 2 * n_out:len(refs) - 4 - n_out]
    out_s = refs[len(refs) - 4 - n_out:len(refs) - 4]
    s = pl.program_id(0)
    half = PROJ_COLS // 2

    def group(in_ref, outs, transposed):
        if kind == "uq":
            g_ref, qg_ref = extra
            xn_ref, u_ref, qb_ref = outs

            def lhs():
                x = in_ref[...]
                ms = jnp.mean(x * x, axis=-1, keepdims=True)
                xn = (x * lax.rsqrt(ms + RMS_EPS) * g_ref[...]).astype(BF16)
                xn_ref[...] = xn
                return xn

            def epilogue(z):
                u_ref[...] = z[:, 0:half]
                qb_ref[...] = (_head_norm(z[:, half:], qg_ref[...]) * QK_SCALE_LOG2).astype(BF16)
        elif kind == "kv":
            (kg_ref,) = extra
            k_ref, kb_ref, v_ref, vb_ref = outs
            lhs = lambda: in_ref[...]

            def epilogue(z):
                kn = _head_norm(z[:, 0:half], kg_ref[...])
                k_ref[...] = kn
                kb_ref[...] = kn.astype(BF16)
                v = z[:, half:]
                v_ref[...] = v
                if transposed:
                    vt = v.T
                    for c in range(vb_ref.shape[0]):
                        vb_ref[c] = vt[:, c * ATT_TK:(c + 1) * ATT_TK].astype(BF16)
                else:
                    vb_ref[...] = v.astype(BF16)
        elif kind == "gate_ssm":
            (bf_ref,) = extra
            gate_ref, logf_ref = outs
            lhs = lambda: in_ref[...]

            def epilogue(z):
                gate_ref[...] = _sigmoid(z).astype(BF16)
                logf_ref[...] = _log_sigmoid(z[:, 0:LANES] + bf_ref[...])
        else:
            (wt_ref,) = extra
            gate_ref, tail_ref = outs

            def lhs():
                xn = in_ref[...]
                tail_ref[...] = _sigmoid(jnp.dot(xn, wt_ref[...], preferred_element_type=F32)).astype(BF16)
                return xn

            def epilogue(z):
                gate_ref[...] = _sigmoid(z).astype(BF16)
        return lhs, epilogue

    lhs_p, epilogue_p = group(in_p, out_p, v_transposed)
    lhs_s, epilogue_s = group(in_s, out_s, False)

    @pl.when(s == 0)
    def _():
        zb[...] = jnp.zeros_like(zb)
        w_sc[...] = w_ref[...].astype(BF16)

    def body(z_write, z_read):
        epilogue_p(z_read[...])
        z_write[...] = jnp.dot(lhs_p(), w_sc[...], preferred_element_type=F32)

    pl.when((s < nt) & (s % 2 == 0))(lambda: body(za, zb))
    pl.when((s < nt) & (s % 2 == 1))(lambda: body(zb, za))

    @pl.when(s == nt)
    def _():
        epilogue_p((za if (nt - 1) % 2 == 0 else zb)[...])
        zs[...] = jnp.dot(lhs_s(), w_sc[...], preferred_element_type=F32)

    @pl.when(s == nt + 1)
    def _():
        epilogue_s(zs[...])


_PROJ_OUTPUTS = {"uq": 3, "kv": 4, "gate_ssm": 2, "gate_att": 2}


def _inproj(x, xs, norm_g, w_in_l, w_tail, b_f, q_g, k_g, tm):
    t, ts = x.shape[0], xs.shape[0]
    nt = t // tm
    hi = nt - 1
    cur = lambda s: (jnp.minimum(s, hi), 0)
    lag = lambda s: (jnp.clip(s - 1, 0, hi), 0)
    lag3 = lambda s: (jnp.clip(s - 1, 0, hi), 0, 0)
    const = lambda s: (0, 0)
    sds = jax.ShapeDtypeStruct

    def call(kind, slab_index, row_inputs, extra, extra_specs, widths, dtypes, lagged, vt=False):
        in_width = row_inputs[0].shape[1]
        in_specs = [pl.BlockSpec((tm, in_width), cur), pl.BlockSpec((ts, in_width), const),
                    pl.BlockSpec((D_MODEL, PROJ_COLS), lambda s: (0, slab_index), pipeline_mode=pl.Buffered(1))]
        out_shape, out_specs = [], []
        for rows_n, block_n, prompt in ((t, tm, True), (ts, ts, False)):
            for w, dt, lg in zip(widths, dtypes, lagged):
                if vt and prompt and w is None:
                    out_shape.append(sds((t // ATT_TK, D_ATT, ATT_TK), BF16))
                    out_specs.append(pl.BlockSpec((tm // ATT_TK, D_ATT, ATT_TK), lag3))
                    continue
                w = D_ATT if w is None else w
                out_shape.append(sds((rows_n, w), dt))
                index = (lag if lg else cur) if prompt else const
                out_specs.append(pl.BlockSpec((block_n, w), index))
        outs = pl.pallas_call(
            functools.partial(_proj_kernel, kind=kind, nt=nt, v_transposed=vt),
            out_shape=tuple(out_shape),
            grid=(nt + 2,),
            in_specs=in_specs + extra_specs,
            out_specs=tuple(out_specs),
            scratch_shapes=[pltpu.VMEM((tm, PROJ_COLS), F32), pltpu.VMEM((tm, PROJ_COLS), F32),
                            pltpu.VMEM((ts, PROJ_COLS), F32), pltpu.VMEM((D_MODEL, PROJ_COLS), BF16)],
            compiler_params=_params(("arbitrary",)),
            name="proj_" + kind,
        )(*row_inputs, w_in_l, *extra)
        n = len(widths)
        return outs[:n], outs[n:]

    gain = lambda w: pl.BlockSpec((1, w), const)
    (xn, u, qb), (xns, us, qbs) = call(
        "uq", 0, (x, xs), (norm_g, q_g), [gain(D_MODEL), gain(HEAD_DIM)],
        (D_MODEL, D_SSM, D_ATT), (BF16, F32, BF16), (False, True, True))
    (k, kb, v, vb), (ks, kbs, vs, vbs) = call(
        "kv", 1, (xn, xns), (k_g,), [gain(HEAD_DIM)],
        (D_ATT, D_ATT, D_ATT, None), (F32, BF16, F32, BF16), (True, True, True, True), vt=True)
    (gate_s, logf), (gate_ss, logfs) = call(
        "gate_ssm", 2, (xn, xns), (b_f,), [gain(LANES)],
        (PROJ_COLS, LANES), (BF16, F32), (True, True))
    (gate_a, gate_t), (gate_as, gate_ts) = call(
        "gate_att", 3, (xn, xns), (w_tail,), [pl.BlockSpec((D_MODEL, LANES), const)],
        (PROJ_COLS, LANES), (BF16, BF16), (True, False))
    return ((u, qb, k, kb, v, vb, (gate_s, gate_a, gate_t), logf),
            (us, qbs, ks, kbs, vs, vbs, (gate_ss, gate_as, gate_ts), logfs))


CUM_ROWS = 256


def _cumaug_kernel(lf_ref, eq_ref, ek_ref, caq_ref, cak_ref, carry_sc):
    @pl.when(pl.program_id(1) == 0)
    def _():
        carry_sc[...] = jnp.zeros_like(carry_sc)

    lf = lf_ref[...]
    r = lax.broadcasted_iota(jnp.int32, (CUM_ROWS, CUM_ROWS), 0)
    c = lax.broadcasted_iota(jnp.int32, (CUM_ROWS, CUM_ROWS), 1)
    tri = (c <= r).astype(BF16)
    lf_hi = lf.astype(BF16)
    lf_r = lf - lf_hi.astype(F32)
    lf_mid = lf_r.astype(BF16)
    lf_lo = (lf_r - lf_mid.astype(F32)).astype(BF16)
    parts = jnp.dot(tri, jnp.concatenate([lf_hi, lf_mid, lf_lo], axis=1), preferred_element_type=F32)
    cs = (parts[:, 0:LANES] + parts[:, LANES:2 * LANES] + parts[:, 2 * LANES:3 * LANES]) + carry_sc[...]
    carry_sc[...] = cs[CUM_ROWS - 1:CUM_ROWS, :]
    cs2 = cs * LOG2E
    hi = cs2.astype(BF16).astype(F32)
    r1 = cs2 - hi
    mid = r1.astype(BF16).astype(F32)
    lo = (r1 - mid).astype(BF16).astype(F32)
    lane = lax.broadcasted_iota(jnp.int32, (CUM_ROWS, LANES), 1)
    packed = jnp.where(lane < N_HEADS, hi,
                       jnp.where(lane < 2 * N_HEADS, pltpu.roll(mid, N_HEADS, axis=1),
                                 jnp.where(lane < 3 * N_HEADS, pltpu.roll(lo, 2 * N_HEADS, axis=1),
                                           jnp.where(lane == 3 * N_HEADS, 1.0, 0.0)))).astype(BF16)
    caq_ref[...] = jnp.dot(packed, eq_ref[...], preferred_element_type=F32).astype(BF16)
    cak_ref[...] = jnp.dot(packed, ek_ref[...], preferred_element_type=F32).astype(BF16)


def _placement_matrices():
    eq = np.zeros((LANES, D_ATT), np.float32)
    ek = np.zeros((LANES, D_ATT), np.float32)
    for h in range(N_HEADS):
        base = h * HEAD_DIM
        for piece in range(3):
            eq[piece * N_HEADS + h, base + piece] = 1.0
            ek[piece * N_HEADS + h, base + 3 + piece] = -1.0
            eq[3 * N_HEADS, base + 3 + piece] = 1.0
            ek[3 * N_HEADS, base + piece] = 1.0
    return jnp.asarray(eq, BF16), jnp.asarray(ek, BF16)


def _cumaug(logf, batch, seq):
    nb = seq // CUM_ROWS
    eq, ek = _placement_matrices()
    const = lambda: pl.BlockSpec((LANES, D_ATT), lambda b, i: (0, 0))
    return pl.pallas_call(
        _cumaug_kernel,
        out_shape=(jax.ShapeDtypeStruct((batch * seq, D_ATT), BF16),
                   jax.ShapeDtypeStruct((batch * seq, D_ATT), BF16)),
        grid=(batch, nb),
        in_specs=[pl.BlockSpec((CUM_ROWS, LANES), lambda b, i: (b * nb + i, 0)), const(), const()],
        out_specs=(pl.BlockSpec((CUM_ROWS, D_ATT), lambda b, i: (b * nb + i, 0)),
                   pl.BlockSpec((CUM_ROWS, D_ATT), lambda b, i: (b * nb + i, 0))),
        scratch_shapes=[pltpu.VMEM((1, LANES), F32)],
        compiler_params=_params(("arbitrary", "arbitrary")),
        name="cumaug",
    )(logf, eq, ek)


ACC_ROWS = HEAD_DIM + 16


def _attn_kernel(q_ref, cq_ref, k_ref, ck_ref, vt_ref, o_ref, m_sc, acc_sc, sa_sc, sb_sc):
    i = pl.program_id(2)
    tk = ATT_TK
    m_sc[...] = jnp.full(m_sc.shape, NEG_INF, F32)
    acc_sc[...] = jnp.zeros_like(acc_sc)
    sub = lax.broadcasted_iota(jnp.int32, (ACC_ROWS - HEAD_DIM, tk), 0)
    ones_rows = jnp.where(sub == 0, 1.0, 0.0).astype(BF16)

    def logits(j, s_ref, lo):
        ks = pl.multiple_of(j * tk, tk)
        kk = jnp.concatenate([k_ref[pl.ds(ks, tk), :], ck_ref[pl.ds(ks, tk), :]], axis=1)
        qq = jnp.concatenate([q_ref[lo:, :], cq_ref[lo:, :]], axis=1)
        s_ref[:, lo:] = lax.dot_general(kk, qq, _NT, preferred_element_type=F32)

    def consume(j, s_ref, lo, masked):
        s = s_ref[:, lo:]
        if masked:
            key = lax.broadcasted_iota(jnp.int32, s.shape, 0)
            qry = lax.broadcasted_iota(jnp.int32, s.shape, 1)
            s = jnp.where(key <= qry, s, NEG_INF)
        vt = jnp.concatenate([vt_ref[j], ones_rows], axis=0)
        m_old = m_sc[:, lo:]
        m_new = jnp.maximum(m_old, jnp.max(s, axis=0, keepdims=True))
        alpha = jnp.exp2(m_old - m_new)
        p = jnp.exp2(s - m_new).astype(BF16)
        acc_sc[:, lo:] = alpha * acc_sc[:, lo:] + jnp.dot(vt, p, preferred_element_type=F32)
        m_sc[:, lo:] = m_new

    logits(0, sa_sc, 0)

    def pair(jj, carry):
        logits(2 * jj + 1, sb_sc, 0)
        consume(2 * jj, sa_sc, 0, False)
        logits(2 * jj + 2, sa_sc, 0)
        consume(2 * jj + 1, sb_sc, 0, False)
        return carry

    lax.fori_loop(0, i, pair, 0)
    logits(2 * i + 1, sb_sc, tk)
    consume(2 * i, sa_sc, 0, True)
    consume(2 * i + 1, sb_sc, tk, True)

    acc = acc_sc[...]
    out_t = acc[0:HEAD_DIM, :] / acc[HEAD_DIM:HEAD_DIM + 1, :]
    o_ref[...] = out_t.T.astype(BF16)


def _attention(qb, caq, kb, cak, vt, batch, seq):
    nq = seq // ATT_TQ
    nkb = seq // ATT_TK
    qspec = lambda: pl.BlockSpec((ATT_TQ, HEAD_DIM), lambda b, h, i: (b * nq + i, h))
    kspec = lambda: pl.BlockSpec((seq, HEAD_DIM), lambda b, h, i: (b, h))
    return pl.pallas_call(
        _attn_kernel,
        out_shape=jax.ShapeDtypeStruct((batch * seq, D_ATT), BF16),
        grid=(batch, N_HEADS, nq),
        in_specs=[qspec(), qspec(), kspec(), kspec(),
                  pl.BlockSpec((nkb, HEAD_DIM, ATT_TK), lambda b, h, i: (b, h, 0))],
        out_specs=qspec(),
        scratch_shapes=[pltpu.VMEM((1, ATT_TQ), F32), pltpu.VMEM((ACC_ROWS, ATT_TQ), F32),
                        pltpu.VMEM((ATT_TK, ATT_TQ), F32), pltpu.VMEM((ATT_TK, ATT_TQ), F32)],
        compiler_params=_params(("arbitrary", "arbitrary", "arbitrary")),
        name="fox_prompt",
    )(qb, caq, kb, cak, vt)


def _discretise(a_re, a_im, log_dt):
    dt = jnp.exp(log_dt)
    mag = jnp.exp(dt * a_re)
    ang = dt * a_im
    abr = mag * jnp.cos(ang)
    abi = mag * jnp.sin(ang)
    e_re = abr - 1.0
    e_im = abi
    inv_den = 1.0 / (a_re * a_re + a_im * a_im)
    f_re = (e_re * a_re + e_im * a_im) * inv_den
    f_im = (e_im * a_re - e_re * a_im) * inv_den
    return abr, abi, f_re, f_im


def _powers(abr, abi, n):
    pr, pi = [jnp.ones_like(abr)], [jnp.zeros_like(abi)]
    for _ in range(n):
        r, i = pr[-1], pi[-1]
        pr.append(r * abr - i * abi)
        pi.append(r * abi + i * abr)
    return pr, pi


def _s5gen_kernel(ar_row, ai_row, dt_row, bre_ref, bim_ref, cre_ref, cim_ref,
                  vrev_ref, bsum_ref, m_ref, bmat_ref, cmat_ref, a1_ref, a16_ref):
    ns = BLOCK_STATE
    abr, abi, f_re, f_im = _discretise(ar_row[...], ai_row[...], dt_row[...])
    bre, bim = bre_ref[...], bim_ref[...]
    bbr = f_re * bre - f_im * bim
    bbi = f_re * bim + f_im * bre
    bmat = jnp.concatenate([bbr, bbi], axis=1)
    bmat_ref[...] = bmat
    pr, pi = _powers(abr, abi, CHUNK)
    a1_ref[...] = jnp.concatenate([pr[1], pi[1]], axis=1)
    a16_ref[...] = jnp.concatenate([pr[CHUNK], pi[CHUNK]], axis=1)
    for i in range(CHUNK):
        r, im = pr[CHUNK - 1 - i], pi[CHUNK - 1 - i]
        blk = jnp.concatenate([r * bbr - im * bbi, r * bbi + im * bbr], axis=1)
        bsum_ref[i * LANES:(i + 1) * LANES, :] = blk.astype(BF16)
    npow = CHUNK + 1
    stacked = jnp.concatenate(pr + pi + [jnp.zeros((LANES - 2 * npow, ns), F32)], axis=0)
    pt = stacked.T
    qr = [pt[:, t:t + 1] for t in range(npow)]
    qi = [pt[:, npow + t:npow + t + 1] for t in range(npow)]
    cre, cim = cre_ref[...], cim_ref[...]
    vrev_ref[(CHUNK - 1) * LANES:CHUNK * LANES, 0:LANES] = jnp.zeros((LANES, LANES), BF16)
    bmat_hi = bmat.astype(BF16)
    for tau in range(CHUNK + 1):
        blk = jnp.concatenate([cre * qr[tau] - cim * qi[tau],
                               -(cre * qi[tau] + cim * qr[tau])], axis=0)
        if tau == 0:
            cmat_ref[...] = blk
        else:
            m_ref[:, (tau - 1) * LANES:tau * LANES] = blk.astype(BF16)
        if tau < CHUNK:
            w = jnp.dot(bmat_hi, blk.astype(BF16), preferred_element_type=F32).astype(BF16)
            k = CHUNK - 1 - tau
            vrev_ref[k * LANES:(k + 1) * LANES, LANES:2 * LANES] = w
            if k >= 1:
                vrev_ref[(k - 1) * LANES:k * LANES, 0:LANES] = w


def _s5gen(ar_row, ai_row, dt_row, bd_bre, bd_bim, bd_cre, bd_cim):
    ns = BLOCK_STATE
    b3 = lambda s1, s2: pl.BlockSpec((None, s1, s2), lambda o: (o, 0, 0))
    out_shape = (
        jax.ShapeDtypeStruct((N_BLOCKS, CHUNK * LANES, 2 * LANES), BF16),
        jax.ShapeDtypeStruct((N_BLOCKS, CHUNK * LANES, 2 * ns), BF16),
        jax.ShapeDtypeStruct((N_BLOCKS, 2 * ns, CHUNK * LANES), BF16),
        jax.ShapeDtypeStruct((N_BLOCKS, LANES, 2 * ns), F32),
        jax.ShapeDtypeStruct((N_BLOCKS, 2 * ns, LANES), F32),
        jax.ShapeDtypeStruct((N_BLOCKS, 1, 2 * ns), F32),
        jax.ShapeDtypeStruct((N_BLOCKS, 1, 2 * ns), F32),
    )
    return pl.pallas_call(
        _s5gen_kernel,
        out_shape=out_shape,
        grid=(N_BLOCKS,),
        in_specs=[b3(1, ns), b3(1, ns), b3(1, ns),
                  b3(LANES, ns), b3(LANES, ns), b3(ns, LANES), b3(ns, LANES)],
        out_specs=(b3(CHUNK * LANES, 2 * LANES), b3(CHUNK * LANES, 2 * ns), b3(2 * ns, CHUNK * LANES),
                   b3(LANES, 2 * ns), b3(2 * ns, LANES), b3(1, 2 * ns), b3(1, 2 * ns)),
        compiler_params=_params(("arbitrary",)),
        name="s5_operators",
    )(ar_row, ai_row, dt_row, bd_bre, bd_bim, bd_cre, bd_cim)


def _s5_prompt_kernel(u_ref, vrev_ref, bsum_ref, m_ref, a16_ref, d_ref,
                      y_ref, hre_ref, him_ref, ucat_sc, s_sc, hin_sc, *, n_chunks):
    ns = BLOCK_STATE
    for i in range(CHUNK):
        ucat_sc[:, i * LANES:(i + 1) * LANES] = u_ref[pl.ds(i, n_chunks, stride=CHUNK), :].astype(BF16)
    s_sc[...] = jnp.dot(ucat_sc[...], bsum_ref[...], preferred_element_type=F32)
    ar = a16_ref[:, 0:ns]
    ai = a16_ref[:, ns:2 * ns]

    def step(k, carry):
        hr, hi = carry
        hin_sc[pl.ds(k, 1), 0:ns] = hr
        hin_sc[pl.ds(k, 1), ns:2 * ns] = hi
        sr = s_sc[pl.ds(k, 1), 0:ns]
        si = s_sc[pl.ds(k, 1), ns:2 * ns]
        return ar * hr - ai * hi + sr, ar * hi + ai * hr + si

    zero = jnp.zeros((1, ns), F32)
    hr, hi = lax.fori_loop(0, n_chunks, step, (zero, zero))
    hre_ref[...] = hr
    him_ref[...] = hi
    hin = hin_sc[...].astype(BF16)
    d = d_ref[...]
    for jp in range(CHUNK // 2):
        j = 2 * jp
        kk = (j + 2) * LANES
        acc = jnp.dot(ucat_sc[:, 0:kk], vrev_ref[(CHUNK - 2 - j) * LANES:, :],
                      preferred_element_type=F32)
        acc = acc + jnp.dot(hin, m_ref[:, j * LANES:(j + 2) * LANES], preferred_element_type=F32)
        for jj in range(2):
            uj = u_ref[pl.ds(j + jj, n_chunks, stride=CHUNK), :]
            val = acc[:, jj * LANES:(jj + 1) * LANES] + d * uj
            y_ref[pl.ds(j + jj, n_chunks, stride=CHUNK), :] = _gelu_exact(val)


def _s5_prompt(u, vrev, bsum, m, a16, d, batch, seq):
    ns = BLOCK_STATE
    n_chunks = seq // CHUNK
    w3 = lambda s1, s2: pl.BlockSpec((None, s1, s2), lambda o, b: (o, 0, 0))
    return pl.pallas_call(
        functools.partial(_s5_prompt_kernel, n_chunks=n_chunks),
        out_shape=(jax.ShapeDtypeStruct((batch * seq, D_SSM), F32),
                   jax.ShapeDtypeStruct((batch, 1, N_GROUPS * SSM_STATE), F32),
                   jax.ShapeDtypeStruct((batch, 1, N_GROUPS * SSM_STATE), F32)),
        grid=(N_BLOCKS, batch),
        in_specs=[pl.BlockSpec((seq, LANES), lambda o, b: (b, o)),
                  w3(CHUNK * LANES, 2 * LANES), w3(CHUNK * LANES, 2 * ns), w3(2 * ns, CHUNK * LANES),
                  w3(1, 2 * ns),
                  pl.BlockSpec((1, LANES), lambda o, b: (0, o))],
        out_specs=(pl.BlockSpec((seq, LANES), lambda o, b: (b, o)),
                   pl.BlockSpec((None, 1, ns), lambda o, b: (b, 0, o)),
                   pl.BlockSpec((None, 1, ns), lambda o, b: (b, 0, o))),
        scratch_shapes=[pltpu.VMEM((n_chunks, CHUNK * LANES), BF16),
                        pltpu.VMEM((n_chunks, 2 * ns), F32),
                        pltpu.VMEM((n_chunks, 2 * ns), F32)],
        compiler_params=_params(("arbitrary", "arbitrary")),
        name="s5_prompt",
    )(u, vrev, bsum, m, a16, d)


def _s5_sample_kernel(u_ref, h0r_ref, h0i_ref, bmat_ref, cmat_ref, a1_ref, d_ref,
                      y_ref, hre_ref, him_ref, *, n_seq, n_tok):
    ns = BLOCK_STATE
    u = u_ref[...]
    bu = jnp.dot(u, bmat_ref[...], precision=HIGHEST, preferred_element_type=F32)
    ar = a1_ref[:, 0:ns]
    ai = a1_ref[:, ns:2 * ns]
    hr, hi = h0r_ref[...], h0i_ref[...]
    hs = []
    for t in range(n_tok):
        br = bu[t * n_seq:(t + 1) * n_seq, 0:ns]
        bi = bu[t * n_seq:(t + 1) * n_seq, ns:2 * ns]
        hr, hi = ar * hr - ai * hi + br, ar * hi + ai * hr + bi
        hs.append(jnp.concatenate([hr, hi], axis=1))
    hcat = jnp.concatenate(hs, axis=0)
    val = jnp.dot(hcat, cmat_ref[...], precision=HIGHEST, preferred_element_type=F32) + d_ref[...] * u
    y_ref[...] = _gelu_exact(val)
    hre_ref[...] = hr
    him_ref[...] = hi


def _s5_sample(u, h0r, h0i, bmat, cmat, a1, d, n_seq, n_tok):
    ns = BLOCK_STATE
    rows = n_seq * n_tok
    w3 = lambda s1, s2: pl.BlockSpec((None, s1, s2), lambda o: (o, 0, 0))
    col = lambda r, c: pl.BlockSpec((r, c), lambda o: (0, o))
    return pl.pallas_call(
        functools.partial(_s5_sample_kernel, n_seq=n_seq, n_tok=n_tok),
        out_shape=(jax.ShapeDtypeStruct((rows, D_SSM), F32),
                   jax.ShapeDtypeStruct((n_seq, N_GROUPS * SSM_STATE), F32),
                   jax.ShapeDtypeStruct((n_seq, N_GROUPS * SSM_STATE), F32)),
        grid=(N_BLOCKS,),
        in_specs=[col(rows, LANES), col(n_seq, ns), col(n_seq, ns),
                  w3(LANES, 2 * ns), w3(2 * ns, LANES), w3(1, 2 * ns), col(1, LANES)],
        out_specs=(col(rows, LANES), col(n_seq, ns), col(n_seq, ns)),
        compiler_params=_params(("arbitrary",)),
        name="s5_sample",
    )(u, h0r, h0i, bmat, cmat, a1, d)


PAGES_PER_STEP = 16
PAGE_GROUP = 4

PAGE_KEYS = PAGE_SIZE * N_HEADS


def _decode_kernel(pt_ref, q_ref, lfrow_ref, lfcol_ref, kn_ref, vn_ref, *refs, n_tok, n_steps):
    g = PAGES_PER_STEP
    k_refs = refs[0:g]
    v_refs = refs[g:2 * g]
    lf_refs = refs[2 * g:3 * g]
    o_ref = refs[3 * g]
    m_sc, l_sc, acc_sc, carry_sc = refs[3 * g + 1:]
    j = pl.program_id(1)
    rows = n_tok * N_HEADS
    head_mask = N_HEADS - 1
    head_shift = N_HEADS.bit_length() - 1

    @pl.when(j == 0)
    def _():
        m_sc[...] = jnp.full(m_sc.shape, NEG_INF, F32)
        l_sc[...] = jnp.zeros_like(l_sc)
        acc_sc[...] = jnp.zeros_like(acc_sc)
        carry_sc[...] = jnp.zeros_like(carry_sc)

    q = q_ref[...]
    lfcol = lfcol_ref[...]
    run = jnp.zeros((N_HEADS, 1), F32)
    pieces = []
    for t in range(n_tok):
        run = run + lfcol[t * N_HEADS:(t + 1) * N_HEADS, :]
        pieces.append(run)
    c_col = jnp.concatenate(pieces, axis=0)

    def online(s_blocks, v_blocks):
        m = m_sc[...]
        smax = s_blocks[0]
        for s in s_blocks[1:]:
            smax = jnp.maximum(smax, s)
        m_new = jnp.maximum(m, _row_max(smax))
        alpha = jnp.exp2(m - m_new)
        psum = None
        pv = None
        for s, vblk in zip(s_blocks, v_blocks):
            p = jnp.exp2(s - m_new)
            psum = p if psum is None else psum + p
            d = jnp.dot(p.astype(BF16), vblk, preferred_element_type=F32)
            pv = d if pv is None else pv + d
        l_sc[...] = alpha * l_sc[...] + _row_sum(psum)
        acc_sc[...] = alpha * acc_sc[...] + pv
        m_sc[...] = m_new

    col = lax.broadcasted_iota(jnp.int32, (rows, PAGE_KEYS), 1)
    row = lax.broadcasted_iota(jnp.int32, (rows, PAGE_KEYS), 0)
    own = (col & head_mask) == (row & head_mask)
    fixed = jnp.where(own, c_col * LOG2E, NEG_INF)

    lf = jnp.concatenate([lf_refs[pg][...] for pg in range(g)], axis=0)
    lane = lax.broadcasted_iota(jnp.int32, (g, PAGE_KEYS), 1)
    suffix = lf
    total = lf
    sh = N_HEADS
    while sh < PAGE_KEYS:
        suffix = suffix + jnp.where(lane + sh < PAGE_KEYS, pltpu.roll(suffix, PAGE_KEYS - sh, axis=1), 0.0)
        total = total + pltpu.roll(total, PAGE_KEYS - sh, axis=1)
        sh *= 2
    carry = carry_sc[...]
    past = []
    for pg in range(g):
        past.append((suffix[pg:pg + 1, :] - lf[pg:pg + 1, :] + carry) * LOG2E)
        carry = carry + total[pg:pg + 1, :]
    carry_sc[...] = carry

    def logits(pages):
        out = []
        for pg in pages:
            kp = k_refs[pg][...].reshape(PAGE_KEYS, HEAD_DIM).astype(BF16)
            s = lax.dot_general(q, kp, _NT, preferred_element_type=F32)
            out.append(s + fixed + past[pg])
        return out

    def values(pages):
        return [v_refs[pg][...].reshape(PAGE_KEYS, HEAD_DIM).astype(BF16) for pg in pages]

    groups = [list(range(a, a + PAGE_GROUP)) for a in range(0, g, PAGE_GROUP)]
    s_next = logits(groups[0])
    for gi, pages in enumerate(groups):
        s_cur = s_next
        if gi + 1 < len(groups):
            s_next = logits(groups[gi + 1])
        online(s_cur, values(pages))

    @pl.when(j == n_steps - 1)
    def _():
        ln = lax.broadcasted_iota(jnp.int32, (1, LANES), 1)
        c_row = lfrow_ref[...]
        sh2 = N_HEADS
        while sh2 < rows:
            c_row = c_row + jnp.where(ln >= sh2, pltpu.roll(c_row, sh2, axis=1), 0.0)
            sh2 *= 2
        cl = lax.broadcasted_iota(jnp.int32, (rows, LANES), 1)
        rw = lax.broadcasted_iota(jnp.int32, (rows, LANES), 0)
        valid = ((cl < rows) & ((cl & head_mask) == (rw & head_mask))
                 & ((cl >> head_shift) <= (rw >> head_shift)))
        s = lax.dot_general(q, kn_ref[...], _NT, preferred_element_type=F32)
        s = jnp.where(valid, s + (c_col - c_row) * LOG2E, NEG_INF)
        online([s], [vn_ref[...]])
        o_ref[...] = acc_sc[...] / l_sc[...]


def _decode_attention(page_table, q, lfrow, lfcol, kn, vn, cache_k, cache_v, cache_lf, n_tok):
    n_seq, n_pages = page_table.shape
    g = PAGES_PER_STEP
    n_steps = n_pages // g
    rows = n_tok * N_HEADS

    def page5(pg):
        return lambda b, j, pt: (0, pt[b * n_pages + (n_pages - 1 - (j * g + pg))], 0, 0, 0)

    def page3(pg):
        return lambda b, j, pt: (pt[b * n_pages + (n_pages - 1 - (j * g + pg))], 0, 0)

    seq3 = lambda s1, s2: pl.BlockSpec((None, s1, s2), lambda b, j, pt: (b, 0, 0))
    kv_block = (None, None, PAGE_SIZE, N_HEADS, HEAD_DIM)
    in_specs = [seq3(rows, HEAD_DIM), seq3(1, LANES), seq3(rows, 1), seq3(LANES, HEAD_DIM), seq3(LANES, HEAD_DIM)]
    in_specs += [pl.BlockSpec(kv_block, page5(pg)) for pg in range(g)]
    in_specs += [pl.BlockSpec(kv_block, page5(pg)) for pg in range(g)]
    in_specs += [pl.BlockSpec((None, 1, PAGE_KEYS), page3(pg)) for pg in range(g)]
    grid_spec = pltpu.PrefetchScalarGridSpec(
        num_scalar_prefetch=1,
        grid=(n_seq, n_steps),
        in_specs=in_specs,
        out_specs=pl.BlockSpec((None, rows, HEAD_DIM), lambda b, j, pt: (b, 0, 0)),
        scratch_shapes=[pltpu.VMEM((rows, 1), F32),
                        pltpu.VMEM((rows, 1), F32),
                        pltpu.VMEM((rows, HEAD_DIM), F32),
                        pltpu.VMEM((1, PAGE_KEYS), F32)],
    )
    return pl.pallas_call(
        functools.partial(_decode_kernel, n_tok=n_tok, n_steps=n_steps),
        out_shape=jax.ShapeDtypeStruct((n_seq, rows, HEAD_DIM), F32),
        grid_spec=grid_spec,
        compiler_params=_params(("arbitrary", "arbitrary")),
        name="fox_sample",
    )(page_table.reshape(-1), q, lfrow, lfcol, kn, vn,
      *([cache_k] * g), *([cache_v] * g), *([cache_lf] * g))


def _merge_kernel(y_ref, ya_ref, gs_ref, ga_ref, gt_ref, x_ref, wglu_ref, bglu_ref, wbs_ref, wba_ref, wo_ref,
                  ng_ref, x1_ref, xn_ref):
    y = y_ref[...]
    gl = jnp.dot(y.astype(BF16), wglu_ref[...], preferred_element_type=F32) + bglu_ref[...]
    ys = (y * _sigmoid(gl)).astype(BF16)
    ms = jnp.dot(ys, wbs_ref[...], preferred_element_type=F32)
    ma = jnp.dot(ya_ref[...], wba_ref[...], preferred_element_type=F32)
    ga = ga_ref[...]
    gate_s = jnp.concatenate([gs_ref[...], ga[:, 0:LANES]], axis=1).astype(F32)
    gate_a = jnp.concatenate([ga, gt_ref[...]], axis=1).astype(F32)
    merged = gate_s * ms + gate_a * ma
    x1 = x_ref[...] + jnp.dot(merged.astype(BF16), wo_ref[...], preferred_element_type=F32)
    x1_ref[...] = x1
    ms1 = jnp.mean(x1 * x1, axis=-1, keepdims=True)
    xn_ref[...] = (x1 * lax.rsqrt(ms1 + RMS_EPS) * ng_ref[...]).astype(BF16)


MERGE_COLS = D_MODEL + LANES


def _merge(y, ya, gates, x, w_glu, b_glu, w_bs, w_ba, w_o, n_g, tm):
    t = x.shape[0]
    gate_s, gate_a, gate_t = gates
    row = lambda c: pl.BlockSpec((tm, c), lambda i: (i, 0))
    const = lambda r, c: pl.BlockSpec((r, c), lambda i: (0, 0), pipeline_mode=pl.Buffered(1))
    return pl.pallas_call(
        _merge_kernel,
        out_shape=(jax.ShapeDtypeStruct((t, D_MODEL), F32), jax.ShapeDtypeStruct((t, D_MODEL), BF16)),
        grid=(t // tm,),
        in_specs=[row(D_SSM), row(D_ATT), row(PROJ_COLS), row(PROJ_COLS), row(LANES), row(D_MODEL),
                  const(D_SSM, D_SSM), const(1, D_SSM), const(D_SSM, MERGE_COLS), const(D_ATT, MERGE_COLS),
                  const(MERGE_COLS, D_MODEL), const(1, D_MODEL)],
        out_specs=(row(D_MODEL), row(D_MODEL)),
        compiler_params=_params(("arbitrary",)),
        name="merge_out",
    )(y, ya, gate_s, gate_a, gate_t, x, w_glu, b_glu, w_bs, w_ba, w_o, n_g)


FFN_TF = 256


def _ffn_kernel(x1_ref, xn_ref, x1s_ref, xns_ref, wg_ref, wu_ref, wd_ref, o_ref, os_ref, wg_sc, wu_sc, wd_sc):
    i, f = pl.program_id(0), pl.program_id(1)
    @pl.when(f == 0)
    def _():
        o_ref[...] = x1_ref[...]

    xn = xn_ref[...]
    wg_sc[...] = wg_ref[...].astype(BF16)
    a = jnp.dot(xn, wg_sc[...], preferred_element_type=F32)
    wu_sc[...] = wu_ref[...].astype(BF16)
    b = jnp.dot(xn, wu_sc[...], preferred_element_type=F32)
    wd_sc[...] = wd_ref[...].astype(BF16)
    o_ref[...] += jnp.dot((a * _sigmoid(a) * b).astype(BF16), wd_sc[...], preferred_element_type=F32)

    @pl.when(i == pl.num_programs(0) - 1)
    def _():
        @pl.when(f == 0)
        def _():
            os_ref[...] = x1s_ref[...]

        xs = xns_ref[...]
        a_s = jnp.dot(xs, wg_sc[...], preferred_element_type=F32)
        b_s = jnp.dot(xs, wu_sc[...], preferred_element_type=F32)
        os_ref[...] += jnp.dot((a_s * _sigmoid(a_s) * b_s).astype(BF16), wd_sc[...], preferred_element_type=F32)


def _ffn(x1, xn, x1s, xns, w_g, w_u, w_d, tm):
    t = x1.shape[0]
    ts = x1s.shape[0]
    small = lambda: pl.BlockSpec((ts, D_MODEL), lambda i, f: (0, 0))
    return pl.pallas_call(
        _ffn_kernel,
        out_shape=(jax.ShapeDtypeStruct((t, D_MODEL), F32), jax.ShapeDtypeStruct((ts, D_MODEL), F32)),
        grid=(t // tm, D_FF // FFN_TF),
        in_specs=[pl.BlockSpec((tm, D_MODEL), lambda i, f: (i, 0)),
                  pl.BlockSpec((tm, D_MODEL), lambda i, f: (i, 0), pipeline_mode=pl.Buffered(1)),
                  small(), small(),
                  pl.BlockSpec((D_MODEL, FFN_TF), lambda i, f: (0, f)),
                  pl.BlockSpec((D_MODEL, FFN_TF), lambda i, f: (0, f)),
                  pl.BlockSpec((FFN_TF, D_MODEL), lambda i, f: (f, 0))],
        out_specs=(pl.BlockSpec((tm, D_MODEL), lambda i, f: (i, 0)), small()),
        scratch_shapes=[pltpu.VMEM((D_MODEL, FFN_TF), BF16), pltpu.VMEM((D_MODEL, FFN_TF), BF16),
                        pltpu.VMEM((FFN_TF, D_MODEL), BF16)],
        compiler_params=_params(("arbitrary", "arbitrary")),
        name="ffn",
    )(x1, xn, x1s, xns, w_g, w_u, w_d)


def _block_diag_lanes(p):
    p4 = p.reshape(N_BLOCKS, GROUPS_PER_BLOCK, SSM_STATE, SSM_GROUP)
    eye = jnp.eye(GROUPS_PER_BLOCK, dtype=p.dtype)
    out = jnp.einsum("ohsc,gh->ogchs", p4, eye)
    return out.reshape(N_BLOCKS, LANES, BLOCK_STATE)


def kernel(x_prompt, x_sample, cache_k, cache_v, cache_logf, state_ssm_re, state_ssm_im, page_table,
           norm_mix_g, w_in, b_f, q_norm_g, k_norm_g, ssm_a_re, ssm_a_im, ssm_log_dt, ssm_b_re, ssm_b_im,
           ssm_c_re, ssm_c_im, ssm_d, w_glu, b_glu, w_br_ssm, w_br_att, w_out, norm_ffn_g,
           w_ffn_gate, w_ffn_up, w_ffn_down):
    batch, seq, _ = x_prompt.shape
    n_seq, n_tok, _ = x_sample.shape
    l = 0
    n_qkv = D_SSM + 3 * D_ATT

    w_in_l = w_in[l]
    n_slabs = w_in_l.shape[1] // PROJ_COLS
    n_tail = w_in_l.shape[1] - n_slabs * PROJ_COLS
    assert n_qkv == 2 * PROJ_COLS and n_tail == GATE_SHIFT == N_HEADS
    w_tail = jnp.pad(w_in_l[:, n_slabs * PROJ_COLS:], ((0, 0), (0, LANES - n_tail))).astype(BF16)
    b_f_pad = jnp.pad(b_f[l], (0, LANES - N_HEADS)).reshape(1, LANES)
    norm_g = norm_mix_g[l].reshape(1, D_MODEL)
    q_g = q_norm_g[l].reshape(1, HEAD_DIM)
    k_g = k_norm_g[l].reshape(1, HEAD_DIM)
    ns = BLOCK_STATE
    ar_row = ssm_a_re[l].reshape(N_BLOCKS, 1, ns)
    ai_row = ssm_a_im[l].reshape(N_BLOCKS, 1, ns)
    dt_row = jnp.repeat(ssm_log_dt[l], SSM_STATE).reshape(N_BLOCKS, 1, ns)
    bd_bre = _block_diag_lanes(ssm_b_re[l])
    bd_bim = _block_diag_lanes(ssm_b_im[l])
    bd_cre = _block_diag_lanes(ssm_c_re[l].transpose(0, 2, 1)).transpose(0, 2, 1)
    bd_cim = _block_diag_lanes(ssm_c_im[l].transpose(0, 2, 1)).transpose(0, 2, 1)
    d_row = ssm_d[l].reshape(1, D_SSM)
    w_glu_b = w_glu[l].astype(BF16)
    b_glu_r = b_glu[l].reshape(1, D_SSM)
    shift_pad = (GATE_SHIFT, MERGE_COLS - D_MODEL - GATE_SHIFT)
    w_bs = jnp.pad(w_br_ssm[l].astype(BF16), ((0, 0), shift_pad))
    w_ba = jnp.pad(w_br_att[l].astype(BF16), ((0, 0), shift_pad))
    w_o = jnp.pad(w_out[l].astype(BF16), (shift_pad, (0, 0)))
    n_g = norm_ffn_g[l].reshape(1, D_MODEL)
    w_g = w_ffn_gate[l]
    w_u = w_ffn_up[l]
    w_d = w_ffn_down[l]

    vrev, bsum, m_op, bmat, cmat, a1, a16 = _s5gen(ar_row, ai_row, dt_row, bd_bre, bd_bim, bd_cre, bd_cim)

    xp = x_prompt.reshape(batch * seq, D_MODEL)
    rows = n_seq * n_tok
    xs = x_sample.transpose(1, 0, 2).reshape(rows, D_MODEL)
    ((u, qb, k, kb, v, vt, gates, logf),
     (us, qbs, ks, kbs, vs, vbs, gates_s, logf_s)) = _inproj(xp, xs, norm_g, w_in_l, w_tail, b_f_pad, q_g, k_g, tm=512)
    caq, cak = _cumaug(logf, batch, seq)
    y_ssm, hre, him = _s5_prompt(u, vrev, bsum, m_op, a16, d_row, batch, seq)
    y_att = _attention(qb, caq, kb, cak, vt, batch, seq)
    x1, xn1 = _merge(y_ssm, y_att, gates, xp, w_glu_b, b_glu_r, w_bs, w_ba, w_o, n_g, tm=256)

    ys_ssm, hre_s, him_s = _s5_sample(us, state_ssm_re[l].reshape(n_seq, -1), state_ssm_im[l].reshape(n_seq, -1),
                                      bmat, cmat, a1, d_row, n_seq, n_tok)
    by_seq = lambda a: a.reshape(n_tok, n_seq, -1).transpose(1, 0, 2)
    th = n_tok * N_HEADS
    th_rows = lambda a: by_seq(a).reshape(n_seq, th, HEAD_DIM)
    pad_keys = lambda a: jnp.pad(th_rows(a), ((0, 0), (0, LANES - th), (0, 0)))
    lf_s = by_seq(logf_s[:, :N_HEADS]).reshape(n_seq, th)
    lfrow = jnp.pad(lf_s, ((0, 0), (0, LANES - th))).reshape(n_seq, 1, LANES)
    lfcol = lf_s.reshape(n_seq, th, 1)
    n_pool = cache_k.shape[1]
    ya_s = _decode_attention(page_table, th_rows(qbs), lfrow, lfcol, pad_keys(kbs), pad_keys(vbs),
                             cache_k, cache_v, cache_logf[l].reshape(n_pool, 1, PAGE_KEYS), n_tok)
    ya_s = ya_s.reshape(n_seq, n_tok, D_ATT).transpose(1, 0, 2).reshape(rows, D_ATT).astype(BF16)
    x1s, xn1s = _merge(ys_ssm, ya_s, gates_s, xs, w_glu_b, b_glu_r, w_bs, w_ba, w_o, n_g, tm=rows)
    y_p, y_s = _ffn(x1, xn1, x1s, xn1s, w_g, w_u, w_d, tm=1024)
    y_p = y_p.reshape(batch, seq, D_MODEL)
    y_s = y_s.reshape(n_tok, n_seq, D_MODEL).transpose(1, 0, 2)

    heads = lambda a, b_, t_: a.reshape(1, b_, t_, N_HEADS, HEAD_DIM)
    tok_major = lambda a: a.reshape(n_tok, n_seq, -1).transpose(1, 0, 2)
    return (
        y_p, y_s,
        heads(k, batch, seq), heads(v, batch, seq),
        logf[:, :N_HEADS].reshape(1, batch, seq, N_HEADS),
        hre.reshape(1, batch, N_GROUPS, SSM_STATE), him.reshape(1, batch, N_GROUPS, SSM_STATE),
        heads(tok_major(ks), n_seq, n_tok), heads(tok_major(vs), n_seq, n_tok),
        tok_major(logf_s[:, :N_HEADS]).reshape(1, n_seq, n_tok, N_HEADS),
        hre_s.reshape(1, n_seq, N_GROUPS, SSM_STATE), him_s.reshape(1, n_seq, N_GROUPS, SSM_STATE),
    )
```

```python
import functools
import math

import jax
import jax.numpy as jnp
import numpy as np
from jax import lax
from jax.experimental import pallas as pl
from jax.experimental.pallas import tpu as pltpu

F32 = jnp.float32
BF16 = jnp.bfloat16
HIGHEST = lax.Precision.HIGHEST

D_MODEL = 2048
D_SSM = D_MODEL // 2
SSM_GROUP = 16
N_GROUPS = D_SSM // SSM_GROUP
SSM_STATE = 64
HEAD_DIM = 128
N_HEADS = (D_MODEL // 2) // HEAD_DIM
D_ATT = N_HEADS * HEAD_DIM
D_FF = ((8 * D_MODEL + 3 * 256 - 1) // (3 * 256)) * 256
PAGE_SIZE = 128
RMS_EPS = 1e-6
NEG_INF = -1e30

LANES = 128
GROUPS_PER_BLOCK = LANES // SSM_GROUP
N_BLOCKS = N_GROUPS // GROUPS_PER_BLOCK
BLOCK_STATE = GROUPS_PER_BLOCK * SSM_STATE
CHUNK = 16
PROJ_COLS = 2048
GATE_SHIFT = N_HEADS
ATT_TK = 512
ATT_TQ = 2 * ATT_TK
VMEM_LIMIT = 56 * 1024 * 1024

LOG2E = math.log2(math.e)
QK_SCALE_LOG2 = (HEAD_DIM ** -0.5) * LOG2E

_NT = (((1,), (1,)), ((), ()))


def _params(sem):
    return pltpu.CompilerParams(dimension_semantics=sem, vmem_limit_bytes=VMEM_LIMIT)


def _gelu_exact(x):
    return 0.5 * x * (1.0 + lax.erf(x * math.sqrt(0.5)))


def _sigmoid(x):
    return 1.0 / (1.0 + jnp.exp(-x))


def _fold_lanes(x, op):
    acc = x[:, 0:LANES]
    for c in range(1, x.shape[1] // LANES):
        acc = op(acc, x[:, c * LANES:(c + 1) * LANES])
    return acc


def _row_max(x):
    return jnp.max(_fold_lanes(x, jnp.maximum), axis=-1, keepdims=True)


def _row_sum(x):
    return jnp.sum(_fold_lanes(x, jnp.add), axis=-1, keepdims=True)


def _log_sigmoid(x):
    return jnp.minimum(x, 0.0) - jnp.log1p(jnp.exp(-jnp.abs(x)))


def _head_norm(z, gain):
    outs = []
    for h in range(N_HEADS):
        blk = z[:, h * HEAD_DIM:(h + 1) * HEAD_DIM]
        ms = jnp.mean(blk * blk, axis=-1, keepdims=True)
        outs.append(blk * lax.rsqrt(ms + RMS_EPS) * gain)
    return jnp.concatenate(outs, axis=1)


def _proj_kernel(*refs, kind, nt, v_transposed):
    za, zb, zs, w_sc = refs[-4:]
    n_out = _PROJ_OUTPUTS[kind]
    in_p, in_s, w_ref = refs[0], refs[1], refs[2]
    extra = refs[3:len(refs) - 4 - 2 * n_out]
    out_p = refs[len(refs) - 4 - 2 * n_out:len(refs) - 4 - n_out]
    out_s = refs[len(refs) - 4 - n_out:len(refs) - 4]
    s = pl.program_id(0)
    half = PROJ_COLS // 2

    def group(in_ref, outs, transposed):
        if kind == "uq":
            g_ref, qg_ref = extra
            xn_ref, u_ref, qb_ref = outs

            def lhs():
                x = in_ref[...]
                ms = jnp.mean(x * x, axis=-1, keepdims=True)
                xn = (x * lax.rsqrt(ms + RMS_EPS) * g_ref[...]).astype(BF16)
                xn_ref[...] = xn
                return xn

            def epilogue(z):
                u_ref[...] = z[:, 0:half]
                qb_ref[...] = (_head_norm(z[:, half:], qg_ref[...]) * QK_SCALE_LOG2).astype(BF16)
        elif kind == "kv":
            (kg_ref,) = extra
            k_ref, kb_ref, v_ref, vb_ref = outs
            lhs = lambda: in_ref[...]

            def epilogue(z):
                kn = _head_norm(z[:, 0:half], kg_ref[...])
                k_ref[...] = kn
                kb_ref[...] = kn.astype(BF16)
                v = z[:, half:]
                v_ref[...] = v
                if transposed:
                    vt = v.T
                    for c in range(vb_ref.shape[0]):
                        vb_ref[c] = vt[:, c * ATT_TK:(c + 1) * ATT_TK].astype(BF16)
                else:
                    vb_ref[...] = v.astype(BF16)
        elif kind == "gate_ssm":
            (bf_ref,) = extra
            gate_ref, logf_ref = outs
            lhs = lambda: in_ref[...]

            def epilogue(z):
                gate_ref[...] = _sigmoid(z).astype(BF16)
                logf_ref[...] = _log_sigmoid(z[:, 0:LANES] + bf_ref[...])
        else:
            (wt_ref,) = extra
            gate_ref, tail_ref = outs

            def lhs():
                xn = in_ref[...]
                tail_ref[...] = _sigmoid(jnp.dot(xn, wt_ref[...], preferred_element_type=F32)).astype(BF16)
                return xn

            def epilogue(z):
                gate_ref[...] = _sigmoid(z).astype(BF16)
        return lhs, epilogue

    lhs_p, epilogue_p = group(in_p, out_p, v_transposed)
    lhs_s, epilogue_s = group(in_s, out_s, False)

    @pl.when(s == 0)
    def _():
        zb[...] = jnp.zeros_like(zb)
        w_sc[...] = w_ref[...].astype(BF16)

    def body(z_write, z_read):
        epilogue_p(z_read[...])
        z_write[...] = jnp.dot(lhs_p(), w_sc[...], preferred_element_type=F32)

    pl.when((s < nt) & (s % 2 == 0))(lambda: body(za, zb))
    pl.when((s < nt) & (s % 2 == 1))(lambda: body(zb, za))

    @pl.when(s == nt)
    def _():
        epilogue_p((za if (nt - 1) % 2 == 0 else zb)[...])
        zs[...] = jnp.dot(lhs_s(), w_sc[...], preferred_element_type=F32)

    @pl.when(s == nt + 1)
    def _():
        epilogue_s(zs[...])


_PROJ_OUTPUTS = {"uq": 3, "kv": 4, "gate_ssm": 2, "gate_att": 2}


def _inproj(x, xs, norm_g, w_in, layer, w_tail, b_f, q_g, k_g, tm):
    t, ts = x.shape[0], xs.shape[0]
    nt = t // tm
    hi = nt - 1
    cur = lambda s: (jnp.minimum(s, hi), 0)
    lag = lambda s: (jnp.clip(s - 1, 0, hi), 0)
    lag3 = lambda s: (jnp.clip(s - 1, 0, hi), 0, 0)
    const = lambda s: (0, 0)
    sds = jax.ShapeDtypeStruct

    def call(kind, slab_index, row_inputs, extra, extra_specs, widths, dtypes, lagged, vt=False):
        in_width = row_inputs[0].shape[1]
        in_specs = [pl.BlockSpec((tm, in_width), cur), pl.BlockSpec((ts, in_width), const),
                    pl.BlockSpec((None, D_MODEL, PROJ_COLS), lambda s: (layer, 0, slab_index),
                                 pipeline_mode=pl.Buffered(1))]
        out_shape, out_specs = [], []
        for rows_n, block_n, prompt in ((t, tm, True), (ts, ts, False)):
            for w, dt, lg in zip(widths, dtypes, lagged):
                if vt and prompt and w is None:
                    out_shape.append(sds((t // ATT_TK, D_ATT, ATT_TK), BF16))
                    out_specs.append(pl.BlockSpec((tm // ATT_TK, D_ATT, ATT_TK), lag3))
                    continue
                w = D_ATT if w is None else w
                out_shape.append(sds((rows_n, w), dt))
                index = (lag if lg else cur) if prompt else const
                out_specs.append(pl.BlockSpec((block_n, w), index))
        outs = pl.pallas_call(
            functools.partial(_proj_kernel, kind=kind, nt=nt, v_transposed=vt),
            out_shape=tuple(out_shape),
            grid=(nt + 2,),
            in_specs=in_specs + extra_specs,
            out_specs=tuple(out_specs),
            scratch_shapes=[pltpu.VMEM((tm, PROJ_COLS), F32), pltpu.VMEM((tm, PROJ_COLS), F32),
                            pltpu.VMEM((ts, PROJ_COLS), F32), pltpu.VMEM((D_MODEL, PROJ_COLS), BF16)],
            compiler_params=_params(("arbitrary",)),
            name="proj_" + kind,
        )(*row_inputs, w_in, *extra)
        n = len(widths)
        return outs[:n], outs[n:]

    gain = lambda w: pl.BlockSpec((1, w), const)
    (xn, u, qb), (xns, us, qbs) = call(
        "uq", 0, (x, xs), (norm_g, q_g), [gain(D_MODEL), gain(HEAD_DIM)],
        (D_MODEL, D_SSM, D_ATT), (BF16, F32, BF16), (False, True, True))
    (k, kb, v, vb), (ks, kbs, vs, vbs) = call(
        "kv", 1, (xn, xns), (k_g,), [gain(HEAD_DIM)],
        (D_ATT, D_ATT, D_ATT, None), (F32, BF16, F32, BF16), (True, True, True, True), vt=True)
    (gate_s, logf), (gate_ss, logfs) = call(
        "gate_ssm", 2, (xn, xns), (b_f,), [gain(LANES)],
        (PROJ_COLS, LANES), (BF16, F32), (True, True))
    (gate_a, gate_t), (gate_as, gate_ts) = call(
        "gate_att", 3, (xn, xns), (w_tail,), [pl.BlockSpec((D_MODEL, LANES), const)],
        (PROJ_COLS, LANES), (BF16, BF16), (True, False))
    return ((u, qb, k, kb, v, vb, (gate_s, gate_a, gate_t), logf),
            (us, qbs, ks, kbs, vs, vbs, (gate_ss, gate_as, gate_ts), logfs))


CUM_ROWS = 256


def _cumaug_kernel(lf_ref, eq_ref, ek_ref, caq_ref, cak_ref, carry_sc):
    @pl.when(pl.program_id(1) == 0)
    def _():
        carry_sc[...] = jnp.zeros_like(carry_sc)

    lf = lf_ref[...]
    r = lax.broadcasted_iota(jnp.int32, (CUM_ROWS, CUM_ROWS), 0)
    c = lax.broadcasted_iota(jnp.int32, (CUM_ROWS, CUM_ROWS), 1)
    tri = (c <= r).astype(BF16)
    lf_hi = lf.astype(BF16)
    lf_r = lf - lf_hi.astype(F32)
    lf_mid = lf_r.astype(BF16)
    lf_lo = (lf_r - lf_mid.astype(F32)).astype(BF16)
    parts = jnp.dot(tri, jnp.concatenate([lf_hi, lf_mid, lf_lo], axis=1), preferred_element_type=F32)
    cs = (parts[:, 0:LANES] + parts[:, LANES:2 * LANES] + parts[:, 2 * LANES:3 * LANES]) + carry_sc[...]
    carry_sc[...] = cs[CUM_ROWS - 1:CUM_ROWS, :]
    cs2 = cs * LOG2E
    hi = cs2.astype(BF16).astype(F32)
    r1 = cs2 - hi
    mid = r1.astype(BF16).astype(F32)
    lo = (r1 - mid).astype(BF16).astype(F32)
    lane = lax.broadcasted_iota(jnp.int32, (CUM_ROWS, LANES), 1)
    packed = jnp.where(lane < N_HEADS, hi,
                       jnp.where(lane < 2 * N_HEADS, pltpu.roll(mid, N_HEADS, axis=1),
                                 jnp.where(lane < 3 * N_HEADS, pltpu.roll(lo, 2 * N_HEADS, axis=1),
                                           jnp.where(lane == 3 * N_HEADS, 1.0, 0.0)))).astype(BF16)
    caq_ref[...] = jnp.dot(packed, eq_ref[...], preferred_element_type=F32).astype(BF16)
    cak_ref[...] = jnp.dot(packed, ek_ref[...], preferred_element_type=F32).astype(BF16)


def _placement_matrices():
    eq = np.zeros((LANES, D_ATT), np.float32)
    ek = np.zeros((LANES, D_ATT), np.float32)
    for h in range(N_HEADS):
        base = h * HEAD_DIM
        for piece in range(3):
            eq[piece * N_HEADS + h, base + piece] = 1.0
            ek[piece * N_HEADS + h, base + 3 + piece] = -1.0
            eq[3 * N_HEADS, base + 3 + piece] = 1.0
            ek[3 * N_HEADS, base + piece] = 1.0
    return jnp.asarray(eq, BF16), jnp.asarray(ek, BF16)


def _cumaug(logf, batch, seq):
    nb = seq // CUM_ROWS
    eq, ek = _placement_matrices()
    const = lambda: pl.BlockSpec((LANES, D_ATT), lambda b, i: (0, 0))
    return pl.pallas_call(
        _cumaug_kernel,
        out_shape=(jax.ShapeDtypeStruct((batch * seq, D_ATT), BF16),
                   jax.ShapeDtypeStruct((batch * seq, D_ATT), BF16)),
        grid=(batch, nb),
        in_specs=[pl.BlockSpec((CUM_ROWS, LANES), lambda b, i: (b * nb + i, 0)), const(), const()],
        out_specs=(pl.BlockSpec((CUM_ROWS, D_ATT), lambda b, i: (b * nb + i, 0)),
                   pl.BlockSpec((CUM_ROWS, D_ATT), lambda b, i: (b * nb + i, 0))),
        scratch_shapes=[pltpu.VMEM((1, LANES), F32)],
        compiler_params=_params(("arbitrary", "arbitrary")),
        name="cumaug",
    )(logf, eq, ek)


ACC_ROWS = HEAD_DIM + 16


def _attn_kernel(q_ref, cq_ref, k_ref, ck_ref, vt_ref, o_ref, m_sc, acc_sc, sa_sc, sb_sc):
    i = pl.program_id(2)
    tk = ATT_TK
    m_sc[...] = jnp.full(m_sc.shape, NEG_INF, F32)
    acc_sc[...] = jnp.zeros_like(acc_sc)
    sub = lax.broadcasted_iota(jnp.int32, (ACC_ROWS - HEAD_DIM, tk), 0)
    ones_rows = jnp.where(sub == 0, 1.0, 0.0).astype(BF16)

    def logits(j, s_ref, lo):
        ks = pl.multiple_of(j * tk, tk)
        kk = jnp.concatenate([k_ref[pl.ds(ks, tk), :], ck_ref[pl.ds(ks, tk), :]], axis=1)
        qq = jnp.concatenate([q_ref[lo:, :], cq_ref[lo:, :]], axis=1)
        s_ref[:, lo:] = lax.dot_general(kk, qq, _NT, preferred_element_type=F32)

    def consume(j, s_ref, lo, masked):
        s = s_ref[:, lo:]
        if masked:
            key = lax.broadcasted_iota(jnp.int32, s.shape, 0)
            qry = lax.broadcasted_iota(jnp.int32, s.shape, 1)
            s = jnp.where(key <= qry, s, NEG_INF)
        vt = jnp.concatenate([vt_ref[j], ones_rows], axis=0)
        m_old = m_sc[:, lo:]
        m_new = jnp.maximum(m_old, jnp.max(s, axis=0, keepdims=True))
        alpha = jnp.exp2(m_old - m_new)
        p = jnp.exp2(s - m_new).astype(BF16)
        acc_sc[:, lo:] = alpha * acc_sc[:, lo:] + jnp.dot(vt, p, preferred_element_type=F32)
        m_sc[:, lo:] = m_new

    logits(0, sa_sc, 0)

    def pair(jj, carry):
        logits(2 * jj + 1, sb_sc, 0)
        consume(2 * jj, sa_sc, 0, False)
        logits(2 * jj + 2, sa_sc, 0)
        consume(2 * jj + 1, sb_sc, 0, False)
        return carry

    lax.fori_loop(0, i, pair, 0)
    logits(2 * i + 1, sb_sc, tk)
    consume(2 * i, sa_sc, 0, True)
    consume(2 * i + 1, sb_sc, tk, True)

    acc = acc_sc[...]
    out_t = acc[0:HEAD_DIM, :] / acc[HEAD_DIM:HEAD_DIM + 1, :]
    o_ref[...] = out_t.T.astype(BF16)


def _attention(qb, caq, kb, cak, vt, batch, seq):
    nq = seq // ATT_TQ
    nkb = seq // ATT_TK
    qspec = lambda: pl.BlockSpec((ATT_TQ, HEAD_DIM), lambda b, h, i: (b * nq + i, h))
    kspec = lambda: pl.BlockSpec((seq, HEAD_DIM), lambda b, h, i: (b, h))
    return pl.pallas_call(
        _attn_kernel,
        out_shape=jax.ShapeDtypeStruct((batch * seq, D_ATT), BF16),
        grid=(batch, N_HEADS, nq),
        in_specs=[qspec(), qspec(), kspec(), kspec(),
                  pl.BlockSpec((nkb, HEAD_DIM, ATT_TK), lambda b, h, i: (b, h, 0))],
        out_specs=qspec(),
        scratch_shapes=[pltpu.VMEM((1, ATT_TQ), F32), pltpu.VMEM((ACC_ROWS, ATT_TQ), F32),
                        pltpu.VMEM((ATT_TK, ATT_TQ), F32), pltpu.VMEM((ATT_TK, ATT_TQ), F32)],
        compiler_params=_params(("arbitrary", "arbitrary", "arbitrary")),
        name="fox_prompt",
    )(qb, caq, kb, cak, vt)


def _discretise(a_re, a_im, log_dt):
    dt = jnp.exp(log_dt)
    mag = jnp.exp(dt * a_re)
    ang = dt * a_im
    abr = mag * jnp.cos(ang)
    abi = mag * jnp.sin(ang)
    e_re = abr - 1.0
    e_im = abi
    inv_den = 1.0 / (a_re * a_re + a_im * a_im)
    f_re = (e_re * a_re + e_im * a_im) * inv_den
    f_im = (e_im * a_re - e_re * a_im) * inv_den
    return abr, abi, f_re, f_im


def _powers(abr, abi, n):
    pr, pi = [jnp.ones_like(abr)], [jnp.zeros_like(abi)]
    for _ in range(n):
        r, i = pr[-1], pi[-1]
        pr.append(r * abr - i * abi)
        pi.append(r * abi + i * abr)
    return pr, pi


def _s5gen_kernel(ar_row, ai_row, dt_row, bre_ref, bim_ref, cre_ref, cim_ref,
                  vrev_ref, bsum_ref, m_ref, bmat_ref, cmat_ref, a1_ref, a16_ref):
    ns = BLOCK_STATE
    abr, abi, f_re, f_im = _discretise(ar_row[...], ai_row[...], dt_row[...])
    bre, bim = bre_ref[...], bim_ref[...]
    bbr = f_re * bre - f_im * bim
    bbi = f_re * bim + f_im * bre
    bmat = jnp.concatenate([bbr, bbi], axis=1)
    bmat_ref[...] = bmat
    pr, pi = _powers(abr, abi, CHUNK)
    a1_ref[...] = jnp.concatenate([pr[1], pi[1]], axis=1)
    a16_ref[...] = jnp.concatenate([pr[CHUNK], pi[CHUNK]], axis=1)
    for i in range(CHUNK):
        r, im = pr[CHUNK - 1 - i], pi[CHUNK - 1 - i]
        blk = jnp.concatenate([r * bbr - im * bbi, r * bbi + im * bbr], axis=1)
        bsum_ref[i * LANES:(i + 1) * LANES, :] = blk.astype(BF16)
    npow = CHUNK + 1
    stacked = jnp.concatenate(pr + pi + [jnp.zeros((LANES - 2 * npow, ns), F32)], axis=0)
    pt = stacked.T
    qr = [pt[:, t:t + 1] for t in range(npow)]
    qi = [pt[:, npow + t:npow + t + 1] for t in range(npow)]
    cre, cim = cre_ref[...], cim_ref[...]
    vrev_ref[(CHUNK - 1) * LANES:CHUNK * LANES, 0:LANES] = jnp.zeros((LANES, LANES), BF16)
    bmat_hi = bmat.astype(BF16)
    for tau in range(CHUNK + 1):
        blk = jnp.concatenate([cre * qr[tau] - cim * qi[tau],
                               -(cre * qi[tau] + cim * qr[tau])], axis=0)
        if tau == 0:
            cmat_ref[...] = blk
        else:
            m_ref[:, (tau - 1) * LANES:tau * LANES] = blk.astype(BF16)
        if tau < CHUNK:
            w = jnp.dot(bmat_hi, blk.astype(BF16), preferred_element_type=F32).astype(BF16)
            k = CHUNK - 1 - tau
            vrev_ref[k * LANES:(k + 1) * LANES, LANES:2 * LANES] = w
            if k >= 1:
                vrev_ref[(k - 1) * LANES:k * LANES, 0:LANES] = w


def _s5gen(ar_row, ai_row, dt_row, bd_bre, bd_bim, bd_cre, bd_cim):
    ns = BLOCK_STATE
    b3 = lambda s1, s2: pl.BlockSpec((None, s1, s2), lambda o: (o, 0, 0))
    out_shape = (
        jax.ShapeDtypeStruct((N_BLOCKS, CHUNK * LANES, 2 * LANES), BF16),
        jax.ShapeDtypeStruct((N_BLOCKS, CHUNK * LANES, 2 * ns), BF16),
        jax.ShapeDtypeStruct((N_BLOCKS, 2 * ns, CHUNK * LANES), BF16),
        jax.ShapeDtypeStruct((N_BLOCKS, LANES, 2 * ns), F32),
        jax.ShapeDtypeStruct((N_BLOCKS, 2 * ns, LANES), F32),
        jax.ShapeDtypeStruct((N_BLOCKS, 1, 2 * ns), F32),
        jax.ShapeDtypeStruct((N_BLOCKS, 1, 2 * ns), F32),
    )
    return pl.pallas_call(
        _s5gen_kernel,
        out_shape=out_shape,
        grid=(N_BLOCKS,),
        in_specs=[b3(1, ns), b3(1, ns), b3(1, ns),
                  b3(LANES, ns), b3(LANES, ns), b3(ns, LANES), b3(ns, LANES)],
        out_specs=(b3(CHUNK * LANES, 2 * LANES), b3(CHUNK * LANES, 2 * ns), b3(2 * ns, CHUNK * LANES),
                   b3(LANES, 2 * ns), b3(2 * ns, LANES), b3(1, 2 * ns), b3(1, 2 * ns)),
        compiler_params=_params(("arbitrary",)),
        name="s5_operators",
    )(ar_row, ai_row, dt_row, bd_bre, bd_bim, bd_cre, bd_cim)


def _s5_prompt_kernel(u_ref, vrev_ref, bsum_ref, m_ref, a16_ref, d_ref,
                      y_ref, hre_ref, him_ref, ucat_sc, s_sc, hin_sc, *, n_chunks):
    ns = BLOCK_STATE
    for i in range(CHUNK):
        ucat_sc[:, i * LANES:(i + 1) * LANES] = u_ref[pl.ds(i, n_chunks, stride=CHUNK), :].astype(BF16)
    s_sc[...] = jnp.dot(ucat_sc[...], bsum_ref[...], preferred_element_type=F32)
    ar = a16_ref[:, 0:ns]
    ai = a16_ref[:, ns:2 * ns]

    def step(k, carry):
        hr, hi = carry
        hin_sc[pl.ds(k, 1), 0:ns] = hr
        hin_sc[pl.ds(k, 1), ns:2 * ns] = hi
        sr = s_sc[pl.ds(k, 1), 0:ns]
        si = s_sc[pl.ds(k, 1), ns:2 * ns]
        return ar * hr - ai * hi + sr, ar * hi + ai * hr + si

    zero = jnp.zeros((1, ns), F32)
    hr, hi = lax.fori_loop(0, n_chunks, step, (zero, zero))
    hre_ref[...] = hr
    him_ref[...] = hi
    hin = hin_sc[...].astype(BF16)
    d = d_ref[...]
    for jp in range(CHUNK // 2):
        j = 2 * jp
        kk = (j + 2) * LANES
        acc = jnp.dot(ucat_sc[:, 0:kk], vrev_ref[(CHUNK - 2 - j) * LANES:, :],
                      preferred_element_type=F32)
        acc = acc + jnp.dot(hin, m_ref[:, j * LANES:(j + 2) * LANES], preferred_element_type=F32)
        for jj in range(2):
            uj = u_ref[pl.ds(j + jj, n_chunks, stride=CHUNK), :]
            val = acc[:, jj * LANES:(jj + 1) * LANES] + d * uj
            y_ref[pl.ds(j + jj, n_chunks, stride=CHUNK), :] = _gelu_exact(val)


def _s5_prompt(u, vrev, bsum, m, a16, d, batch, seq):
    ns = BLOCK_STATE
    n_chunks = seq // CHUNK
    w3 = lambda s1, s2: pl.BlockSpec((None, s1, s2), lambda o, b: (o, 0, 0))
    return pl.pallas_call(
        functools.partial(_s5_prompt_kernel, n_chunks=n_chunks),
        out_shape=(jax.ShapeDtypeStruct((batch * seq, D_SSM), F32),
                   jax.ShapeDtypeStruct((batch, 1, N_GROUPS * SSM_STATE), F32),
                   jax.ShapeDtypeStruct((batch, 1, N_GROUPS * SSM_STATE), F32)),
        grid=(N_BLOCKS, batch),
        in_specs=[pl.BlockSpec((seq, LANES), lambda o, b: (b, o)),
                  w3(CHUNK * LANES, 2 * LANES), w3(CHUNK * LANES, 2 * ns), w3(2 * ns, CHUNK * LANES),
                  w3(1, 2 * ns),
                  pl.BlockSpec((1, LANES), lambda o, b: (0, o))],
        out_specs=(pl.BlockSpec((seq, LANES), lambda o, b: (b, o)),
                   pl.BlockSpec((None, 1, ns), lambda o, b: (b, 0, o)),
                   pl.BlockSpec((None, 1, ns), lambda o, b: (b, 0, o))),
        scratch_shapes=[pltpu.VMEM((n_chunks, CHUNK * LANES), BF16),
                        pltpu.VMEM((n_chunks, 2 * ns), F32),
                        pltpu.VMEM((n_chunks, 2 * ns), F32)],
        compiler_params=_params(("arbitrary", "arbitrary")),
        name="s5_prompt",
    )(u, vrev, bsum, m, a16, d)


def _s5_sample_kernel(u_ref, h0r_ref, h0i_ref, bmat_ref, cmat_ref, a1_ref, d_ref,
                      y_ref, hre_ref, him_ref, *, n_seq, n_tok):
    ns = BLOCK_STATE
    u = u_ref[...]
    bu = jnp.dot(u, bmat_ref[...], precision=HIGHEST, preferred_element_type=F32)
    ar = a1_ref[:, 0:ns]
    ai = a1_ref[:, ns:2 * ns]
    hr, hi = h0r_ref[...], h0i_ref[...]
    hs = []
    for t in range(n_tok):
        br = bu[t * n_seq:(t + 1) * n_seq, 0:ns]
        bi = bu[t * n_seq:(t + 1) * n_seq, ns:2 * ns]
        hr, hi = ar * hr - ai * hi + br, ar * hi + ai * hr + bi
        hs.append(jnp.concatenate([hr, hi], axis=1))
    hcat = jnp.concatenate(hs, axis=0)
    val = jnp.dot(hcat, cmat_ref[...], precision=HIGHEST, preferred_element_type=F32) + d_ref[...] * u
    y_ref[...] = _gelu_exact(val)
    hre_ref[...] = hr
    him_ref[...] = hi


def _s5_sample(u, h0r, h0i, bmat, cmat, a1, d, n_seq, n_tok):
    ns = BLOCK_STATE
    rows = n_seq * n_tok
    w3 = lambda s1, s2: pl.BlockSpec((None, s1, s2), lambda o: (o, 0, 0))
    col = lambda r, c: pl.BlockSpec((r, c), lambda o: (0, o))
    return pl.pallas_call(
        functools.partial(_s5_sample_kernel, n_seq=n_seq, n_tok=n_tok),
        out_shape=(jax.ShapeDtypeStruct((rows, D_SSM), F32),
                   jax.ShapeDtypeStruct((n_seq, N_GROUPS * SSM_STATE), F32),
                   jax.ShapeDtypeStruct((n_seq, N_GROUPS * SSM_STATE), F32)),
        grid=(N_BLOCKS,),
        in_specs=[col(rows, LANES), col(n_seq, ns), col(n_seq, ns),
                  w3(LANES, 2 * ns), w3(2 * ns, LANES), w3(1, 2 * ns), col(1, LANES)],
        out_specs=(col(rows, LANES), col(n_seq, ns), col(n_seq, ns)),
        compiler_params=_params(("arbitrary",)),
        name="s5_sample",
    )(u, h0r, h0i, bmat, cmat, a1, d)


PAGES_PER_STEP = 16
PAGE_GROUP = 4

PAGE_KEYS = PAGE_SIZE * N_HEADS


def _decode_kernel(pt_ref, q_ref, lfrow_ref, lfcol_ref, kn_ref, vn_ref, *refs, n_tok, n_steps):
    g = PAGES_PER_STEP
    k_refs = refs[0:g]
    v_refs = refs[g:2 * g]
    lf_refs = refs[2 * g:3 * g]
    o_ref = refs[3 * g]
    m_sc, l_sc, acc_sc, carry_sc = refs[3 * g + 1:]
    j = pl.program_id(1)
    rows = n_tok * N_HEADS
    head_mask = N_HEADS - 1
    head_shift = N_HEADS.bit_length() - 1

    @pl.when(j == 0)
    def _():
        m_sc[...] = jnp.full(m_sc.shape, NEG_INF, F32)
        l_sc[...] = jnp.zeros_like(l_sc)
        acc_sc[...] = jnp.zeros_like(acc_sc)
        carry_sc[...] = jnp.zeros_like(carry_sc)

    q = q_ref[...]
    lfcol = lfcol_ref[...]
    run = jnp.zeros((N_HEADS, 1), F32)
    pieces = []
    for t in range(n_tok):
        run = run + lfcol[t * N_HEADS:(t + 1) * N_HEADS, :]
        pieces.append(run)
    c_col = jnp.concatenate(pieces, axis=0)

    def online(s_blocks, v_blocks):
        m = m_sc[...]
        smax = s_blocks[0]
        for s in s_blocks[1:]:
            smax = jnp.maximum(smax, s)
        m_new = jnp.maximum(m, _row_max(smax))
        alpha = jnp.exp2(m - m_new)
        psum = None
        pv = None
        for s, vblk in zip(s_blocks, v_blocks):
            p = jnp.exp2(s - m_new)
            psum = p if psum is None else psum + p
            d = jnp.dot(p.astype(BF16), vblk, preferred_element_type=F32)
            pv = d if pv is None else pv + d
        l_sc[...] = alpha * l_sc[...] + _row_sum(psum)
        acc_sc[...] = alpha * acc_sc[...] + pv
        m_sc[...] = m_new

    col = lax.broadcasted_iota(jnp.int32, (rows, PAGE_KEYS), 1)
    row = lax.broadcasted_iota(jnp.int32, (rows, PAGE_KEYS), 0)
    own = (col & head_mask) == (row & head_mask)
    fixed = jnp.where(own, c_col * LOG2E, NEG_INF)

    lf = jnp.concatenate([lf_refs[pg][...] for pg in range(g)], axis=0)
    lane = lax.broadcasted_iota(jnp.int32, (g, PAGE_KEYS), 1)
    suffix = lf
    total = lf
    sh = N_HEADS
    while sh < PAGE_KEYS:
        suffix = suffix + jnp.where(lane + sh < PAGE_KEYS, pltpu.roll(suffix, PAGE_KEYS - sh, axis=1), 0.0)
        total = total + pltpu.roll(total, PAGE_KEYS - sh, axis=1)
        sh *= 2
    carry = carry_sc[...]
    past = []
    for pg in range(g):
        past.append((suffix[pg:pg + 1, :] - lf[pg:pg + 1, :] + carry) * LOG2E)
        carry = carry + total[pg:pg + 1, :]
    carry_sc[...] = carry

    def logits(pages):
        out = []
        for pg in pages:
            kp = k_refs[pg][...].reshape(PAGE_KEYS, HEAD_DIM).astype(BF16)
            s = lax.dot_general(q, kp, _NT, preferred_element_type=F32)
            out.append(s + fixed + past[pg])
        return out

    def values(pages):
        return [v_refs[pg][...].reshape(PAGE_KEYS, HEAD_DIM).astype(BF16) for pg in pages]

    groups = [list(range(a, a + PAGE_GROUP)) for a in range(0, g, PAGE_GROUP)]
    s_next = logits(groups[0])
    for gi, pages in enumerate(groups):
        s_cur = s_next
        if gi + 1 < len(groups):
            s_next = logits(groups[gi + 1])
        online(s_cur, values(pages))

    @pl.when(j == n_steps - 1)
    def _():
        ln = lax.broadcasted_iota(jnp.int32, (1, LANES), 1)
        c_row = lfrow_ref[...]
        sh2 = N_HEADS
        while sh2 < rows:
            c_row = c_row + jnp.where(ln >= sh2, pltpu.roll(c_row, sh2, axis=1), 0.0)
            sh2 *= 2
        cl = lax.broadcasted_iota(jnp.int32, (rows, LANES), 1)
        rw = lax.broadcasted_iota(jnp.int32, (rows, LANES), 0)
        valid = ((cl < rows) & ((cl & head_mask) == (rw & head_mask))
                 & ((cl >> head_shift) <= (rw >> head_shift)))
        s = lax.dot_general(q, kn_ref[...], _NT, preferred_element_type=F32)
        s = jnp.where(valid, s + (c_col - c_row) * LOG2E, NEG_INF)
        online([s], [vn_ref[...]])
        o_ref[...] = acc_sc[...] / l_sc[...]


def _decode_attention(page_table, q, lfrow, lfcol, kn, vn, cache_k, cache_v, cache_lf, n_tok):
    n_seq, n_pages = page_table.shape
    g = PAGES_PER_STEP
    n_steps = n_pages // g
    rows = n_tok * N_HEADS

    def page5(pg):
        return lambda b, j, pt: (0, pt[b * n_pages + (n_pages - 1 - (j * g + pg))], 0, 0, 0)

    def page3(pg):
        return lambda b, j, pt: (pt[b * n_pages + (n_pages - 1 - (j * g + pg))], 0, 0)

    seq3 = lambda s1, s2: pl.BlockSpec((None, s1, s2), lambda b, j, pt: (b, 0, 0))
    kv_block = (None, None, PAGE_SIZE, N_HEADS, HEAD_DIM)
    in_specs = [seq3(rows, HEAD_DIM), seq3(1, LANES), seq3(rows, 1), seq3(LANES, HEAD_DIM), seq3(LANES, HEAD_DIM)]
    in_specs += [pl.BlockSpec(kv_block, page5(pg)) for pg in range(g)]
    in_specs += [pl.BlockSpec(kv_block, page5(pg)) for pg in range(g)]
    in_specs += [pl.BlockSpec((None, 1, PAGE_KEYS), page3(pg)) for pg in range(g)]
    grid_spec = pltpu.PrefetchScalarGridSpec(
        num_scalar_prefetch=1,
        grid=(n_seq, n_steps),
        in_specs=in_specs,
        out_specs=pl.BlockSpec((None, rows, HEAD_DIM), lambda b, j, pt: (b, 0, 0)),
        scratch_shapes=[pltpu.VMEM((rows, 1), F32),
                        pltpu.VMEM((rows, 1), F32),
                        pltpu.VMEM((rows, HEAD_DIM), F32),
                        pltpu.VMEM((1, PAGE_KEYS), F32)],
    )
    return pl.pallas_call(
        functools.partial(_decode_kernel, n_tok=n_tok, n_steps=n_steps),
        out_shape=jax.ShapeDtypeStruct((n_seq, rows, HEAD_DIM), F32),
        grid_spec=grid_spec,
        compiler_params=_params(("arbitrary", "arbitrary")),
        name="fox_sample",
    )(page_table.reshape(-1), q, lfrow, lfcol, kn, vn,
      *([cache_k] * g), *([cache_v] * g), *([cache_lf] * g))


def _merge_kernel(y_ref, ya_ref, gs_ref, ga_ref, gt_ref, x_ref, wglu_ref, bglu_ref, wbs_ref, wba_ref, wo_ref,
                  ng_ref, x1_ref, xn_ref):
    y = y_ref[...]
    gl = jnp.dot(y.astype(BF16), wglu_ref[...], preferred_element_type=F32) + bglu_ref[...]
    ys = (y * _sigmoid(gl)).astype(BF16)
    ms = jnp.dot(ys, wbs_ref[...], preferred_element_type=F32)
    ma = jnp.dot(ya_ref[...], wba_ref[...], preferred_element_type=F32)
    ga = ga_ref[...]
    gate_s = jnp.concatenate([gs_ref[...], ga[:, 0:LANES]], axis=1).astype(F32)
    gate_a = jnp.concatenate([ga, gt_ref[...]], axis=1).astype(F32)
    merged = gate_s * ms + gate_a * ma
    x1 = x_ref[...] + jnp.dot(merged.astype(BF16), wo_ref[...], preferred_element_type=F32)
    x1_ref[...] = x1
    ms1 = jnp.mean(x1 * x1, axis=-1, keepdims=True)
    xn_ref[...] = (x1 * lax.rsqrt(ms1 + RMS_EPS) * ng_ref[...]).astype(BF16)


MERGE_COLS = D_MODEL + LANES


def _merge(y, ya, gates, x, w_glu, b_glu, w_bs, w_ba, w_o, n_g, tm):
    t = x.shape[0]
    gate_s, gate_a, gate_t = gates
    row = lambda c: pl.BlockSpec((tm, c), lambda i: (i, 0))
    const = lambda r, c: pl.BlockSpec((r, c), lambda i: (0, 0), pipeline_mode=pl.Buffered(1))
    return pl.pallas_call(
        _merge_kernel,
        out_shape=(jax.ShapeDtypeStruct((t, D_MODEL), F32), jax.ShapeDtypeStruct((t, D_MODEL), BF16)),
        grid=(t // tm,),
        in_specs=[row(D_SSM), row(D_ATT), row(PROJ_COLS), row(PROJ_COLS), row(LANES), row(D_MODEL),
                  const(D_SSM, D_SSM), const(1, D_SSM), const(D_SSM, MERGE_COLS), const(D_ATT, MERGE_COLS),
                  const(MERGE_COLS, D_MODEL), const(1, D_MODEL)],
        out_specs=(row(D_MODEL), row(D_MODEL)),
        compiler_params=_params(("arbitrary",)),
        name="merge_out",
    )(y, ya, gate_s, gate_a, gate_t, x, w_glu, b_glu, w_bs, w_ba, w_o, n_g)


FFN_TF = 256


def _ffn_kernel(x1_ref, xn_ref, x1s_ref, xns_ref, wg_ref, wu_ref, wd_ref, o_ref, os_ref, wg_sc, wu_sc, wd_sc):
    i, f = pl.program_id(0), pl.program_id(1)
    @pl.when(f == 0)
    def _():
        o_ref[...] = x1_ref[...]

    xn = xn_ref[...]
    wg_sc[...] = wg_ref[...].astype(BF16)
    a = jnp.dot(xn, wg_sc[...], preferred_element_type=F32)
    wu_sc[...] = wu_ref[...].astype(BF16)
    b = jnp.dot(xn, wu_sc[...], preferred_element_type=F32)
    wd_sc[...] = wd_ref[...].astype(BF16)
    o_ref[...] += jnp.dot((a * _sigmoid(a) * b).astype(BF16), wd_sc[...], preferred_element_type=F32)

    @pl.when(i == pl.num_programs(0) - 1)
    def _():
        @pl.when(f == 0)
        def _():
            os_ref[...] = x1s_ref[...]

        xs = xns_ref[...]
        a_s = jnp.dot(xs, wg_sc[...], preferred_element_type=F32)
        b_s = jnp.dot(xs, wu_sc[...], preferred_element_type=F32)
        os_ref[...] += jnp.dot((a_s * _sigmoid(a_s) * b_s).astype(BF16), wd_sc[...], preferred_element_type=F32)


def _ffn(x1, xn, x1s, xns, w_g, w_u, w_d, tm):
    t = x1.shape[0]
    ts = x1s.shape[0]
    small = lambda: pl.BlockSpec((ts, D_MODEL), lambda i, f: (0, 0))
    return pl.pallas_call(
        _ffn_kernel,
        out_shape=(jax.ShapeDtypeStruct((t, D_MODEL), F32), jax.ShapeDtypeStruct((ts, D_MODEL), F32)),
        grid=(t // tm, D_FF // FFN_TF),
        in_specs=[pl.BlockSpec((tm, D_MODEL), lambda i, f: (i, 0)),
                  pl.BlockSpec((tm, D_MODEL), lambda i, f: (i, 0), pipeline_mode=pl.Buffered(1)),
                  small(), small(),
                  pl.BlockSpec((D_MODEL, FFN_TF), lambda i, f: (0, f)),
                  pl.BlockSpec((D_MODEL, FFN_TF), lambda i, f: (0, f)),
                  pl.BlockSpec((FFN_TF, D_MODEL), lambda i, f: (f, 0))],
        out_specs=(pl.BlockSpec((tm, D_MODEL), lambda i, f: (i, 0)), small()),
        scratch_shapes=[pltpu.VMEM((D_MODEL, FFN_TF), BF16), pltpu.VMEM((D_MODEL, FFN_TF), BF16),
                        pltpu.VMEM((FFN_TF, D_MODEL), BF16)],
        compiler_params=_params(("arbitrary", "arbitrary")),
        name="ffn",
    )(x1, xn, x1s, xns, w_g, w_u, w_d)


def _same_group_mask():
    chan_group = np.arange(LANES)[:, None] // SSM_GROUP
    state_group = np.arange(BLOCK_STATE)[None, :] // SSM_STATE
    return chan_group == state_group


def _block_diag_lanes(p):
    p4 = p.reshape(N_BLOCKS, GROUPS_PER_BLOCK, SSM_STATE, SSM_GROUP)
    rows = p4.transpose(0, 3, 1, 2).reshape(N_BLOCKS, SSM_GROUP, BLOCK_STATE)
    tiled = jnp.tile(rows, (1, GROUPS_PER_BLOCK, 1))
    return jnp.where(jnp.asarray(_same_group_mask()), tiled, 0.0)


def _block_diag_sublanes(p):
    p4 = p.reshape(N_BLOCKS, GROUPS_PER_BLOCK, SSM_GROUP, SSM_STATE)
    cols = p4.transpose(0, 1, 3, 2).reshape(N_BLOCKS, BLOCK_STATE, SSM_GROUP)
    tiled = jnp.tile(cols, (1, 1, GROUPS_PER_BLOCK))
    return jnp.where(jnp.asarray(_same_group_mask().T), tiled, 0.0)


def kernel(x_prompt, x_sample, cache_k, cache_v, cache_logf, state_ssm_re, state_ssm_im, page_table,
           norm_mix_g, w_in, b_f, q_norm_g, k_norm_g, ssm_a_re, ssm_a_im, ssm_log_dt, ssm_b_re, ssm_b_im,
           ssm_c_re, ssm_c_im, ssm_d, w_glu, b_glu, w_br_ssm, w_br_att, w_out, norm_ffn_g,
           w_ffn_gate, w_ffn_up, w_ffn_down):
    batch, seq, _ = x_prompt.shape
    n_seq, n_tok, _ = x_sample.shape
    l = 0
    n_qkv = D_SSM + 3 * D_ATT

    w_in_l = w_in[l]
    n_slabs = w_in_l.shape[1] // PROJ_COLS
    n_tail = w_in_l.shape[1] - n_slabs * PROJ_COLS
    assert n_qkv == 2 * PROJ_COLS and n_tail == GATE_SHIFT == N_HEADS
    w_tail = jnp.pad(w_in_l[:, n_slabs * PROJ_COLS:], ((0, 0), (0, LANES - n_tail))).astype(BF16)
    b_f_pad = jnp.pad(b_f[l], (0, LANES - N_HEADS)).reshape(1, LANES)
    norm_g = norm_mix_g[l].reshape(1, D_MODEL)
    q_g = q_norm_g[l].reshape(1, HEAD_DIM)
    k_g = k_norm_g[l].reshape(1, HEAD_DIM)
    ns = BLOCK_STATE
    ar_row = ssm_a_re[l].reshape(N_BLOCKS, 1, ns)
    ai_row = ssm_a_im[l].reshape(N_BLOCKS, 1, ns)
    dt_row = jnp.repeat(ssm_log_dt[l], SSM_STATE).reshape(N_BLOCKS, 1, ns)
    bd_bre = _block_diag_lanes(ssm_b_re[l])
    bd_bim = _block_diag_lanes(ssm_b_im[l])
    bd_cre = _block_diag_sublanes(ssm_c_re[l])
    bd_cim = _block_diag_sublanes(ssm_c_im[l])
    d_row = ssm_d[l].reshape(1, D_SSM)
    w_glu_b = w_glu[l].astype(BF16)
    b_glu_r = b_glu[l].reshape(1, D_SSM)
    shift_pad = (GATE_SHIFT, MERGE_COLS - D_MODEL - GATE_SHIFT)
    w_bs = jnp.pad(w_br_ssm[l].astype(BF16), ((0, 0), shift_pad))
    w_ba = jnp.pad(w_br_att[l].astype(BF16), ((0, 0), shift_pad))
    w_o = jnp.pad(w_out[l].astype(BF16), (shift_pad, (0, 0)))
    n_g = norm_ffn_g[l].reshape(1, D_MODEL)
    w_g = w_ffn_gate[l]
    w_u = w_ffn_up[l]
    w_d = w_ffn_down[l]

    vrev, bsum, m_op, bmat, cmat, a1, a16 = _s5gen(ar_row, ai_row, dt_row, bd_bre, bd_bim, bd_cre, bd_cim)

    xp = x_prompt.reshape(batch * seq, D_MODEL)
    rows = n_seq * n_tok
    xs = x_sample.transpose(1, 0, 2).reshape(rows, D_MODEL)
    ((u, qb, k, kb, v, vt, gates, logf),
     (us, qbs, ks, kbs, vs, vbs, gates_s, logf_s)) = _inproj(xp, xs, norm_g, w_in, l, w_tail, b_f_pad, q_g, k_g, tm=512)
    caq, cak = _cumaug(logf, batch, seq)
    y_ssm, hre, him = _s5_prompt(u, vrev, bsum, m_op, a16, d_row, batch, seq)
    y_att = _attention(qb, caq, kb, cak, vt, batch, seq)
    x1, xn1 = _merge(y_ssm, y_att, gates, xp, w_glu_b, b_glu_r, w_bs, w_ba, w_o, n_g, tm=256)

    ys_ssm, hre_s, him_s = _s5_sample(us, state_ssm_re[l].reshape(n_seq, -1), state_ssm_im[l].reshape(n_seq, -1),
                                      bmat, cmat, a1, d_row, n_seq, n_tok)
    by_seq = lambda a: a.reshape(n_tok, n_seq, -1).transpose(1, 0, 2)
    th = n_tok * N_HEADS
    th_rows = lambda a: by_seq(a).reshape(n_seq, th, HEAD_DIM)
    pad_keys = lambda a: jnp.pad(th_rows(a), ((0, 0), (0, LANES - th), (0, 0)))
    lf_s = by_seq(logf_s[:, :N_HEADS]).reshape(n_seq, th)
    lfrow = jnp.pad(lf_s, ((0, 0), (0, LANES - th))).reshape(n_seq, 1, LANES)
    lfcol = lf_s.reshape(n_seq, th, 1)
    n_pool = cache_k.shape[1]
    ya_s = _decode_attention(page_table, th_rows(qbs), lfrow, lfcol, pad_keys(kbs), pad_keys(vbs),
                             cache_k, cache_v, cache_logf[l].reshape(n_pool, 1, PAGE_KEYS), n_tok)
    ya_s = ya_s.reshape(n_seq, n_tok, D_ATT).transpose(1, 0, 2).reshape(rows, D_ATT).astype(BF16)
    x1s, xn1s = _merge(ys_ssm, ya_s, gates_s, xs, w_glu_b, b_glu_r, w_bs, w_ba, w_o, n_g, tm=rows)
    y_p, y_s = _ffn(x1, xn1, x1s, xn1s, w_g, w_u, w_d, tm=1024)
    y_p = y_p.reshape(batch, seq, D_MODEL)
    y_s = y_s.reshape(n_tok, n_seq, D_MODEL).transpose(1, 0, 2)

    heads = lambda a, b_, t_: a.reshape(1, b_, t_, N_HEADS, HEAD_DIM)
    tok_major = lambda a: a.reshape(n_tok, n_seq, -1).transpose(1, 0, 2)
    return (
        y_p, y_s,
        heads(k, batch, seq), heads(v, batch, seq),
        logf[:, :N_HEADS].reshape(1, batch, seq, N_HEADS),
        hre.reshape(1, batch, N_GROUPS, SSM_STATE), him.reshape(1, batch, N_GROUPS, SSM_STATE),
        heads(tok_major(ks), n_seq, n_tok), heads(tok_major(vs), n_seq, n_tok),
        tok_major(logf_s[:, :N_HEADS]).reshape(1, n_seq, n_tok, N_HEADS),
        hre_s.reshape(1, n_seq, N_GROUPS, SSM_STATE), him_s.reshape(1, n_seq, N_GROUPS, SSM_STATE),
    )
```

```python
import functools
import math

import jax
import jax.numpy as jnp
import numpy as np
from jax import lax
from jax.experimental import pallas as pl
from jax.experimental.pallas import tpu as pltpu

F32 = jnp.float32
BF16 = jnp.bfloat16
HIGHEST = lax.Precision.HIGHEST

D_MODEL = 2048
D_SSM = D_MODEL // 2
SSM_GROUP = 16
N_GROUPS = D_SSM // SSM_GROUP
SSM_STATE = 64
HEAD_DIM = 128
N_HEADS = (D_MODEL // 2) // HEAD_DIM
D_ATT = N_HEADS * HEAD_DIM
D_FF = ((8 * D_MODEL + 3 * 256 - 1) // (3 * 256)) * 256
PAGE_SIZE = 128
RMS_EPS = 1e-6
NEG_INF = -1e30

LANES = 128
GROUPS_PER_BLOCK = LANES // SSM_GROUP
N_BLOCKS = N_GROUPS // GROUPS_PER_BLOCK
BLOCK_STATE = GROUPS_PER_BLOCK * SSM_STATE
CHUNK = 16
PROJ_COLS = 2048
GATE_SHIFT = N_HEADS
ATT_TK = 512
ATT_TQ = 2 * ATT_TK
VMEM_LIMIT = 56 * 1024 * 1024

LOG2E = math.log2(math.e)
QK_SCALE_LOG2 = (HEAD_DIM ** -0.5) * LOG2E

_NT = (((1,), (1,)), ((), ()))


def _params(sem):
    return pltpu.CompilerParams(dimension_semantics=sem, vmem_limit_bytes=VMEM_LIMIT)


def _gelu_exact(x):
    return 0.5 * x * (1.0 + lax.erf(x * math.sqrt(0.5)))


def _sigmoid(x):
    return 1.0 / (1.0 + jnp.exp(-x))


def _fold_lanes(x, op):
    acc = x[:, 0:LANES]
    for c in range(1, x.shape[1] // LANES):
        acc = op(acc, x[:, c * LANES:(c + 1) * LANES])
    return acc


def _row_max(x):
    return jnp.max(_fold_lanes(x, jnp.maximum), axis=-1, keepdims=True)


def _row_sum(x):
    return jnp.sum(_fold_lanes(x, jnp.add), axis=-1, keepdims=True)


def _log_sigmoid(x):
    return jnp.minimum(x, 0.0) - jnp.log1p(jnp.exp(-jnp.abs(x)))


def _head_norm(z, gain):
    outs = []
    for h in range(N_HEADS):
        blk = z[:, h * HEAD_DIM:(h + 1) * HEAD_DIM]
        ms = jnp.mean(blk * blk, axis=-1, keepdims=True)
        outs.append(blk * lax.rsqrt(ms + RMS_EPS) * gain)
    return jnp.concatenate(outs, axis=1)


def _proj_kernel(*refs, kind, nt, v_transposed):
    za, zb, zs = refs[-3:]
    n_out = _PROJ_OUTPUTS[kind]
    in_p, in_s, w_ref = refs[0], refs[1], refs[2]
    extra = refs[3:len(refs) - 3 - 2 * n_out]
    out_p = refs[len(refs) - 3 - 2 * n_out:len(refs) - 3 - n_out]
    out_s = refs[len(refs) - 3 - n_out:len(refs) - 3]
    s = pl.program_id(0)
    half = PROJ_COLS // 2

    def group(in_ref, outs, transposed):
        if kind == "uq":
            g_ref, qg_ref = extra
            xn_ref, u_ref, qb_ref = outs

            def lhs():
                x = in_ref[...]
                ms = jnp.mean(x * x, axis=-1, keepdims=True)
                xn = (x * lax.rsqrt(ms + RMS_EPS) * g_ref[...]).astype(BF16)
                xn_ref[...] = xn
                return xn

            def epilogue(z):
                u_ref[...] = z[:, 0:half]
                qb_ref[...] = (_head_norm(z[:, half:], qg_ref[...]) * QK_SCALE_LOG2).astype(BF16)
        elif kind == "kv":
            (kg_ref,) = extra
            k_ref, kb_ref, v_ref, vb_ref = outs
            lhs = lambda: in_ref[...]

            def epilogue(z):
                kn = _head_norm(z[:, 0:half], kg_ref[...])
                k_ref[...] = kn
                kb_ref[...] = kn.astype(BF16)
                v = z[:, half:]
                v_ref[...] = v
                if transposed:
                    vt = v.T
                    for c in range(vb_ref.shape[0]):
                        vb_ref[c] = vt[:, c * ATT_TK:(c + 1) * ATT_TK].astype(BF16)
                else:
                    vb_ref[...] = v.astype(BF16)
        elif kind == "gate_ssm":
            (bf_ref,) = extra
            gate_ref, logf_ref = outs
            lhs = lambda: in_ref[...]

            def epilogue(z):
                gate_ref[...] = _sigmoid(z).astype(BF16)
                logf_ref[...] = _log_sigmoid(z[:, 0:LANES] + bf_ref[...])
        else:
            (wt_ref,) = extra
            gate_ref, tail_ref = outs

            def lhs():
                xn = in_ref[...]
                tail_ref[...] = _sigmoid(jnp.dot(xn, wt_ref[...], preferred_element_type=F32)).astype(BF16)
                return xn

            def epilogue(z):
                gate_ref[...] = _sigmoid(z).astype(BF16)
        return lhs, epilogue

    lhs_p, epilogue_p = group(in_p, out_p, v_transposed)
    lhs_s, epilogue_s = group(in_s, out_s, False)

    @pl.when(s == 0)
    def _():
        zb[...] = jnp.zeros_like(zb)

    def body(z_write, z_read):
        epilogue_p(z_read[...])
        z_write[...] = jnp.dot(lhs_p(), w_ref[...], preferred_element_type=F32)

    pl.when((s < nt) & (s % 2 == 0))(lambda: body(za, zb))
    pl.when((s < nt) & (s % 2 == 1))(lambda: body(zb, za))

    @pl.when(s == nt)
    def _():
        epilogue_p((za if (nt - 1) % 2 == 0 else zb)[...])
        zs[...] = jnp.dot(lhs_s(), w_ref[...], preferred_element_type=F32)

    @pl.when(s == nt + 1)
    def _():
        epilogue_s(zs[...])


_PROJ_OUTPUTS = {"uq": 3, "kv": 4, "gate_ssm": 2, "gate_att": 2}


def _inproj(x, xs, norm_g, w_in_b, w_tail, b_f, q_g, k_g, tm):
    t, ts = x.shape[0], xs.shape[0]
    nt = t // tm
    hi = nt - 1
    cur = lambda s: (jnp.minimum(s, hi), 0)
    lag = lambda s: (jnp.clip(s - 1, 0, hi), 0)
    lag3 = lambda s: (jnp.clip(s - 1, 0, hi), 0, 0)
    const = lambda s: (0, 0)
    sds = jax.ShapeDtypeStruct

    def call(kind, slab_index, row_inputs, extra, extra_specs, widths, dtypes, lagged, vt=False):
        in_width = row_inputs[0].shape[1]
        in_specs = [pl.BlockSpec((tm, in_width), cur), pl.BlockSpec((ts, in_width), const),
                    pl.BlockSpec((D_MODEL, PROJ_COLS), lambda s: (0, slab_index), pipeline_mode=pl.Buffered(1))]
        out_shape, out_specs = [], []
        for rows_n, block_n, prompt in ((t, tm, True), (ts, ts, False)):
            for w, dt, lg in zip(widths, dtypes, lagged):
                if vt and prompt and w is None:
                    out_shape.append(sds((t // ATT_TK, D_ATT, ATT_TK), BF16))
                    out_specs.append(pl.BlockSpec((tm // ATT_TK, D_ATT, ATT_TK), lag3))
                    continue
                w = D_ATT if w is None else w
                out_shape.append(sds((rows_n, w), dt))
                index = (lag if lg else cur) if prompt else const
                out_specs.append(pl.BlockSpec((block_n, w), index))
        outs = pl.pallas_call(
            functools.partial(_proj_kernel, kind=kind, nt=nt, v_transposed=vt),
            out_shape=tuple(out_shape),
            grid=(nt + 2,),
            in_specs=in_specs + extra_specs,
            out_specs=tuple(out_specs),
            scratch_shapes=[pltpu.VMEM((tm, PROJ_COLS), F32), pltpu.VMEM((tm, PROJ_COLS), F32),
                            pltpu.VMEM((ts, PROJ_COLS), F32)],
            compiler_params=_params(("arbitrary",)),
            name="proj_" + kind,
        )(*row_inputs, w_in_b, *extra)
        n = len(widths)
        return outs[:n], outs[n:]

    gain = lambda w: pl.BlockSpec((1, w), const)
    (xn, u, qb), (xns, us, qbs) = call(
        "uq", 0, (x, xs), (norm_g, q_g), [gain(D_MODEL), gain(HEAD_DIM)],
        (D_MODEL, D_SSM, D_ATT), (BF16, F32, BF16), (False, True, True))
    (k, kb, v, vb), (ks, kbs, vs, vbs) = call(
        "kv", 1, (xn, xns), (k_g,), [gain(HEAD_DIM)],
        (D_ATT, D_ATT, D_ATT, None), (F32, BF16, F32, BF16), (True, True, True, True), vt=True)
    (gate_s, logf), (gate_ss, logfs) = call(
        "gate_ssm", 2, (xn, xns), (b_f,), [gain(LANES)],
        (PROJ_COLS, LANES), (BF16, F32), (True, True))
    (gate_a, gate_t), (gate_as, gate_ts) = call(
        "gate_att", 3, (xn, xns), (w_tail,), [pl.BlockSpec((D_MODEL, LANES), const)],
        (PROJ_COLS, LANES), (BF16, BF16), (True, False))
    return ((u, qb, k, kb, v, vb, (gate_s, gate_a, gate_t), logf),
            (us, qbs, ks, kbs, vs, vbs, (gate_ss, gate_as, gate_ts), logfs))


CUM_ROWS = 256


def _cumaug_kernel(lf_ref, eq_ref, ek_ref, caq_ref, cak_ref, carry_sc):
    @pl.when(pl.program_id(1) == 0)
    def _():
        carry_sc[...] = jnp.zeros_like(carry_sc)

    lf = lf_ref[...]
    r = lax.broadcasted_iota(jnp.int32, (CUM_ROWS, CUM_ROWS), 0)
    c = lax.broadcasted_iota(jnp.int32, (CUM_ROWS, CUM_ROWS), 1)
    tri = (c <= r).astype(BF16)
    lf_hi = lf.astype(BF16)
    lf_r = lf - lf_hi.astype(F32)
    lf_mid = lf_r.astype(BF16)
    lf_lo = (lf_r - lf_mid.astype(F32)).astype(BF16)
    parts = jnp.dot(tri, jnp.concatenate([lf_hi, lf_mid, lf_lo], axis=1), preferred_element_type=F32)
    cs = (parts[:, 0:LANES] + parts[:, LANES:2 * LANES] + parts[:, 2 * LANES:3 * LANES]) + carry_sc[...]
    carry_sc[...] = cs[CUM_ROWS - 1:CUM_ROWS, :]
    cs2 = cs * LOG2E
    hi = cs2.astype(BF16).astype(F32)
    r1 = cs2 - hi
    mid = r1.astype(BF16).astype(F32)
    lo = (r1 - mid).astype(BF16).astype(F32)
    lane = lax.broadcasted_iota(jnp.int32, (CUM_ROWS, LANES), 1)
    packed = jnp.where(lane < N_HEADS, hi,
                       jnp.where(lane < 2 * N_HEADS, pltpu.roll(mid, N_HEADS, axis=1),
                                 jnp.where(lane < 3 * N_HEADS, pltpu.roll(lo, 2 * N_HEADS, axis=1),
                                           jnp.where(lane == 3 * N_HEADS, 1.0, 0.0)))).astype(BF16)
    caq_ref[...] = jnp.dot(packed, eq_ref[...], preferred_element_type=F32).astype(BF16)
    cak_ref[...] = jnp.dot(packed, ek_ref[...], preferred_element_type=F32).astype(BF16)


def _placement_matrices():
    eq = np.zeros((LANES, D_ATT), np.float32)
    ek = np.zeros((LANES, D_ATT), np.float32)
    for h in range(N_HEADS):
        base = h * HEAD_DIM
        for piece in range(3):
            eq[piece * N_HEADS + h, base + piece] = 1.0
            ek[piece * N_HEADS + h, base + 3 + piece] = -1.0
            eq[3 * N_HEADS, base + 3 + piece] = 1.0
            ek[3 * N_HEADS, base + piece] = 1.0
    return jnp.asarray(eq, BF16), jnp.asarray(ek, BF16)


def _cumaug(logf, batch, seq):
    nb = seq // CUM_ROWS
    eq, ek = _placement_matrices()
    const = lambda: pl.BlockSpec((LANES, D_ATT), lambda b, i: (0, 0))
    return pl.pallas_call(
        _cumaug_kernel,
        out_shape=(jax.ShapeDtypeStruct((batch * seq, D_ATT), BF16),
                   jax.ShapeDtypeStruct((batch * seq, D_ATT), BF16)),
        grid=(batch, nb),
        in_specs=[pl.BlockSpec((CUM_ROWS, LANES), lambda b, i: (b * nb + i, 0)), const(), const()],
        out_specs=(pl.BlockSpec((CUM_ROWS, D_ATT), lambda b, i: (b * nb + i, 0)),
                   pl.BlockSpec((CUM_ROWS, D_ATT), lambda b, i: (b * nb + i, 0))),
        scratch_shapes=[pltpu.VMEM((1, LANES), F32)],
        compiler_params=_params(("arbitrary", "arbitrary")),
        name="cumaug",
    )(logf, eq, ek)


ACC_ROWS = HEAD_DIM + 16


def _attn_kernel(q_ref, cq_ref, k_ref, ck_ref, vt_ref, o_ref, m_sc, acc_sc, sa_sc, sb_sc):
    i = pl.program_id(2)
    tk = ATT_TK
    m_sc[...] = jnp.full(m_sc.shape, NEG_INF, F32)
    acc_sc[...] = jnp.zeros_like(acc_sc)
    sub = lax.broadcasted_iota(jnp.int32, (ACC_ROWS - HEAD_DIM, tk), 0)
    ones_rows = jnp.where(sub == 0, 1.0, 0.0).astype(BF16)

    def logits(j, s_ref, lo):
        ks = pl.multiple_of(j * tk, tk)
        kk = jnp.concatenate([k_ref[pl.ds(ks, tk), :], ck_ref[pl.ds(ks, tk), :]], axis=1)
        qq = jnp.concatenate([q_ref[lo:, :], cq_ref[lo:, :]], axis=1)
        s_ref[:, lo:] = lax.dot_general(kk, qq, _NT, preferred_element_type=F32)

    def consume(j, s_ref, lo, masked):
        s = s_ref[:, lo:]
        if masked:
            key = lax.broadcasted_iota(jnp.int32, s.shape, 0)
            qry = lax.broadcasted_iota(jnp.int32, s.shape, 1)
            s = jnp.where(key <= qry, s, NEG_INF)
        vt = jnp.concatenate([vt_ref[j], ones_rows], axis=0)
        m_old = m_sc[:, lo:]
        m_new = jnp.maximum(m_old, jnp.max(s, axis=0, keepdims=True))
        alpha = jnp.exp2(m_old - m_new)
        p = jnp.exp2(s - m_new).astype(BF16)
        acc_sc[:, lo:] = alpha * acc_sc[:, lo:] + jnp.dot(vt, p, preferred_element_type=F32)
        m_sc[:, lo:] = m_new

    logits(0, sa_sc, 0)

    def pair(jj, carry):
        logits(2 * jj + 1, sb_sc, 0)
        consume(2 * jj, sa_sc, 0, False)
        logits(2 * jj + 2, sa_sc, 0)
        consume(2 * jj + 1, sb_sc, 0, False)
        return carry

    lax.fori_loop(0, i, pair, 0)
    logits(2 * i + 1, sb_sc, tk)
    consume(2 * i, sa_sc, 0, True)
    consume(2 * i + 1, sb_sc, tk, True)

    acc = acc_sc[...]
    out_t = acc[0:HEAD_DIM, :] / acc[HEAD_DIM:HEAD_DIM + 1, :]
    o_ref[...] = out_t.T.astype(BF16)


def _attention(qb, caq, kb, cak, vt, batch, seq):
    nq = seq // ATT_TQ
    nkb = seq // ATT_TK
    qspec = lambda: pl.BlockSpec((ATT_TQ, HEAD_DIM), lambda b, h, i: (b * nq + i, h))
    kspec = lambda: pl.BlockSpec((seq, HEAD_DIM), lambda b, h, i: (b, h))
    return pl.pallas_call(
        _attn_kernel,
        out_shape=jax.ShapeDtypeStruct((batch * seq, D_ATT), BF16),
        grid=(batch, N_HEADS, nq),
        in_specs=[qspec(), qspec(), kspec(), kspec(),
                  pl.BlockSpec((nkb, HEAD_DIM, ATT_TK), lambda b, h, i: (b, h, 0))],
        out_specs=qspec(),
        scratch_shapes=[pltpu.VMEM((1, ATT_TQ), F32), pltpu.VMEM((ACC_ROWS, ATT_TQ), F32),
                        pltpu.VMEM((ATT_TK, ATT_TQ), F32), pltpu.VMEM((ATT_TK, ATT_TQ), F32)],
        compiler_params=_params(("arbitrary", "arbitrary", "arbitrary")),
        name="fox_prompt",
    )(qb, caq, kb, cak, vt)


def _discretise(a_re, a_im, log_dt):
    dt = jnp.exp(log_dt)
    mag = jnp.exp(dt * a_re)
    ang = dt * a_im
    abr = mag * jnp.cos(ang)
    abi = mag * jnp.sin(ang)
    e_re = abr - 1.0
    e_im = abi
    inv_den = 1.0 / (a_re * a_re + a_im * a_im)
    f_re = (e_re * a_re + e_im * a_im) * inv_den
    f_im = (e_im * a_re - e_re * a_im) * inv_den
    return abr, abi, f_re, f_im


def _powers(abr, abi, n):
    pr, pi = [jnp.ones_like(abr)], [jnp.zeros_like(abi)]
    for _ in range(n):
        r, i = pr[-1], pi[-1]
        pr.append(r * abr - i * abi)
        pi.append(r * abi + i * abr)
    return pr, pi


def _s5gen_kernel(ar_row, ai_row, dt_row, bre_ref, bim_ref, cre_ref, cim_ref,
                  vrev_ref, bsum_ref, m_ref, bmat_ref, cmat_ref, a1_ref, a16_ref):
    ns = BLOCK_STATE
    abr, abi, f_re, f_im = _discretise(ar_row[...], ai_row[...], dt_row[...])
    bre, bim = bre_ref[...], bim_ref[...]
    bbr = f_re * bre - f_im * bim
    bbi = f_re * bim + f_im * bre
    bmat = jnp.concatenate([bbr, bbi], axis=1)
    bmat_ref[...] = bmat
    pr, pi = _powers(abr, abi, CHUNK)
    a1_ref[...] = jnp.concatenate([pr[1], pi[1]], axis=1)
    a16_ref[...] = jnp.concatenate([pr[CHUNK], pi[CHUNK]], axis=1)
    for i in range(CHUNK):
        r, im = pr[CHUNK - 1 - i], pi[CHUNK - 1 - i]
        blk = jnp.concatenate([r * bbr - im * bbi, r * bbi + im * bbr], axis=1)
        bsum_ref[i * LANES:(i + 1) * LANES, :] = blk.astype(BF16)
    npow = CHUNK + 1
    stacked = jnp.concatenate(pr + pi + [jnp.zeros((LANES - 2 * npow, ns), F32)], axis=0)
    pt = stacked.T
    qr = [pt[:, t:t + 1] for t in range(npow)]
    qi = [pt[:, npow + t:npow + t + 1] for t in range(npow)]
    cre, cim = cre_ref[...], cim_ref[...]
    vrev_ref[(CHUNK - 1) * LANES:CHUNK * LANES, 0:LANES] = jnp.zeros((LANES, LANES), BF16)
    bmat_hi = bmat.astype(BF16)
    for tau in range(CHUNK + 1):
        blk = jnp.concatenate([cre * qr[tau] - cim * qi[tau],
                               -(cre * qi[tau] + cim * qr[tau])], axis=0)
        if tau == 0:
            cmat_ref[...] = blk
        else:
            m_ref[:, (tau - 1) * LANES:tau * LANES] = blk.astype(BF16)
        if tau < CHUNK:
            w = jnp.dot(bmat_hi, blk.astype(BF16), preferred_element_type=F32).astype(BF16)
            k = CHUNK - 1 - tau
            vrev_ref[k * LANES:(k + 1) * LANES, LANES:2 * LANES] = w
            if k >= 1:
                vrev_ref[(k - 1) * LANES:k * LANES, 0:LANES] = w


def _s5gen(ar_row, ai_row, dt_row, bd_bre, bd_bim, bd_cre, bd_cim):
    ns = BLOCK_STATE
    b3 = lambda s1, s2: pl.BlockSpec((None, s1, s2), lambda o: (o, 0, 0))
    out_shape = (
        jax.ShapeDtypeStruct((N_BLOCKS, CHUNK * LANES, 2 * LANES), BF16),
        jax.ShapeDtypeStruct((N_BLOCKS, CHUNK * LANES, 2 * ns), BF16),
        jax.ShapeDtypeStruct((N_BLOCKS, 2 * ns, CHUNK * LANES), BF16),
        jax.ShapeDtypeStruct((N_BLOCKS, LANES, 2 * ns), F32),
        jax.ShapeDtypeStruct((N_BLOCKS, 2 * ns, LANES), F32),
        jax.ShapeDtypeStruct((N_BLOCKS, 1, 2 * ns), F32),
        jax.ShapeDtypeStruct((N_BLOCKS, 1, 2 * ns), F32),
    )
    return pl.pallas_call(
        _s5gen_kernel,
        out_shape=out_shape,
        grid=(N_BLOCKS,),
        in_specs=[b3(1, ns), b3(1, ns), b3(1, ns),
                  b3(LANES, ns), b3(LANES, ns), b3(ns, LANES), b3(ns, LANES)],
        out_specs=(b3(CHUNK * LANES, 2 * LANES), b3(CHUNK * LANES, 2 * ns), b3(2 * ns, CHUNK * LANES),
                   b3(LANES, 2 * ns), b3(2 * ns, LANES), b3(1, 2 * ns), b3(1, 2 * ns)),
        compiler_params=_params(("arbitrary",)),
        name="s5_operators",
    )(ar_row, ai_row, dt_row, bd_bre, bd_bim, bd_cre, bd_cim)


def _s5_prompt_kernel(u_ref, vrev_ref, bsum_ref, m_ref, a16_ref, d_ref,
                      y_ref, hre_ref, him_ref, ucat_sc, s_sc, hin_sc, yt_sc, *, n_chunks):
    ns = BLOCK_STATE
    for i in range(CHUNK):
        ucat_sc[:, i * LANES:(i + 1) * LANES] = u_ref[pl.ds(i, n_chunks, stride=CHUNK), :].astype(BF16)
    s_sc[...] = jnp.dot(ucat_sc[...], bsum_ref[...], preferred_element_type=F32)
    ar = a16_ref[:, 0:ns]
    ai = a16_ref[:, ns:2 * ns]
    for jp in range(CHUNK // 2):
        j = 2 * jp
        kk = (j + 2) * LANES
        yt_sc[:, j * LANES:(j + 2) * LANES] = jnp.dot(
            ucat_sc[:, 0:kk], vrev_ref[(CHUNK - 2 - j) * LANES:, :], preferred_element_type=F32)

    hr = jnp.zeros((1, ns), F32)
    hi = jnp.zeros((1, ns), F32)
    for k in range(n_chunks):
        hin_sc[k:k + 1, 0:ns] = hr
        hin_sc[k:k + 1, ns:2 * ns] = hi
        sr = s_sc[k:k + 1, 0:ns]
        si = s_sc[k:k + 1, ns:2 * ns]
        hr, hi = ar * hr - ai * hi + sr, ar * hi + ai * hr + si
    hre_ref[...] = hr
    him_ref[...] = hi
    hin = hin_sc[...].astype(BF16)
    d = d_ref[...]
    for jp in range(CHUNK // 2):
        j = 2 * jp
        acc = yt_sc[:, j * LANES:(j + 2) * LANES] + jnp.dot(
            hin, m_ref[:, j * LANES:(j + 2) * LANES], preferred_element_type=F32)
        for jj in range(2):
            uj = u_ref[pl.ds(j + jj, n_chunks, stride=CHUNK), :]
            val = acc[:, jj * LANES:(jj + 1) * LANES] + d * uj
            y_ref[pl.ds(j + jj, n_chunks, stride=CHUNK), :] = _gelu_exact(val)


def _s5_prompt(u, vrev, bsum, m, a16, d, batch, seq):
    ns = BLOCK_STATE
    n_chunks = seq // CHUNK
    w3 = lambda s1, s2: pl.BlockSpec((None, s1, s2), lambda o, b: (o, 0, 0))
    return pl.pallas_call(
        functools.partial(_s5_prompt_kernel, n_chunks=n_chunks),
        out_shape=(jax.ShapeDtypeStruct((batch * seq, D_SSM), F32),
                   jax.ShapeDtypeStruct((batch, 1, N_GROUPS * SSM_STATE), F32),
                   jax.ShapeDtypeStruct((batch, 1, N_GROUPS * SSM_STATE), F32)),
        grid=(N_BLOCKS, batch),
        in_specs=[pl.BlockSpec((seq, LANES), lambda o, b: (b, o)),
                  w3(CHUNK * LANES, 2 * LANES), w3(CHUNK * LANES, 2 * ns), w3(2 * ns, CHUNK * LANES),
                  w3(1, 2 * ns),
                  pl.BlockSpec((1, LANES), lambda o, b: (0, o))],
        out_specs=(pl.BlockSpec((seq, LANES), lambda o, b: (b, o)),
                   pl.BlockSpec((None, 1, ns), lambda o, b: (b, 0, o)),
                   pl.BlockSpec((None, 1, ns), lambda o, b: (b, 0, o))),
        scratch_shapes=[pltpu.VMEM((n_chunks, CHUNK * LANES), BF16),
                        pltpu.VMEM((n_chunks, 2 * ns), F32),
                        pltpu.VMEM((n_chunks, 2 * ns), F32),
                        pltpu.VMEM((n_chunks, CHUNK * LANES), F32)],
        compiler_params=_params(("arbitrary", "arbitrary")),
        name="s5_prompt",
    )(u, vrev, bsum, m, a16, d)


def _s5_sample_kernel(u_ref, h0r_ref, h0i_ref, bmat_ref, cmat_ref, a1_ref, d_ref,
                      y_ref, hre_ref, him_ref, *, n_seq, n_tok):
    ns = BLOCK_STATE
    u = u_ref[...]
    bu = jnp.dot(u, bmat_ref[...], precision=HIGHEST, preferred_element_type=F32)
    ar = a1_ref[:, 0:ns]
    ai = a1_ref[:, ns:2 * ns]
    hr, hi = h0r_ref[...], h0i_ref[...]
    hs = []
    for t in range(n_tok):
        br = bu[t * n_seq:(t + 1) * n_seq, 0:ns]
        bi = bu[t * n_seq:(t + 1) * n_seq, ns:2 * ns]
        hr, hi = ar * hr - ai * hi + br, ar * hi + ai * hr + bi
        hs.append(jnp.concatenate([hr, hi], axis=1))
    hcat = jnp.concatenate(hs, axis=0)
    val = jnp.dot(hcat, cmat_ref[...], precision=HIGHEST, preferred_element_type=F32) + d_ref[...] * u
    y_ref[...] = _gelu_exact(val)
    hre_ref[...] = hr
    him_ref[...] = hi


def _s5_sample(u, h0r, h0i, bmat, cmat, a1, d, n_seq, n_tok):
    ns = BLOCK_STATE
    rows = n_seq * n_tok
    w3 = lambda s1, s2: pl.BlockSpec((None, s1, s2), lambda o: (o, 0, 0))
    col = lambda r, c: pl.BlockSpec((r, c), lambda o: (0, o))
    return pl.pallas_call(
        functools.partial(_s5_sample_kernel, n_seq=n_seq, n_tok=n_tok),
        out_shape=(jax.ShapeDtypeStruct((rows, D_SSM), F32),
                   jax.ShapeDtypeStruct((n_seq, N_GROUPS * SSM_STATE), F32),
                   jax.ShapeDtypeStruct((n_seq, N_GROUPS * SSM_STATE), F32)),
        grid=(N_BLOCKS,),
        in_specs=[col(rows, LANES), col(n_seq, ns), col(n_seq, ns),
                  w3(LANES, 2 * ns), w3(2 * ns, LANES), w3(1, 2 * ns), col(1, LANES)],
        out_specs=(col(rows, LANES), col(n_seq, ns), col(n_seq, ns)),
        compiler_params=_params(("arbitrary",)),
        name="s5_sample",
    )(u, h0r, h0i, bmat, cmat, a1, d)


PAGES_PER_STEP = 16
PAGE_GROUP = 4

PAGE_KEYS = PAGE_SIZE * N_HEADS


def _decode_kernel(pt_ref, q_ref, lfrow_ref, lfcol_ref, kn_ref, vn_ref, *refs, n_tok, n_steps):
    g = PAGES_PER_STEP
    k_refs = refs[0:g]
    v_refs = refs[g:2 * g]
    lf_refs = refs[2 * g:3 * g]
    o_ref = refs[3 * g]
    m_sc, l_sc, acc_sc, carry_sc = refs[3 * g + 1:]
    j = pl.program_id(1)
    rows = n_tok * N_HEADS
    head_mask = N_HEADS - 1
    head_shift = N_HEADS.bit_length() - 1

    @pl.when(j == 0)
    def _():
        m_sc[...] = jnp.full(m_sc.shape, NEG_INF, F32)
        l_sc[...] = jnp.zeros_like(l_sc)
        acc_sc[...] = jnp.zeros_like(acc_sc)
        carry_sc[...] = jnp.zeros_like(carry_sc)

    q = q_ref[...]
    lfcol = lfcol_ref[...]
    run = jnp.zeros((N_HEADS, 1), F32)
    pieces = []
    for t in range(n_tok):
        run = run + lfcol[t * N_HEADS:(t + 1) * N_HEADS, :]
        pieces.append(run)
    c_col = jnp.concatenate(pieces, axis=0)

    def online(s_blocks, v_blocks):
        m = m_sc[...]
        smax = s_blocks[0]
        for s in s_blocks[1:]:
            smax = jnp.maximum(smax, s)
        m_new = jnp.maximum(m, _row_max(smax))
        alpha = jnp.exp2(m - m_new)
        psum = None
        pv = None
        for s, vblk in zip(s_blocks, v_blocks):
            p = jnp.exp2(s - m_new)
            psum = p if psum is None else psum + p
            d = jnp.dot(p.astype(BF16), vblk, preferred_element_type=F32)
            pv = d if pv is None else pv + d
        l_sc[...] = alpha * l_sc[...] + _row_sum(psum)
        acc_sc[...] = alpha * acc_sc[...] + pv
        m_sc[...] = m_new

    col = lax.broadcasted_iota(jnp.int32, (rows, PAGE_KEYS), 1)
    row = lax.broadcasted_iota(jnp.int32, (rows, PAGE_KEYS), 0)
    own = (col & head_mask) == (row & head_mask)
    fixed = jnp.where(own, c_col * LOG2E, NEG_INF)

    lf = jnp.concatenate([lf_refs[pg][...] for pg in range(g)], axis=0)
    lane = lax.broadcasted_iota(jnp.int32, (g, PAGE_KEYS), 1)
    suffix = lf
    total = lf
    sh = N_HEADS
    while sh < PAGE_KEYS:
        suffix = suffix + jnp.where(lane + sh < PAGE_KEYS, pltpu.roll(suffix, PAGE_KEYS - sh, axis=1), 0.0)
        total = total + pltpu.roll(total, PAGE_KEYS - sh, axis=1)
        sh *= 2
    carry = carry_sc[...]
    past = []
    for pg in range(g):
        past.append((suffix[pg:pg + 1, :] - lf[pg:pg + 1, :] + carry) * LOG2E)
        carry = carry + total[pg:pg + 1, :]
    carry_sc[...] = carry

    def logits(pages):
        out = []
        for pg in pages:
            kp = k_refs[pg][...].reshape(PAGE_KEYS, HEAD_DIM).astype(BF16)
            s = lax.dot_general(q, kp, _NT, preferred_element_type=F32)
            out.append(s + fixed + past[pg])
        return out

    def values(pages):
        return [v_refs[pg][...].reshape(PAGE_KEYS, HEAD_DIM).astype(BF16) for pg in pages]

    groups = [list(range(a, a + PAGE_GROUP)) for a in range(0, g, PAGE_GROUP)]
    s_next = logits(groups[0])
    for gi, pages in enumerate(groups):
        s_cur = s_next
        if gi + 1 < len(groups):
            s_next = logits(groups[gi + 1])
        online(s_cur, values(pages))

    @pl.when(j == n_steps - 1)
    def _():
        ln = lax.broadcasted_iota(jnp.int32, (1, LANES), 1)
        c_row = lfrow_ref[...]
        sh2 = N_HEADS
        while sh2 < rows:
            c_row = c_row + jnp.where(ln >= sh2, pltpu.roll(c_row, sh2, axis=1), 0.0)
            sh2 *= 2
        cl = lax.broadcasted_iota(jnp.int32, (rows, LANES), 1)
        rw = lax.broadcasted_iota(jnp.int32, (rows, LANES), 0)
        valid = ((cl < rows) & ((cl & head_mask) == (rw & head_mask))
                 & ((cl >> head_shift) <= (rw >> head_shift)))
        s = lax.dot_general(q, kn_ref[...], _NT, preferred_element_type=F32)
        s = jnp.where(valid, s + (c_col - c_row) * LOG2E, NEG_INF)
        online([s], [vn_ref[...]])
        o_ref[...] = acc_sc[...] / l_sc[...]


def _decode_attention(page_table, q, lfrow, lfcol, kn, vn, cache_k, cache_v, cache_lf, n_tok):
    n_seq, n_pages = page_table.shape
    g = PAGES_PER_STEP
    n_steps = n_pages // g
    rows = n_tok * N_HEADS

    def page5(pg):
        return lambda b, j, pt: (0, pt[b * n_pages + (n_pages - 1 - (j * g + pg))], 0, 0, 0)

    def page3(pg):
        return lambda b, j, pt: (pt[b * n_pages + (n_pages - 1 - (j * g + pg))], 0, 0)

    seq3 = lambda s1, s2: pl.BlockSpec((None, s1, s2), lambda b, j, pt: (b, 0, 0))
    kv_block = (None, None, PAGE_SIZE, N_HEADS, HEAD_DIM)
    in_specs = [seq3(rows, HEAD_DIM), seq3(1, LANES), seq3(rows, 1), seq3(LANES, HEAD_DIM), seq3(LANES, HEAD_DIM)]
    in_specs += [pl.BlockSpec(kv_block, page5(pg)) for pg in range(g)]
    in_specs += [pl.BlockSpec(kv_block, page5(pg)) for pg in range(g)]
    in_specs += [pl.BlockSpec((None, 1, PAGE_KEYS), page3(pg)) for pg in range(g)]
    grid_spec = pltpu.PrefetchScalarGridSpec(
        num_scalar_prefetch=1,
        grid=(n_seq, n_steps),
        in_specs=in_specs,
        out_specs=pl.BlockSpec((None, rows, HEAD_DIM), lambda b, j, pt: (b, 0, 0)),
        scratch_shapes=[pltpu.VMEM((rows, 1), F32),
                        pltpu.VMEM((rows, 1), F32),
                        pltpu.VMEM((rows, HEAD_DIM), F32),
                        pltpu.VMEM((1, PAGE_KEYS), F32)],
    )
    return pl.pallas_call(
        functools.partial(_decode_kernel, n_tok=n_tok, n_steps=n_steps),
        out_shape=jax.ShapeDtypeStruct((n_seq, rows, HEAD_DIM), F32),
        grid_spec=grid_spec,
        compiler_params=_params(("arbitrary", "arbitrary")),
        name="fox_sample",
    )(page_table.reshape(-1), q, lfrow, lfcol, kn, vn,
      *([cache_k] * g), *([cache_v] * g), *([cache_lf] * g))


def _merge_kernel(y_ref, ya_ref, gs_ref, ga_ref, gt_ref, x_ref, wglu_ref, bglu_ref, wbs_ref, wba_ref, wo_ref,
                  ng_ref, x1_ref, xn_ref):
    y = y_ref[...]
    gl = jnp.dot(y.astype(BF16), wglu_ref[...], preferred_element_type=F32) + bglu_ref[...]
    ys = (y * _sigmoid(gl)).astype(BF16)
    ms = jnp.dot(ys, wbs_ref[...], preferred_element_type=F32)
    ma = jnp.dot(ya_ref[...], wba_ref[...], preferred_element_type=F32)
    ga = ga_ref[...]
    gate_s = jnp.concatenate([gs_ref[...], ga[:, 0:LANES]], axis=1).astype(F32)
    gate_a = jnp.concatenate([ga, gt_ref[...]], axis=1).astype(F32)
    merged = gate_s * ms + gate_a * ma
    x1 = x_ref[...] + jnp.dot(merged.astype(BF16), wo_ref[...], preferred_element_type=F32)
    x1_ref[...] = x1
    ms1 = jnp.mean(x1 * x1, axis=-1, keepdims=True)
    xn_ref[...] = (x1 * lax.rsqrt(ms1 + RMS_EPS) * ng_ref[...]).astype(BF16)


MERGE_COLS = D_MODEL + LANES


def _merge(y, ya, gates, x, w_glu, b_glu, w_bs, w_ba, w_o, n_g, tm):
    t = x.shape[0]
    gate_s, gate_a, gate_t = gates
    row = lambda c: pl.BlockSpec((tm, c), lambda i: (i, 0))
    const = lambda r, c: pl.BlockSpec((r, c), lambda i: (0, 0), pipeline_mode=pl.Buffered(1))
    return pl.pallas_call(
        _merge_kernel,
        out_shape=(jax.ShapeDtypeStruct((t, D_MODEL), F32), jax.ShapeDtypeStruct((t, D_MODEL), BF16)),
        grid=(t // tm,),
        in_specs=[row(D_SSM), row(D_ATT), row(PROJ_COLS), row(PROJ_COLS), row(LANES), row(D_MODEL),
                  const(D_SSM, D_SSM), const(1, D_SSM), const(D_SSM, MERGE_COLS), const(D_ATT, MERGE_COLS),
                  const(MERGE_COLS, D_MODEL), const(1, D_MODEL)],
        out_specs=(row(D_MODEL), row(D_MODEL)),
        compiler_params=_params(("arbitrary",)),
        name="merge_out",
    )(y, ya, gate_s, gate_a, gate_t, x, w_glu, b_glu, w_bs, w_ba, w_o, n_g)


FFN_TF = 256


def _ffn_kernel(x1_ref, xn_ref, x1s_ref, xns_ref, wg_ref, wu_ref, wd_ref, o_ref, os_ref, wg_sc, wu_sc, wd_sc):
    i, f = pl.program_id(0), pl.program_id(1)
    @pl.when(f == 0)
    def _():
        o_ref[...] = x1_ref[...]

    xn = xn_ref[...]
    wg_sc[...] = wg_ref[...].astype(BF16)
    a = jnp.dot(xn, wg_sc[...], preferred_element_type=F32)
    wu_sc[...] = wu_ref[...].astype(BF16)
    b = jnp.dot(xn, wu_sc[...], preferred_element_type=F32)
    wd_sc[...] = wd_ref[...].astype(BF16)
    o_ref[...] += jnp.dot((a * _sigmoid(a) * b).astype(BF16), wd_sc[...], preferred_element_type=F32)

    @pl.when(i == pl.num_programs(0) - 1)
    def _():
        @pl.when(f == 0)
        def _():
            os_ref[...] = x1s_ref[...]

        xs = xns_ref[...]
        a_s = jnp.dot(xs, wg_sc[...], preferred_element_type=F32)
        b_s = jnp.dot(xs, wu_sc[...], preferred_element_type=F32)
        os_ref[...] += jnp.dot((a_s * _sigmoid(a_s) * b_s).astype(BF16), wd_sc[...], preferred_element_type=F32)


def _ffn(x1, xn, x1s, xns, w_g, w_u, w_d, tm):
    t = x1.shape[0]
    ts = x1s.shape[0]
    small = lambda: pl.BlockSpec((ts, D_MODEL), lambda i, f: (0, 0))
    return pl.pallas_call(
        _ffn_kernel,
        out_shape=(jax.ShapeDtypeStruct((t, D_MODEL), F32), jax.ShapeDtypeStruct((ts, D_MODEL), F32)),
        grid=(t // tm, D_FF // FFN_TF),
        in_specs=[pl.BlockSpec((tm, D_MODEL), lambda i, f: (i, 0)),
                  pl.BlockSpec((tm, D_MODEL), lambda i, f: (i, 0), pipeline_mode=pl.Buffered(1)),
                  small(), small(),
                  pl.BlockSpec((D_MODEL, FFN_TF), lambda i, f: (0, f)),
                  pl.BlockSpec((D_MODEL, FFN_TF), lambda i, f: (0, f)),
                  pl.BlockSpec((FFN_TF, D_MODEL), lambda i, f: (f, 0))],
        out_specs=(pl.BlockSpec((tm, D_MODEL), lambda i, f: (i, 0)), small()),
        scratch_shapes=[pltpu.VMEM((D_MODEL, FFN_TF), BF16), pltpu.VMEM((D_MODEL, FFN_TF), BF16),
                        pltpu.VMEM((FFN_TF, D_MODEL), BF16)],
        compiler_params=_params(("arbitrary", "arbitrary")),
        name="ffn",
    )(x1, xn, x1s, xns, w_g, w_u, w_d)


def _same_group_mask():
    chan_group = np.arange(LANES)[:, None] // SSM_GROUP
    state_group = np.arange(BLOCK_STATE)[None, :] // SSM_STATE
    return chan_group == state_group


def _block_diag_lanes(p):
    p4 = p.reshape(N_BLOCKS, GROUPS_PER_BLOCK, SSM_STATE, SSM_GROUP)
    rows = p4.transpose(0, 3, 1, 2).reshape(N_BLOCKS, SSM_GROUP, BLOCK_STATE)
    tiled = jnp.tile(rows, (1, GROUPS_PER_BLOCK, 1))
    return jnp.where(jnp.asarray(_same_group_mask()), tiled, 0.0)


def _block_diag_sublanes(p):
    p4 = p.reshape(N_BLOCKS, GROUPS_PER_BLOCK, SSM_GROUP, SSM_STATE)
    cols = p4.transpose(0, 1, 3, 2).reshape(N_BLOCKS, BLOCK_STATE, SSM_GROUP)
    tiled = jnp.tile(cols, (1, 1, GROUPS_PER_BLOCK))
    return jnp.where(jnp.asarray(_same_group_mask().T), tiled, 0.0)


def kernel(x_prompt, x_sample, cache_k, cache_v, cache_logf, state_ssm_re, state_ssm_im, page_table,
           norm_mix_g, w_in, b_f, q_norm_g, k_norm_g, ssm_a_re, ssm_a_im, ssm_log_dt, ssm_b_re, ssm_b_im,
           ssm_c_re, ssm_c_im, ssm_d, w_glu, b_glu, w_br_ssm, w_br_att, w_out, norm_ffn_g,
           w_ffn_gate, w_ffn_up, w_ffn_down):
    batch, seq, _ = x_prompt.shape
    n_seq, n_tok, _ = x_sample.shape
    l = 0
    n_qkv = D_SSM + 3 * D_ATT

    w_in_b = w_in[l].astype(BF16)
    n_slabs = w_in_b.shape[1] // PROJ_COLS
    n_tail = w_in_b.shape[1] - n_slabs * PROJ_COLS
    assert n_qkv == 2 * PROJ_COLS and n_tail == GATE_SHIFT == N_HEADS
    w_tail = jnp.pad(w_in_b[:, n_slabs * PROJ_COLS:], ((0, 0), (0, LANES - n_tail)))
    b_f_pad = jnp.pad(b_f[l], (0, LANES - N_HEADS)).reshape(1, LANES)
    norm_g = norm_mix_g[l].reshape(1, D_MODEL)
    q_g = q_norm_g[l].reshape(1, HEAD_DIM)
    k_g = k_norm_g[l].reshape(1, HEAD_DIM)
    ns = BLOCK_STATE
    ar_row = ssm_a_re[l].reshape(N_BLOCKS, 1, ns)
    ai_row = ssm_a_im[l].reshape(N_BLOCKS, 1, ns)
    dt_row = jnp.repeat(ssm_log_dt[l], SSM_STATE).reshape(N_BLOCKS, 1, ns)
    bd_bre = _block_diag_lanes(ssm_b_re[l])
    bd_bim = _block_diag_lanes(ssm_b_im[l])
    bd_cre = _block_diag_sublanes(ssm_c_re[l])
    bd_cim = _block_diag_sublanes(ssm_c_im[l])
    d_row = ssm_d[l].reshape(1, D_SSM)
    w_glu_b = w_glu[l].astype(BF16)
    b_glu_r = b_glu[l].reshape(1, D_SSM)
    shift_pad = (GATE_SHIFT, MERGE_COLS - D_MODEL - GATE_SHIFT)
    w_bs = jnp.pad(w_br_ssm[l].astype(BF16), ((0, 0), shift_pad))
    w_ba = jnp.pad(w_br_att[l].astype(BF16), ((0, 0), shift_pad))
    w_o = jnp.pad(w_out[l].astype(BF16), (shift_pad, (0, 0)))
    n_g = norm_ffn_g[l].reshape(1, D_MODEL)
    w_g = w_ffn_gate[l]
    w_u = w_ffn_up[l]
    w_d = w_ffn_down[l]

    vrev, bsum, m_op, bmat, cmat, a1, a16 = _s5gen(ar_row, ai_row, dt_row, bd_bre, bd_bim, bd_cre, bd_cim)

    xp = x_prompt.reshape(batch * seq, D_MODEL)
    rows = n_seq * n_tok
    xs = x_sample.transpose(1, 0, 2).reshape(rows, D_MODEL)
    ((u, qb, k, kb, v, vt, gates, logf),
     (us, qbs, ks, kbs, vs, vbs, gates_s, logf_s)) = _inproj(xp, xs, norm_g, w_in_b, w_tail, b_f_pad, q_g, k_g, tm=512)
    caq, cak = _cumaug(logf, batch, seq)
    y_ssm, hre, him = _s5_prompt(u, vrev, bsum, m_op, a16, d_row, batch, seq)
    y_att = _attention(qb, caq, kb, cak, vt, batch, seq)
    x1, xn1 = _merge(y_ssm, y_att, gates, xp, w_glu_b, b_glu_r, w_bs, w_ba, w_o, n_g, tm=256)

    ys_ssm, hre_s, him_s = _s5_sample(us, state_ssm_re[l].reshape(n_seq, -1), state_ssm_im[l].reshape(n_seq, -1),
                                      bmat, cmat, a1, d_row, n_seq, n_tok)
    by_seq = lambda a: a.reshape(n_tok, n_seq, -1).transpose(1, 0, 2)
    th = n_tok * N_HEADS
    th_rows = lambda a: by_seq(a).reshape(n_seq, th, HEAD_DIM)
    pad_keys = lambda a: jnp.pad(th_rows(a), ((0, 0), (0, LANES - th), (0, 0)))
    lf_s = by_seq(logf_s[:, :N_HEADS]).reshape(n_seq, th)
    lfrow = jnp.pad(lf_s, ((0, 0), (0, LANES - th))).reshape(n_seq, 1, LANES)
    lfcol = lf_s.reshape(n_seq, th, 1)
    n_pool = cache_k.shape[1]
    ya_s = _decode_attention(page_table, th_rows(qbs), lfrow, lfcol, pad_keys(kbs), pad_keys(vbs),
                             cache_k, cache_v, cache_logf[l].reshape(n_pool, 1, PAGE_KEYS), n_tok)
    ya_s = ya_s.reshape(n_seq, n_tok, D_ATT).transpose(1, 0, 2).reshape(rows, D_ATT).astype(BF16)
    x1s, xn1s = _merge(ys_ssm, ya_s, gates_s, xs, w_glu_b, b_glu_r, w_bs, w_ba, w_o, n_g, tm=rows)
    y_p, y_s = _ffn(x1, xn1, x1s, xn1s, w_g, w_u, w_d, tm=1024)
    y_p = y_p.reshape(batch, seq, D_MODEL)
    y_s = y_s.reshape(n_tok, n_seq, D_MODEL).transpose(1, 0, 2)

    heads = lambda a, b_, t_: a.reshape(1, b_, t_, N_HEADS, HEAD_DIM)
    tok_major = lambda a: a.reshape(n_tok, n_seq, -1).transpose(1, 0, 2)
    return (
        y_p, y_s,
        heads(k, batch, seq), heads(v, batch, seq),
        logf[:, :N_HEADS].reshape(1, batch, seq, N_HEADS),
        hre.reshape(1, batch, N_GROUPS, SSM_STATE), him.reshape(1, batch, N_GROUPS, SSM_STATE),
        heads(tok_major(ks), n_seq, n_tok), heads(tok_major(vs), n_seq, n_tok),
        tok_major(logf_s[:, :N_HEADS]).reshape(1, n_seq, n_tok, N_HEADS),
        hre_s.reshape(1, n_seq, N_GROUPS, SSM_STATE), him_s.reshape(1, n_seq, N_GROUPS, SSM_STATE),
    )
```

```python
import functools
import math

import jax
import jax.numpy as jnp
import numpy as np
from jax import lax
from jax.experimental import pallas as pl
from jax.experimental.pallas import tpu as pltpu

F32 = jnp.float32
BF16 = jnp.bfloat16
HIGHEST = lax.Precision.HIGHEST

D_MODEL = 2048
D_SSM = D_MODEL // 2
SSM_GROUP = 16
N_GROUPS = D_SSM // SSM_GROUP
SSM_STATE = 64
HEAD_DIM = 128
N_HEADS = (D_MODEL // 2) // HEAD_DIM
D_ATT = N_HEADS * HEAD_DIM
D_FF = ((8 * D_MODEL + 3 * 256 - 1) // (3 * 256)) * 256
PAGE_SIZE = 128
RMS_EPS = 1e-6
NEG_INF = -1e30

LANES = 128
GROUPS_PER_BLOCK = LANES // SSM_GROUP
N_BLOCKS = N_GROUPS // GROUPS_PER_BLOCK
BLOCK_STATE = GROUPS_PER_BLOCK * SSM_STATE
CHUNK = 16
PROJ_COLS = 2048
GATE_SHIFT = N_HEADS
ATT_TK = 512
ATT_TQ = 2 * ATT_TK
VMEM_LIMIT = 56 * 1024 * 1024

LOG2E = math.log2(math.e)
QK_SCALE_LOG2 = (HEAD_DIM ** -0.5) * LOG2E

_NT = (((1,), (1,)), ((), ()))


def _params(sem):
    return pltpu.CompilerParams(dimension_semantics=sem, vmem_limit_bytes=VMEM_LIMIT)


def _gelu_exact(x):
    return 0.5 * x * (1.0 + lax.erf(x * math.sqrt(0.5)))


def _sigmoid(x):
    return 1.0 / (1.0 + jnp.exp(-x))


def _fold_lanes(x, op):
    acc = x[:, 0:LANES]
    for c in range(1, x.shape[1] // LANES):
        acc = op(acc, x[:, c * LANES:(c + 1) * LANES])
    return acc


def _row_max(x):
    return jnp.max(_fold_lanes(x, jnp.maximum), axis=-1, keepdims=True)


def _row_sum(x):
    return jnp.sum(_fold_lanes(x, jnp.add), axis=-1, keepdims=True)


def _log_sigmoid(x):
    return jnp.minimum(x, 0.0) - jnp.log1p(jnp.exp(-jnp.abs(x)))


def _head_norm(z, gain):
    outs = []
    for h in range(N_HEADS):
        blk = z[:, h * HEAD_DIM:(h + 1) * HEAD_DIM]
        ms = jnp.mean(blk * blk, axis=-1, keepdims=True)
        outs.append(blk * lax.rsqrt(ms + RMS_EPS) * gain)
    return jnp.concatenate(outs, axis=1)


def _proj_kernel(*refs, kind, nt, v_transposed):
    za, zb, zs = refs[-3:]
    n_out = _PROJ_OUTPUTS[kind]
    in_p, in_s, w_ref = refs[0], refs[1], refs[2]
    extra = refs[3:len(refs) - 3 - 2 * n_out]
    out_p = refs[len(refs) - 3 - 2 * n_out:len(refs) - 3 - n_out]
    out_s = refs[len(refs) - 3 - n_out:len(refs) - 3]
    s = pl.program_id(0)
    half = PROJ_COLS // 2

    def group(in_ref, outs, transposed):
        if kind == "uq":
            g_ref, qg_ref = extra
            xn_ref, u_ref, qb_ref = outs

            def lhs():
                x = in_ref[...]
                ms = jnp.mean(x * x, axis=-1, keepdims=True)
                xn = (x * lax.rsqrt(ms + RMS_EPS) * g_ref[...]).astype(BF16)
                xn_ref[...] = xn
                return xn

            def epilogue(z):
                u_ref[...] = z[:, 0:half]
                qb_ref[...] = (_head_norm(z[:, half:], qg_ref[...]) * QK_SCALE_LOG2).astype(BF16)
        elif kind == "kv":
            (kg_ref,) = extra
            k_ref, kb_ref, v_ref, vb_ref = outs
            lhs = lambda: in_ref[...]

            def epilogue(z):
                kn = _head_norm(z[:, 0:half], kg_ref[...])
                k_ref[...] = kn
                kb_ref[...] = kn.astype(BF16)
                v = z[:, half:]
                v_ref[...] = v
                if transposed:
                    vt = v.T
                    for c in range(vb_ref.shape[0]):
                        vb_ref[c] = vt[:, c * ATT_TK:(c + 1) * ATT_TK].astype(BF16)
                else:
                    vb_ref[...] = v.astype(BF16)
        elif kind == "gate_ssm":
            (bf_ref,) = extra
            gate_ref, logf_ref = outs
            lhs = lambda: in_ref[...]

            def epilogue(z, c0=0):
                gate_ref[:, c0:c0 + z.shape[1]] = _sigmoid(z).astype(BF16)
                if c0 == 0:
                    logf_ref[...] = _log_sigmoid(z[:, 0:LANES] + bf_ref[...])
        else:
            (wt_ref,) = extra
            gate_ref, tail_ref = outs

            def lhs():
                xn = in_ref[...]
                tail_ref[...] = _sigmoid(jnp.dot(xn, wt_ref[...], preferred_element_type=F32)).astype(BF16)
                return xn

            def epilogue(z, c0=0):
                gate_ref[:, c0:c0 + z.shape[1]] = _sigmoid(z).astype(BF16)
        return lhs, epilogue

    lhs_p, epilogue_p = group(in_p, out_p, v_transposed)
    lhs_s, epilogue_s = group(in_s, out_s, False)

    @pl.when(s == 0)
    def _():
        zb[...] = jnp.zeros_like(zb)

    def body(z_write, z_read):
        if kind in ("gate_ssm", "gate_att"):
            xn = lhs_p()
            for c0 in (0, half):
                epilogue_p(z_read[:, c0:c0 + half], c0)
                z_write[:, c0:c0 + half] = jnp.dot(xn, w_ref[:, c0:c0 + half], preferred_element_type=F32)
        else:
            epilogue_p(z_read[...])
            z_write[...] = jnp.dot(lhs_p(), w_ref[...], preferred_element_type=F32)

    pl.when((s < nt) & (s % 2 == 0))(lambda: body(za, zb))
    pl.when((s < nt) & (s % 2 == 1))(lambda: body(zb, za))

    @pl.when(s == nt)
    def _():
        epilogue_p((za if (nt - 1) % 2 == 0 else zb)[...])
        zs[...] = jnp.dot(lhs_s(), w_ref[...], preferred_element_type=F32)

    @pl.when(s == nt + 1)
    def _():
        epilogue_s(zs[...])


_PROJ_OUTPUTS = {"uq": 3, "kv": 4, "gate_ssm": 2, "gate_att": 2}


def _inproj(x, xs, norm_g, w_in_b, w_tail, b_f, q_g, k_g, tm):
    t, ts = x.shape[0], xs.shape[0]
    nt = t // tm
    hi = nt - 1
    cur = lambda s: (jnp.minimum(s, hi), 0)
    lag = lambda s: (jnp.clip(s - 1, 0, hi), 0)
    lag3 = lambda s: (jnp.clip(s - 1, 0, hi), 0, 0)
    const = lambda s: (0, 0)
    sds = jax.ShapeDtypeStruct

    def call(kind, slab_index, row_inputs, extra, extra_specs, widths, dtypes, lagged, vt=False):
        in_width = row_inputs[0].shape[1]
        in_specs = [pl.BlockSpec((tm, in_width), cur), pl.BlockSpec((ts, in_width), const),
                    pl.BlockSpec((D_MODEL, PROJ_COLS), lambda s: (0, slab_index), pipeline_mode=pl.Buffered(1))]
        out_shape, out_specs = [], []
        for rows_n, block_n, prompt in ((t, tm, True), (ts, ts, False)):
            for w, dt, lg in zip(widths, dtypes, lagged):
                if vt and prompt and w is None:
                    out_shape.append(sds((t // ATT_TK, D_ATT, ATT_TK), BF16))
                    out_specs.append(pl.BlockSpec((tm // ATT_TK, D_ATT, ATT_TK), lag3))
                    continue
                w = D_ATT if w is None else w
                out_shape.append(sds((rows_n, w), dt))
                index = (lag if lg else cur) if prompt else const
                out_specs.append(pl.BlockSpec((block_n, w), index))
        outs = pl.pallas_call(
            functools.partial(_proj_kernel, kind=kind, nt=nt, v_transposed=vt),
            out_shape=tuple(out_shape),
            grid=(nt + 2,),
            in_specs=in_specs + extra_specs,
            out_specs=tuple(out_specs),
            scratch_shapes=[pltpu.VMEM((tm, PROJ_COLS), F32), pltpu.VMEM((tm, PROJ_COLS), F32),
                            pltpu.VMEM((ts, PROJ_COLS), F32)],
            compiler_params=_params(("arbitrary",)),
            name="proj_" + kind,
        )(*row_inputs, w_in_b, *extra)
        n = len(widths)
        return outs[:n], outs[n:]

    gain = lambda w: pl.BlockSpec((1, w), const)
    (xn, u, qb), (xns, us, qbs) = call(
        "uq", 0, (x, xs), (norm_g, q_g), [gain(D_MODEL), gain(HEAD_DIM)],
        (D_MODEL, D_SSM, D_ATT), (BF16, F32, BF16), (False, True, True))
    (k, kb, v, vb), (ks, kbs, vs, vbs) = call(
        "kv", 1, (xn, xns), (k_g,), [gain(HEAD_DIM)],
        (D_ATT, D_ATT, D_ATT, None), (F32, BF16, F32, BF16), (True, True, True, True), vt=True)
    (gate_s, logf), (gate_ss, logfs) = call(
        "gate_ssm", 2, (xn, xns), (b_f,), [gain(LANES)],
        (PROJ_COLS, LANES), (BF16, F32), (True, True))
    (gate_a, gate_t), (gate_as, gate_ts) = call(
        "gate_att", 3, (xn, xns), (w_tail,), [pl.BlockSpec((D_MODEL, LANES), const)],
        (PROJ_COLS, LANES), (BF16, BF16), (True, False))
    return ((u, qb, k, kb, v, vb, (gate_s, gate_a, gate_t), logf),
            (us, qbs, ks, kbs, vs, vbs, (gate_ss, gate_as, gate_ts), logfs))


CUM_ROWS = 256


def _cumaug_kernel(lf_ref, eq_ref, ek_ref, caq_ref, cak_ref, carry_sc):
    @pl.when(pl.program_id(1) == 0)
    def _():
        carry_sc[...] = jnp.zeros_like(carry_sc)

    lf = lf_ref[...]
    r = lax.broadcasted_iota(jnp.int32, (CUM_ROWS, CUM_ROWS), 0)
    c = lax.broadcasted_iota(jnp.int32, (CUM_ROWS, CUM_ROWS), 1)
    tri = (c <= r).astype(BF16)
    lf_hi = lf.astype(BF16)
    lf_r = lf - lf_hi.astype(F32)
    lf_mid = lf_r.astype(BF16)
    lf_lo = (lf_r - lf_mid.astype(F32)).astype(BF16)
    parts = jnp.dot(tri, jnp.concatenate([lf_hi, lf_mid, lf_lo], axis=1), preferred_element_type=F32)
    cs = (parts[:, 0:LANES] + parts[:, LANES:2 * LANES] + parts[:, 2 * LANES:3 * LANES]) + carry_sc[...]
    carry_sc[...] = cs[CUM_ROWS - 1:CUM_ROWS, :]
    cs2 = cs * LOG2E
    hi = cs2.astype(BF16).astype(F32)
    r1 = cs2 - hi
    mid = r1.astype(BF16).astype(F32)
    lo = (r1 - mid).astype(BF16).astype(F32)
    lane = lax.broadcasted_iota(jnp.int32, (CUM_ROWS, LANES), 1)
    packed = jnp.where(lane < N_HEADS, hi,
                       jnp.where(lane < 2 * N_HEADS, pltpu.roll(mid, N_HEADS, axis=1),
                                 jnp.where(lane < 3 * N_HEADS, pltpu.roll(lo, 2 * N_HEADS, axis=1),
                                           jnp.where(lane == 3 * N_HEADS, 1.0, 0.0)))).astype(BF16)
    caq_ref[...] = jnp.dot(packed, eq_ref[...], preferred_element_type=F32).astype(BF16)
    cak_ref[...] = jnp.dot(packed, ek_ref[...], preferred_element_type=F32).astype(BF16)


def _placement_matrices():
    eq = np.zeros((LANES, D_ATT), np.float32)
    ek = np.zeros((LANES, D_ATT), np.float32)
    for h in range(N_HEADS):
        base = h * HEAD_DIM
        for piece in range(3):
            eq[piece * N_HEADS + h, base + piece] = 1.0
            ek[piece * N_HEADS + h, base + 3 + piece] = -1.0
            eq[3 * N_HEADS, base + 3 + piece] = 1.0
            ek[3 * N_HEADS, base + piece] = 1.0
    return jnp.asarray(eq, BF16), jnp.asarray(ek, BF16)


def _cumaug(logf, batch, seq):
    nb = seq // CUM_ROWS
    eq, ek = _placement_matrices()
    const = lambda: pl.BlockSpec((LANES, D_ATT), lambda b, i: (0, 0))
    return pl.pallas_call(
        _cumaug_kernel,
        out_shape=(jax.ShapeDtypeStruct((batch * seq, D_ATT), BF16),
                   jax.ShapeDtypeStruct((batch * seq, D_ATT), BF16)),
        grid=(batch, nb),
        in_specs=[pl.BlockSpec((CUM_ROWS, LANES), lambda b, i: (b * nb + i, 0)), const(), const()],
        out_specs=(pl.BlockSpec((CUM_ROWS, D_ATT), lambda b, i: (b * nb + i, 0)),
                   pl.BlockSpec((CUM_ROWS, D_ATT), lambda b, i: (b * nb + i, 0))),
        scratch_shapes=[pltpu.VMEM((1, LANES), F32)],
        compiler_params=_params(("arbitrary", "arbitrary")),
        name="cumaug",
    )(logf, eq, ek)


ACC_ROWS = HEAD_DIM + 16


ATT_HEADS = 2


def _attn_kernel(q_ref, cq_ref, k_ref, ck_ref, vt_ref, o_ref, *scratch):
    i = pl.program_id(2)
    tk = ATT_TK
    heads = range(ATT_HEADS)
    m_sc, acc_sc, sa_sc, sb_sc = (scratch[0::4], scratch[1::4], scratch[2::4], scratch[3::4])
    for hd in heads:
        m_sc[hd][...] = jnp.full(m_sc[hd].shape, NEG_INF, F32)
        acc_sc[hd][...] = jnp.zeros_like(acc_sc[hd])
    sub = lax.broadcasted_iota(jnp.int32, (ACC_ROWS - HEAD_DIM, tk), 0)
    ones_rows = jnp.where(sub == 0, 1.0, 0.0).astype(BF16)

    def logits(hd, j, s_ref, lo):
        ks = pl.multiple_of(j * tk, tk)
        cols = slice(hd * HEAD_DIM, (hd + 1) * HEAD_DIM)
        kk = jnp.concatenate([k_ref[pl.ds(ks, tk), cols], ck_ref[pl.ds(ks, tk), cols]], axis=1)
        qq = jnp.concatenate([q_ref[lo:, cols], cq_ref[lo:, cols]], axis=1)
        s_ref[hd][:, lo:] = lax.dot_general(kk, qq, _NT, preferred_element_type=F32)

    def consume(hd, j, s_ref, lo, masked):
        s = s_ref[hd][:, lo:]
        if masked:
            key = lax.broadcasted_iota(jnp.int32, s.shape, 0)
            qry = lax.broadcasted_iota(jnp.int32, s.shape, 1)
            s = jnp.where(key <= qry, s, NEG_INF)
        vt = jnp.concatenate([vt_ref[j, hd * HEAD_DIM:(hd + 1) * HEAD_DIM, :], ones_rows], axis=0)
        m_old = m_sc[hd][:, lo:]
        m_new = jnp.maximum(m_old, jnp.max(s, axis=0, keepdims=True))
        alpha = jnp.exp2(m_old - m_new)
        p = jnp.exp2(s - m_new).astype(BF16)
        acc_sc[hd][:, lo:] = alpha * acc_sc[hd][:, lo:] + jnp.dot(vt, p, preferred_element_type=F32)
        m_sc[hd][:, lo:] = m_new

    for hd in heads:
        logits(hd, 0, sa_sc, 0)

    def pair(jj, carry):
        for hd in heads:
            logits(hd, 2 * jj + 1, sb_sc, 0)
            consume(hd, 2 * jj, sa_sc, 0, False)
        for hd in heads:
            logits(hd, 2 * jj + 2, sa_sc, 0)
            consume(hd, 2 * jj + 1, sb_sc, 0, False)
        return carry

    lax.fori_loop(0, i, pair, 0)
    for hd in heads:
        logits(hd, 2 * i + 1, sb_sc, tk)
        consume(hd, 2 * i, sa_sc, 0, True)
    for hd in heads:
        consume(hd, 2 * i + 1, sb_sc, tk, True)
        acc = acc_sc[hd][...]
        out_t = acc[0:HEAD_DIM, :] / acc[HEAD_DIM:HEAD_DIM + 1, :]
        o_ref[:, hd * HEAD_DIM:(hd + 1) * HEAD_DIM] = out_t.T.astype(BF16)


def _attention(qb, caq, kb, cak, vt, batch, seq):
    nq = seq // ATT_TQ
    nkb = seq // ATT_TK
    width = ATT_HEADS * HEAD_DIM
    qspec = lambda: pl.BlockSpec((ATT_TQ, width), lambda b, h, i: (b * nq + i, h))
    kspec = lambda: pl.BlockSpec((seq, width), lambda b, h, i: (b, h))
    per_head = [pltpu.VMEM((1, ATT_TQ), F32), pltpu.VMEM((ACC_ROWS, ATT_TQ), F32),
                pltpu.VMEM((ATT_TK, ATT_TQ), F32), pltpu.VMEM((ATT_TK, ATT_TQ), F32)]
    return pl.pallas_call(
        _attn_kernel,
        out_shape=jax.ShapeDtypeStruct((batch * seq, D_ATT), BF16),
        grid=(batch, N_HEADS // ATT_HEADS, nq),
        in_specs=[qspec(), qspec(), kspec(), kspec(),
                  pl.BlockSpec((nkb, width, ATT_TK), lambda b, h, i: (b, h, 0))],
        out_specs=qspec(),
        scratch_shapes=per_head * ATT_HEADS,
        compiler_params=_params(("arbitrary", "arbitrary", "arbitrary")),
        name="fox_prompt",
    )(qb, caq, kb, cak, vt)


def _discretise(a_re, a_im, log_dt):
    dt = jnp.exp(log_dt)
    mag = jnp.exp(dt * a_re)
    ang = dt * a_im
    abr = mag * jnp.cos(ang)
    abi = mag * jnp.sin(ang)
    e_re = abr - 1.0
    e_im = abi
    inv_den = 1.0 / (a_re * a_re + a_im * a_im)
    f_re = (e_re * a_re + e_im * a_im) * inv_den
    f_im = (e_im * a_re - e_re * a_im) * inv_den
    return abr, abi, f_re, f_im


def _powers(abr, abi, n):
    pr, pi = [jnp.ones_like(abr)], [jnp.zeros_like(abi)]
    for _ in range(n):
        r, i = pr[-1], pi[-1]
        pr.append(r * abr - i * abi)
        pi.append(r * abi + i * abr)
    return pr, pi


def _s5gen_kernel(ar_row, ai_row, dt_row, bre_ref, bim_ref, cre_ref, cim_ref,
                  vrev_ref, bsum_ref, m_ref, bmat_ref, cmat_ref, a1_ref, a16_ref):
    ns = BLOCK_STATE
    abr, abi, f_re, f_im = _discretise(ar_row[...], ai_row[...], dt_row[...])
    bre, bim = bre_ref[...], bim_ref[...]
    bbr = f_re * bre - f_im * bim
    bbi = f_re * bim + f_im * bre
    bmat = jnp.concatenate([bbr, bbi], axis=1)
    bmat_ref[...] = bmat
    pr, pi = _powers(abr, abi, CHUNK)
    a1_ref[...] = jnp.concatenate([pr[1], pi[1]], axis=1)
    a16_ref[...] = jnp.concatenate([pr[CHUNK], pi[CHUNK]], axis=1)
    for i in range(CHUNK):
        r, im = pr[CHUNK - 1 - i], pi[CHUNK - 1 - i]
        blk = jnp.concatenate([r * bbr - im * bbi, r * bbi + im * bbr], axis=1)
        bsum_ref[i * LANES:(i + 1) * LANES, :] = blk.astype(BF16)
    npow = CHUNK + 1
    stacked = jnp.concatenate(pr + pi + [jnp.zeros((LANES - 2 * npow, ns), F32)], axis=0)
    pt = stacked.T
    qr = [pt[:, t:t + 1] for t in range(npow)]
    qi = [pt[:, npow + t:npow + t + 1] for t in range(npow)]
    cre, cim = cre_ref[...], cim_ref[...]
    vrev_ref[(CHUNK - 1) * LANES:CHUNK * LANES, 0:LANES] = jnp.zeros((LANES, LANES), BF16)
    bmat_hi = bmat.astype(BF16)
    for tau in range(CHUNK + 1):
        blk = jnp.concatenate([cre * qr[tau] - cim * qi[tau],
                               -(cre * qi[tau] + cim * qr[tau])], axis=0)
        if tau == 0:
            cmat_ref[...] = blk
        else:
            m_ref[:, (tau - 1) * LANES:tau * LANES] = blk.astype(BF16)
        if tau < CHUNK:
            w = jnp.dot(bmat_hi, blk.astype(BF16), preferred_element_type=F32).astype(BF16)
            k = CHUNK - 1 - tau
            vrev_ref[k * LANES:(k + 1) * LANES, LANES:2 * LANES] = w
            if k >= 1:
                vrev_ref[(k - 1) * LANES:k * LANES, 0:LANES] = w


def _s5gen(ar_row, ai_row, dt_row, bd_bre, bd_bim, bd_cre, bd_cim):
    ns = BLOCK_STATE
    b3 = lambda s1, s2: pl.BlockSpec((None, s1, s2), lambda o: (o, 0, 0))
    out_shape = (
        jax.ShapeDtypeStruct((N_BLOCKS, CHUNK * LANES, 2 * LANES), BF16),
        jax.ShapeDtypeStruct((N_BLOCKS, CHUNK * LANES, 2 * ns), BF16),
        jax.ShapeDtypeStruct((N_BLOCKS, 2 * ns, CHUNK * LANES), BF16),
        jax.ShapeDtypeStruct((N_BLOCKS, LANES, 2 * ns), F32),
        jax.ShapeDtypeStruct((N_BLOCKS, 2 * ns, LANES), F32),
        jax.ShapeDtypeStruct((N_BLOCKS, 1, 2 * ns), F32),
        jax.ShapeDtypeStruct((N_BLOCKS, 1, 2 * ns), F32),
    )
    return pl.pallas_call(
        _s5gen_kernel,
        out_shape=out_shape,
        grid=(N_BLOCKS,),
        in_specs=[b3(1, ns), b3(1, ns), b3(1, ns),
                  b3(LANES, ns), b3(LANES, ns), b3(ns, LANES), b3(ns, LANES)],
        out_specs=(b3(CHUNK * LANES, 2 * LANES), b3(CHUNK * LANES, 2 * ns), b3(2 * ns, CHUNK * LANES),
                   b3(LANES, 2 * ns), b3(2 * ns, LANES), b3(1, 2 * ns), b3(1, 2 * ns)),
        compiler_params=_params(("arbitrary",)),
        name="s5_operators",
    )(ar_row, ai_row, dt_row, bd_bre, bd_bim, bd_cre, bd_cim)


def _s5_prompt_kernel(u_ref, vrev_ref, bsum_ref, m_ref, a16_ref, d_ref,
                      y_ref, hre_ref, him_ref, ucat_sc, s_sc, hin_sc, yt_sc, *, n_chunks):
    ns = BLOCK_STATE
    for i in range(CHUNK):
        ucat_sc[:, i * LANES:(i + 1) * LANES] = u_ref[pl.ds(i, n_chunks, stride=CHUNK), :].astype(BF16)
    s_sc[...] = jnp.dot(ucat_sc[...], bsum_ref[...], preferred_element_type=F32)
    ar = a16_ref[:, 0:ns]
    ai = a16_ref[:, ns:2 * ns]
    for jp in range(CHUNK // 2):
        j = 2 * jp
        kk = (j + 2) * LANES
        yt_sc[:, j * LANES:(j + 2) * LANES] = jnp.dot(
            ucat_sc[:, 0:kk], vrev_ref[(CHUNK - 2 - j) * LANES:, :], preferred_element_type=F32)

    hr = jnp.zeros((1, ns), F32)
    hi = jnp.zeros((1, ns), F32)
    for k in range(n_chunks):
        hin_sc[k:k + 1, 0:ns] = hr
        hin_sc[k:k + 1, ns:2 * ns] = hi
        sr = s_sc[k:k + 1, 0:ns]
        si = s_sc[k:k + 1, ns:2 * ns]
        hr, hi = ar * hr - ai * hi + sr, ar * hi + ai * hr + si
    hre_ref[...] = hr
    him_ref[...] = hi
    hin = hin_sc[...].astype(BF16)
    d = d_ref[...]
    for jp in range(CHUNK // 2):
        j = 2 * jp
        acc = yt_sc[:, j * LANES:(j + 2) * LANES] + jnp.dot(
            hin, m_ref[:, j * LANES:(j + 2) * LANES], preferred_element_type=F32)
        for jj in range(2):
            uj = u_ref[pl.ds(j + jj, n_chunks, stride=CHUNK), :]
            val = acc[:, jj * LANES:(jj + 1) * LANES] + d * uj
            y_ref[pl.ds(j + jj, n_chunks, stride=CHUNK), :] = _gelu_exact(val)


def _s5_prompt(u, vrev, bsum, m, a16, d, batch, seq):
    ns = BLOCK_STATE
    n_chunks = seq // CHUNK
    w3 = lambda s1, s2: pl.BlockSpec((None, s1, s2), lambda o, b: (o, 0, 0))
    return pl.pallas_call(
        functools.partial(_s5_prompt_kernel, n_chunks=n_chunks),
        out_shape=(jax.ShapeDtypeStruct((batch * seq, D_SSM), F32),
                   jax.ShapeDtypeStruct((batch, 1, N_GROUPS * SSM_STATE), F32),
                   jax.ShapeDtypeStruct((batch, 1, N_GROUPS * SSM_STATE), F32)),
        grid=(N_BLOCKS, batch),
        in_specs=[pl.BlockSpec((seq, LANES), lambda o, b: (b, o)),
                  w3(CHUNK * LANES, 2 * LANES), w3(CHUNK * LANES, 2 * ns), w3(2 * ns, CHUNK * LANES),
                  w3(1, 2 * ns),
                  pl.BlockSpec((1, LANES), lambda o, b: (0, o))],
        out_specs=(pl.BlockSpec((seq, LANES), lambda o, b: (b, o)),
                   pl.BlockSpec((None, 1, ns), lambda o, b: (b, 0, o)),
                   pl.BlockSpec((None, 1, ns), lambda o, b: (b, 0, o))),
        scratch_shapes=[pltpu.VMEM((n_chunks, CHUNK * LANES), BF16),
                        pltpu.VMEM((n_chunks, 2 * ns), F32),
                        pltpu.VMEM((n_chunks, 2 * ns), F32),
                        pltpu.VMEM((n_chunks, CHUNK * LANES), F32)],
        compiler_params=_params(("arbitrary", "arbitrary")),
        name="s5_prompt",
    )(u, vrev, bsum, m, a16, d)


def _s5_sample_kernel(u_ref, h0r_ref, h0i_ref, bmat_ref, cmat_ref, a1_ref, d_ref,
                      y_ref, hre_ref, him_ref, *, n_seq, n_tok):
    ns = BLOCK_STATE
    u = u_ref[...]
    bu = jnp.dot(u, bmat_ref[...], precision=HIGHEST, preferred_element_type=F32)
    ar = a1_ref[:, 0:ns]
    ai = a1_ref[:, ns:2 * ns]
    hr, hi = h0r_ref[...], h0i_ref[...]
    hs = []
    for t in range(n_tok):
        br = bu[t * n_seq:(t + 1) * n_seq, 0:ns]
        bi = bu[t * n_seq:(t + 1) * n_seq, ns:2 * ns]
        hr, hi = ar * hr - ai * hi + br, ar * hi + ai * hr + bi
        hs.append(jnp.concatenate([hr, hi], axis=1))
    hcat = jnp.concatenate(hs, axis=0)
    val = jnp.dot(hcat, cmat_ref[...], precision=HIGHEST, preferred_element_type=F32) + d_ref[...] * u
    y_ref[...] = _gelu_exact(val)
    hre_ref[...] = hr
    him_ref[...] = hi


def _s5_sample(u, h0r, h0i, bmat, cmat, a1, d, n_seq, n_tok):
    ns = BLOCK_STATE
    rows = n_seq * n_tok
    w3 = lambda s1, s2: pl.BlockSpec((None, s1, s2), lambda o: (o, 0, 0))
    col = lambda r, c: pl.BlockSpec((r, c), lambda o: (0, o))
    return pl.pallas_call(
        functools.partial(_s5_sample_kernel, n_seq=n_seq, n_tok=n_tok),
        out_shape=(jax.ShapeDtypeStruct((rows, D_SSM), F32),
                   jax.ShapeDtypeStruct((n_seq, N_GROUPS * SSM_STATE), F32),
                   jax.ShapeDtypeStruct((n_seq, N_GROUPS * SSM_STATE), F32)),
        grid=(N_BLOCKS,),
        in_specs=[col(rows, LANES), col(n_seq, ns), col(n_seq, ns),
                  w3(LANES, 2 * ns), w3(2 * ns, LANES), w3(1, 2 * ns), col(1, LANES)],
        out_specs=(col(rows, LANES), col(n_seq, ns), col(n_seq, ns)),
        compiler_params=_params(("arbitrary",)),
        name="s5_sample",
    )(u, h0r, h0i, bmat, cmat, a1, d)


PAGES_PER_STEP = 16
PAGE_GROUP = 4

PAGE_KEYS = PAGE_SIZE * N_HEADS


def _decode_kernel(pt_ref, q_ref, lfrow_ref, lfcol_ref, kn_ref, vn_ref, *refs, n_tok, n_steps):
    g = PAGES_PER_STEP
    k_refs = refs[0:g]
    v_refs = refs[g:2 * g]
    lf_refs = refs[2 * g:3 * g]
    o_ref = refs[3 * g]
    m_sc, l_sc, acc_sc, carry_sc = refs[3 * g + 1:]
    j = pl.program_id(1)
    rows = n_tok * N_HEADS
    head_mask = N_HEADS - 1
    head_shift = N_HEADS.bit_length() - 1

    @pl.when(j == 0)
    def _():
        m_sc[...] = jnp.full(m_sc.shape, NEG_INF, F32)
        l_sc[...] = jnp.zeros_like(l_sc)
        acc_sc[...] = jnp.zeros_like(acc_sc)
        carry_sc[...] = jnp.zeros_like(carry_sc)

    q = q_ref[...]
    lfcol = lfcol_ref[...]
    run = jnp.zeros((N_HEADS, 1), F32)
    pieces = []
    for t in range(n_tok):
        run = run + lfcol[t * N_HEADS:(t + 1) * N_HEADS, :]
        pieces.append(run)
    c_col = jnp.concatenate(pieces, axis=0)

    def online(s_blocks, v_blocks):
        m = m_sc[...]
        smax = s_blocks[0]
        for s in s_blocks[1:]:
            smax = jnp.maximum(smax, s)
        m_new = jnp.maximum(m, _row_max(smax))
        alpha = jnp.exp2(m - m_new)
        psum = None
        pv = None
        for s, vblk in zip(s_blocks, v_blocks):
            p = jnp.exp2(s - m_new)
            psum = p if psum is None else psum + p
            d = jnp.dot(p.astype(BF16), vblk, preferred_element_type=F32)
            pv = d if pv is None else pv + d
        l_sc[...] = alpha * l_sc[...] + _row_sum(psum)
        acc_sc[...] = alpha * acc_sc[...] + pv
        m_sc[...] = m_new

    col = lax.broadcasted_iota(jnp.int32, (rows, PAGE_KEYS), 1)
    row = lax.broadcasted_iota(jnp.int32, (rows, PAGE_KEYS), 0)
    own = (col & head_mask) == (row & head_mask)
    fixed = jnp.where(own, c_col * LOG2E, NEG_INF)

    lf = jnp.concatenate([lf_refs[pg][...] for pg in range(g)], axis=0)
    lane = lax.broadcasted_iota(jnp.int32, (g, PAGE_KEYS), 1)
    suffix = lf
    total = lf
    sh = N_HEADS
    while sh < PAGE_KEYS:
        suffix = suffix + jnp.where(lane + sh < PAGE_KEYS, pltpu.roll(suffix, PAGE_KEYS - sh, axis=1), 0.0)
        total = total + pltpu.roll(total, PAGE_KEYS - sh, axis=1)
        sh *= 2
    carry = carry_sc[...]
    past = []
    for pg in range(g):
        past.append((suffix[pg:pg + 1, :] - lf[pg:pg + 1, :] + carry) * LOG2E)
        carry = carry + total[pg:pg + 1, :]
    carry_sc[...] = carry

    def logits(pages):
        out = []
        for pg in pages:
            kp = k_refs[pg][...].reshape(PAGE_KEYS, HEAD_DIM).astype(BF16)
            s = lax.dot_general(q, kp, _NT, preferred_element_type=F32)
            out.append(s + fixed + past[pg])
        return out

    def values(pages):
        return [v_refs[pg][...].reshape(PAGE_KEYS, HEAD_DIM).astype(BF16) for pg in pages]

    groups = [list(range(a, a + PAGE_GROUP)) for a in range(0, g, PAGE_GROUP)]
    s_next = logits(groups[0])
    for gi, pages in enumerate(groups):
        s_cur = s_next
        if gi + 1 < len(groups):
            s_next = logits(groups[gi + 1])
        online(s_cur, values(pages))

    @pl.when(j == n_steps - 1)
    def _():
        ln = lax.broadcasted_iota(jnp.int32, (1, LANES), 1)
        c_row = lfrow_ref[...]
        sh2 = N_HEADS
        while sh2 < rows:
            c_row = c_row + jnp.where(ln >= sh2, pltpu.roll(c_row, sh2, axis=1), 0.0)
            sh2 *= 2
        cl = lax.broadcasted_iota(jnp.int32, (rows, LANES), 1)
        rw = lax.broadcasted_iota(jnp.int32, (rows, LANES), 0)
        valid = ((cl < rows) & ((cl & head_mask) == (rw & head_mask))
                 & ((cl >> head_shift) <= (rw >> head_shift)))
        s = lax.dot_general(q, kn_ref[...], _NT, preferred_element_type=F32)
        s = jnp.where(valid, s + (c_col - c_row) * LOG2E, NEG_INF)
        online([s], [vn_ref[...]])
        o_ref[...] = acc_sc[...] / l_sc[...]


def _decode_attention(page_table, q, lfrow, lfcol, kn, vn, cache_k, cache_v, cache_lf, n_tok):
    n_seq, n_pages = page_table.shape
    g = PAGES_PER_STEP
    n_steps = n_pages // g
    rows = n_tok * N_HEADS

    def page5(pg):
        return lambda b, j, pt: (0, pt[b * n_pages + (n_pages - 1 - (j * g + pg))], 0, 0, 0)

    def page3(pg):
        return lambda b, j, pt: (pt[b * n_pages + (n_pages - 1 - (j * g + pg))], 0, 0)

    seq3 = lambda s1, s2: pl.BlockSpec((None, s1, s2), lambda b, j, pt: (b, 0, 0))
    kv_block = (None, None, PAGE_SIZE, N_HEADS, HEAD_DIM)
    in_specs = [seq3(rows, HEAD_DIM), seq3(1, LANES), seq3(rows, 1), seq3(LANES, HEAD_DIM), seq3(LANES, HEAD_DIM)]
    in_specs += [pl.BlockSpec(kv_block, page5(pg)) for pg in range(g)]
    in_specs += [pl.BlockSpec(kv_block, page5(pg)) for pg in range(g)]
    in_specs += [pl.BlockSpec((None, 1, PAGE_KEYS), page3(pg)) for pg in range(g)]
    grid_spec = pltpu.PrefetchScalarGridSpec(
        num_scalar_prefetch=1,
        grid=(n_seq, n_steps),
        in_specs=in_specs,
        out_specs=pl.BlockSpec((None, rows, HEAD_DIM), lambda b, j, pt: (b, 0, 0)),
        scratch_shapes=[pltpu.VMEM((rows, 1), F32),
                        pltpu.VMEM((rows, 1), F32),
                        pltpu.VMEM((rows, HEAD_DIM), F32),
                        pltpu.VMEM((1, PAGE_KEYS), F32)],
    )
    return pl.pallas_call(
        functools.partial(_decode_kernel, n_tok=n_tok, n_steps=n_steps),
        out_shape=jax.ShapeDtypeStruct((n_seq, rows, HEAD_DIM), F32),
        grid_spec=grid_spec,
        compiler_params=_params(("arbitrary", "arbitrary")),
        name="fox_sample",
    )(page_table.reshape(-1), q, lfrow, lfcol, kn, vn,
      *([cache_k] * g), *([cache_v] * g), *([cache_lf] * g))


def _merge_kernel(y_ref, ya_ref, gs_ref, ga_ref, gt_ref, x_ref, wglu_ref, bglu_ref, wbs_ref, wba_ref, wo_ref,
                  ng_ref, x1_ref, xn_ref):
    y = y_ref[...]
    gl = jnp.dot(y.astype(BF16), wglu_ref[...], preferred_element_type=F32) + bglu_ref[...]
    ys = (y * _sigmoid(gl)).astype(BF16)
    ms = jnp.dot(ys, wbs_ref[...], preferred_element_type=F32)
    ma = jnp.dot(ya_ref[...], wba_ref[...], preferred_element_type=F32)
    ga = ga_ref[...]
    gate_s = jnp.concatenate([gs_ref[...], ga[:, 0:LANES]], axis=1).astype(F32)
    gate_a = jnp.concatenate([ga, gt_ref[...]], axis=1).astype(F32)
    merged = gate_s * ms + gate_a * ma
    x1 = x_ref[...] + jnp.dot(merged.astype(BF16), wo_ref[...], preferred_element_type=F32)
    x1_ref[...] = x1
    ms1 = jnp.mean(x1 * x1, axis=-1, keepdims=True)
    xn_ref[...] = (x1 * lax.rsqrt(ms1 + RMS_EPS) * ng_ref[...]).astype(BF16)


MERGE_COLS = D_MODEL + LANES


def _merge(y, ya, gates, x, w_glu, b_glu, w_bs, w_ba, w_o, n_g, tm):
    t = x.shape[0]
    gate_s, gate_a, gate_t = gates
    row = lambda c: pl.BlockSpec((tm, c), lambda i: (i, 0))
    const = lambda r, c: pl.BlockSpec((r, c), lambda i: (0, 0), pipeline_mode=pl.Buffered(1))
    return pl.pallas_call(
        _merge_kernel,
        out_shape=(jax.ShapeDtypeStruct((t, D_MODEL), F32), jax.ShapeDtypeStruct((t, D_MODEL), BF16)),
        grid=(t // tm,),
        in_specs=[row(D_SSM), row(D_ATT), row(PROJ_COLS), row(PROJ_COLS), row(LANES), row(D_MODEL),
                  const(D_SSM, D_SSM), const(1, D_SSM), const(D_SSM, MERGE_COLS), const(D_ATT, MERGE_COLS),
                  const(MERGE_COLS, D_MODEL), const(1, D_MODEL)],
        out_specs=(row(D_MODEL), row(D_MODEL)),
        compiler_params=_params(("arbitrary",)),
        name="merge_out",
    )(y, ya, gate_s, gate_a, gate_t, x, w_glu, b_glu, w_bs, w_ba, w_o, n_g)


FFN_TF = 256


def _ffn_kernel(x1_ref, xn_ref, x1s_ref, xns_ref, wg_ref, wu_ref, wd_ref, o_ref, os_ref, wg_sc, wu_sc, wd_sc):
    i, f = pl.program_id(0), pl.program_id(1)
    @pl.when(f == 0)
    def _():
        o_ref[...] = x1_ref[...]

    xn = xn_ref[...]
    wg_sc[...] = wg_ref[...].astype(BF16)
    a = jnp.dot(xn, wg_sc[...], preferred_element_type=F32)
    wu_sc[...] = wu_ref[...].astype(BF16)
    b = jnp.dot(xn, wu_sc[...], preferred_element_type=F32)
    wd_sc[...] = wd_ref[...].astype(BF16)
    o_ref[...] += jnp.dot((a * _sigmoid(a) * b).astype(BF16), wd_sc[...], preferred_element_type=F32)

    @pl.when(i == pl.num_programs(0) - 1)
    def _():
        @pl.when(f == 0)
        def _():
            os_ref[...] = x1s_ref[...]

        xs = xns_ref[...]
        a_s = jnp.dot(xs, wg_sc[...], preferred_element_type=F32)
        b_s = jnp.dot(xs, wu_sc[...], preferred_element_type=F32)
        os_ref[...] += jnp.dot((a_s * _sigmoid(a_s) * b_s).astype(BF16), wd_sc[...], preferred_element_type=F32)


def _ffn(x1, xn, x1s, xns, w_g, w_u, w_d, tm):
    t = x1.shape[0]
    ts = x1s.shape[0]
    small = lambda: pl.BlockSpec((ts, D_MODEL), lambda i, f: (0, 0))
    return pl.pallas_call(
        _ffn_kernel,
        out_shape=(jax.ShapeDtypeStruct((t, D_MODEL), F32), jax.ShapeDtypeStruct((ts, D_MODEL), F32)),
        grid=(t // tm, D_FF // FFN_TF),
        in_specs=[pl.BlockSpec((tm, D_MODEL), lambda i, f: (i, 0)),
                  pl.BlockSpec((tm, D_MODEL), lambda i, f: (i, 0), pipeline_mode=pl.Buffered(1)),
                  small(), small(),
                  pl.BlockSpec((D_MODEL, FFN_TF), lambda i, f: (0, f)),
                  pl.BlockSpec((D_MODEL, FFN_TF), lambda i, f: (0, f)),
                  pl.BlockSpec((FFN_TF, D_MODEL), lambda i, f: (f, 0))],
        out_specs=(pl.BlockSpec((tm, D_MODEL), lambda i, f: (i, 0)), small()),
        scratch_shapes=[pltpu.VMEM((D_MODEL, FFN_TF), BF16), pltpu.VMEM((D_MODEL, FFN_TF), BF16),
                        pltpu.VMEM((FFN_TF, D_MODEL), BF16)],
        compiler_params=_params(("arbitrary", "arbitrary")),
        name="ffn",
    )(x1, xn, x1s, xns, w_g, w_u, w_d)


def _same_group_mask():
    chan_group = np.arange(LANES)[:, None] // SSM_GROUP
    state_group = np.arange(BLOCK_STATE)[None, :] // SSM_STATE
    return chan_group == state_group


def _block_diag_lanes(p):
    p4 = p.reshape(N_BLOCKS, GROUPS_PER_BLOCK, SSM_STATE, SSM_GROUP)
    rows = p4.transpose(0, 3, 1, 2).reshape(N_BLOCKS, SSM_GROUP, BLOCK_STATE)
    tiled = jnp.tile(rows, (1, GROUPS_PER_BLOCK, 1))
    return jnp.where(jnp.asarray(_same_group_mask()), tiled, 0.0)


def _block_diag_sublanes(p):
    p4 = p.reshape(N_BLOCKS, GROUPS_PER_BLOCK, SSM_GROUP, SSM_STATE)
    cols = p4.transpose(0, 1, 3, 2).reshape(N_BLOCKS, BLOCK_STATE, SSM_GROUP)
    tiled = jnp.tile(cols, (1, 1, GROUPS_PER_BLOCK))
    return jnp.where(jnp.asarray(_same_group_mask().T), tiled, 0.0)


def kernel(x_prompt, x_sample, cache_k, cache_v, cache_logf, state_ssm_re, state_ssm_im, page_table,
           norm_mix_g, w_in, b_f, q_norm_g, k_norm_g, ssm_a_re, ssm_a_im, ssm_log_dt, ssm_b_re, ssm_b_im,
           ssm_c_re, ssm_c_im, ssm_d, w_glu, b_glu, w_br_ssm, w_br_att, w_out, norm_ffn_g,
           w_ffn_gate, w_ffn_up, w_ffn_down):
    batch, seq, _ = x_prompt.shape
    n_seq, n_tok, _ = x_sample.shape
    l = 0
    n_qkv = D_SSM + 3 * D_ATT

    w_in_b = w_in[l].astype(BF16)
    n_slabs = w_in_b.shape[1] // PROJ_COLS
    n_tail = w_in_b.shape[1] - n_slabs * PROJ_COLS
    assert n_qkv == 2 * PROJ_COLS and n_tail == GATE_SHIFT == N_HEADS
    w_tail = jnp.pad(w_in_b[:, n_slabs * PROJ_COLS:], ((0, 0), (0, LANES - n_tail)))
    b_f_pad = jnp.pad(b_f[l], (0, LANES - N_HEADS)).reshape(1, LANES)
    norm_g = norm_mix_g[l].reshape(1, D_MODEL)
    q_g = q_norm_g[l].reshape(1, HEAD_DIM)
    k_g = k_norm_g[l].reshape(1, HEAD_DIM)
    ns = BLOCK_STATE
    ar_row = ssm_a_re[l].reshape(N_BLOCKS, 1, ns)
    ai_row = ssm_a_im[l].reshape(N_BLOCKS, 1, ns)
    dt_row = jnp.repeat(ssm_log_dt[l], SSM_STATE).reshape(N_BLOCKS, 1, ns)
    bd_bre = _block_diag_lanes(ssm_b_re[l])
    bd_bim = _block_diag_lanes(ssm_b_im[l])
    bd_cre = _block_diag_sublanes(ssm_c_re[l])
    bd_cim = _block_diag_sublanes(ssm_c_im[l])
    d_row = ssm_d[l].reshape(1, D_SSM)
    w_glu_b = w_glu[l].astype(BF16)
    b_glu_r = b_glu[l].reshape(1, D_SSM)
    shift_pad = (GATE_SHIFT, MERGE_COLS - D_MODEL - GATE_SHIFT)
    w_bs = jnp.pad(w_br_ssm[l].astype(BF16), ((0, 0), shift_pad))
    w_ba = jnp.pad(w_br_att[l].astype(BF16), ((0, 0), shift_pad))
    w_o = jnp.pad(w_out[l].astype(BF16), (shift_pad, (0, 0)))
    n_g = norm_ffn_g[l].reshape(1, D_MODEL)
    w_g = w_ffn_gate[l]
    w_u = w_ffn_up[l]
    w_d = w_ffn_down[l]

    vrev, bsum, m_op, bmat, cmat, a1, a16 = _s5gen(ar_row, ai_row, dt_row, bd_bre, bd_bim, bd_cre, bd_cim)

    xp = x_prompt.reshape(batch * seq, D_MODEL)
    rows = n_seq * n_tok
    xs = x_sample.transpose(1, 0, 2).reshape(rows, D_MODEL)
    ((u, qb, k, kb, v, vt, gates, logf),
     (us, qbs, ks, kbs, vs, vbs, gates_s, logf_s)) = _inproj(xp, xs, norm_g, w_in_b, w_tail, b_f_pad, q_g, k_g, tm=512)
    caq, cak = _cumaug(logf, batch, seq)
    y_ssm, hre, him = _s5_prompt(u, vrev, bsum, m_op, a16, d_row, batch, seq)
    y_att = _attention(qb, caq, kb, cak, vt, batch, seq)
    x1, xn1 = _merge(y_ssm, y_att, gates, xp, w_glu_b, b_glu_r, w_bs, w_ba, w_o, n_g, tm=256)

    ys_ssm, hre_s, him_s = _s5_sample(us, state_ssm_re[l].reshape(n_seq, -1), state_ssm_im[l].reshape(n_seq, -1),
                                      bmat, cmat, a1, d_row, n_seq, n_tok)
    by_seq = lambda a: a.reshape(n_tok, n_seq, -1).transpose(1, 0, 2)
    th = n_tok * N_HEADS
    th_rows = lambda a: by_seq(a).reshape(n_seq, th, HEAD_DIM)
    pad_keys = lambda a: jnp.pad(th_rows(a), ((0, 0), (0, LANES - th), (0, 0)))
    lf_s = by_seq(logf_s[:, :N_HEADS]).reshape(n_seq, th)
    lfrow = jnp.pad(lf_s, ((0, 0), (0, LANES - th))).reshape(n_seq, 1, LANES)
    lfcol = lf_s.reshape(n_seq, th, 1)
    n_pool = cache_k.shape[1]
    ya_s = _decode_attention(page_table, th_rows(qbs), lfrow, lfcol, pad_keys(kbs), pad_keys(vbs),
                             cache_k, cache_v, cache_logf[l].reshape(n_pool, 1, PAGE_KEYS), n_tok)
    ya_s = ya_s.reshape(n_seq, n_tok, D_ATT).transpose(1, 0, 2).reshape(rows, D_ATT).astype(BF16)
    x1s, xn1s = _merge(ys_ssm, ya_s, gates_s, xs, w_glu_b, b_glu_r, w_bs, w_ba, w_o, n_g, tm=rows)
    y_p, y_s = _ffn(x1, xn1, x1s, xn1s, w_g, w_u, w_d, tm=1024)
    y_p = y_p.reshape(batch, seq, D_MODEL)
    y_s = y_s.reshape(n_tok, n_seq, D_MODEL).transpose(1, 0, 2)

    heads = lambda a, b_, t_: a.reshape(1, b_, t_, N_HEADS, HEAD_DIM)
    tok_major = lambda a: a.reshape(n_tok, n_seq, -1).transpose(1, 0, 2)
    return (
        y_p, y_s,
        heads(k, batch, seq), heads(v, batch, seq),
        logf[:, :N_HEADS].reshape(1, batch, seq, N_HEADS),
        hre.reshape(1, batch, N_GROUPS, SSM_STATE), him.reshape(1, batch, N_GROUPS, SSM_STATE),
        heads(tok_major(ks), n_seq, n_tok), heads(tok_major(vs), n_seq, n_tok),
        tok_major(logf_s[:, :N_HEADS]).reshape(1, n_seq, n_tok, N_HEADS),
        hre_s.reshape(1, n_seq, N_GROUPS, SSM_STATE), him_s.reshape(1, n_seq, N_GROUPS, SSM_STATE),
    )
```

```python
import functools
import math

import jax
import jax.numpy as jnp
import numpy as np
from jax import lax
from jax.experimental import pallas as pl
from jax.experimental.pallas import tpu as pltpu

F32 = jnp.float32
BF16 = jnp.bfloat16
HIGHEST = lax.Precision.HIGHEST

D_MODEL = 2048
D_SSM = D_MODEL // 2
SSM_GROUP = 16
N_GROUPS = D_SSM // SSM_GROUP
SSM_STATE = 64
HEAD_DIM = 128
N_HEADS = (D_MODEL // 2) // HEAD_DIM
D_ATT = N_HEADS * HEAD_DIM
D_FF = ((8 * D_MODEL + 3 * 256 - 1) // (3 * 256)) * 256
PAGE_SIZE = 128
RMS_EPS = 1e-6
NEG_INF = -1e30

LANES = 128
GROUPS_PER_BLOCK = LANES // SSM_GROUP
N_BLOCKS = N_GROUPS // GROUPS_PER_BLOCK
BLOCK_STATE = GROUPS_PER_BLOCK * SSM_STATE
CHUNK = 16
PROJ_COLS = 2048
GATE_SHIFT = N_HEADS
ATT_TK = 512
ATT_TQ = 2 * ATT_TK
VMEM_LIMIT = 56 * 1024 * 1024
PROJ_TM = 512
MERGE_TM = 256
FFN_TM = 1024

LOG2E = math.log2(math.e)
QK_SCALE_LOG2 = (HEAD_DIM ** -0.5) * LOG2E

_NT = (((1,), (1,)), ((), ()))


def _params(sem):
    return pltpu.CompilerParams(dimension_semantics=sem, vmem_limit_bytes=VMEM_LIMIT)


def _gelu_exact(x):
    return 0.5 * x * (1.0 + lax.erf(x * math.sqrt(0.5)))


def _sigmoid(x):
    return 1.0 / (1.0 + jnp.exp(-x))


def _fold_lanes(x, op):
    acc = x[:, 0:LANES]
    for c in range(1, x.shape[1] // LANES):
        acc = op(acc, x[:, c * LANES:(c + 1) * LANES])
    return acc


def _row_max(x):
    return jnp.max(_fold_lanes(x, jnp.maximum), axis=-1, keepdims=True)


def _row_sum(x):
    return jnp.sum(_fold_lanes(x, jnp.add), axis=-1, keepdims=True)


def _log_sigmoid(x):
    return jnp.minimum(x, 0.0) - jnp.log1p(jnp.exp(-jnp.abs(x)))


def _head_norm(z, gain):
    outs = []
    for h in range(N_HEADS):
        blk = z[:, h * HEAD_DIM:(h + 1) * HEAD_DIM]
        ms = jnp.mean(blk * blk, axis=-1, keepdims=True)
        outs.append(blk * lax.rsqrt(ms + RMS_EPS) * gain)
    return jnp.concatenate(outs, axis=1)


def _proj_kernel(*refs, kind, nt, v_transposed):
    za, zb, zs = refs[-3:]
    n_out = _PROJ_OUTPUTS[kind]
    in_p, in_s, w_ref = refs[0], refs[1], refs[2]
    extra = refs[3:len(refs) - 3 - 2 * n_out]
    out_p = refs[len(refs) - 3 - 2 * n_out:len(refs) - 3 - n_out]
    out_s = refs[len(refs) - 3 - n_out:len(refs) - 3]
    s = pl.program_id(0)
    half = PROJ_COLS // 2

    def group(in_ref, outs, transposed):
        if kind == "uq":
            g_ref, qg_ref = extra
            xn_ref, u_ref, qb_ref = outs

            def lhs():
                x = in_ref[...]
                ms = jnp.mean(x * x, axis=-1, keepdims=True)
                xn = (x * lax.rsqrt(ms + RMS_EPS) * g_ref[...]).astype(BF16)
                xn_ref[...] = xn
                return xn

            def epilogue(z):
                u_ref[...] = z[:, 0:half]
                qb_ref[...] = (_head_norm(z[:, half:], qg_ref[...]) * QK_SCALE_LOG2).astype(BF16)
        elif kind == "kv":
            (kg_ref,) = extra
            k_ref, kb_ref, v_ref, vb_ref = outs
            lhs = lambda: in_ref[...]

            def epilogue(z):
                kn = _head_norm(z[:, 0:half], kg_ref[...])
                k_ref[...] = kn
                kb_ref[...] = kn.astype(BF16)
                v = z[:, half:]
                v_ref[...] = v
                if transposed:
                    vt = v.T
                    for c in range(vb_ref.shape[0]):
                        vb_ref[c] = vt[:, c * ATT_TK:(c + 1) * ATT_TK].astype(BF16)
                else:
                    vb_ref[...] = v.astype(BF16)
        elif kind == "gate_ssm":
            (bf_ref,) = extra
            gate_ref, logf_ref = outs
            lhs = lambda: in_ref[...]

            def epilogue(z, c0=0):
                gate_ref[:, c0:c0 + z.shape[1]] = _sigmoid(z).astype(BF16)
                if c0 == 0:
                    logf_ref[...] = _log_sigmoid(z[:, 0:LANES] + bf_ref[...])
        else:
            (wt_ref,) = extra
            gate_ref, tail_ref = outs

            def lhs():
                xn = in_ref[...]
                tail_ref[...] = _sigmoid(jnp.dot(xn, wt_ref[...], preferred_element_type=F32)).astype(BF16)
                return xn

            def epilogue(z, c0=0):
                gate_ref[:, c0:c0 + z.shape[1]] = _sigmoid(z).astype(BF16)
        return lhs, epilogue

    lhs_p, epilogue_p = group(in_p, out_p, v_transposed)
    lhs_s, epilogue_s = group(in_s, out_s, False)

    @pl.when(s == 0)
    def _():
        zb[...] = jnp.zeros_like(zb)

    def body(z_write, z_read):
        if kind in ("gate_ssm", "gate_att"):
            xn = lhs_p()
            for c0 in (0, half):
                epilogue_p(z_read[:, c0:c0 + half], c0)
                z_write[:, c0:c0 + half] = jnp.dot(xn, w_ref[:, c0:c0 + half], preferred_element_type=F32)
        else:
            epilogue_p(z_read[...])
            z_write[...] = jnp.dot(lhs_p(), w_ref[...], preferred_element_type=F32)

    pl.when((s < nt) & (s % 2 == 0))(lambda: body(za, zb))
    pl.when((s < nt) & (s % 2 == 1))(lambda: body(zb, za))

    @pl.when(s == nt)
    def _():
        epilogue_p((za if (nt - 1) % 2 == 0 else zb)[...])
        zs[...] = jnp.dot(lhs_s(), w_ref[...], preferred_element_type=F32)

    @pl.when(s == nt + 1)
    def _():
        epilogue_s(zs[...])


_PROJ_OUTPUTS = {"uq": 3, "kv": 4, "gate_ssm": 2, "gate_att": 2}


def _inproj(x, xs, norm_g, w_in_b, w_tail, b_f, q_g, k_g, tm):
    t, ts = x.shape[0], xs.shape[0]
    nt = t // tm
    hi = nt - 1
    cur = lambda s: (jnp.minimum(s, hi), 0)
    lag = lambda s: (jnp.clip(s - 1, 0, hi), 0)
    lag3 = lambda s: (jnp.clip(s - 1, 0, hi), 0, 0)
    const = lambda s: (0, 0)
    sds = jax.ShapeDtypeStruct

    def call(kind, slab_index, row_inputs, extra, extra_specs, widths, dtypes, lagged, vt=False):
        in_width = row_inputs[0].shape[1]
        in_specs = [pl.BlockSpec((tm, in_width), cur), pl.BlockSpec((ts, in_width), const),
                    pl.BlockSpec((D_MODEL, PROJ_COLS), lambda s: (0, slab_index), pipeline_mode=pl.Buffered(1))]
        out_shape, out_specs = [], []
        for rows_n, block_n, prompt in ((t, tm, True), (ts, ts, False)):
            for w, dt, lg in zip(widths, dtypes, lagged):
                if vt and prompt and w is None:
                    out_shape.append(sds((t // ATT_TK, D_ATT, ATT_TK), BF16))
                    out_specs.append(pl.BlockSpec((tm // ATT_TK, D_ATT, ATT_TK), lag3))
                    continue
                w = D_ATT if w is None else w
                out_shape.append(sds((rows_n, w), dt))
                index = (lag if lg else cur) if prompt else const
                out_specs.append(pl.BlockSpec((block_n, w), index))
        outs = pl.pallas_call(
            functools.partial(_proj_kernel, kind=kind, nt=nt, v_transposed=vt),
            out_shape=tuple(out_shape),
            grid=(nt + 2,),
            in_specs=in_specs + extra_specs,
            out_specs=tuple(out_specs),
            scratch_shapes=[pltpu.VMEM((tm, PROJ_COLS), F32), pltpu.VMEM((tm, PROJ_COLS), F32),
                            pltpu.VMEM((ts, PROJ_COLS), F32)],
            compiler_params=_params(("arbitrary",)),
            name="proj_" + kind,
        )(*row_inputs, w_in_b, *extra)
        n = len(widths)
        return outs[:n], outs[n:]

    gain = lambda w: pl.BlockSpec((1, w), const)
    (xn, u, qb), (xns, us, qbs) = call(
        "uq", 0, (x, xs), (norm_g, q_g), [gain(D_MODEL), gain(HEAD_DIM)],
        (D_MODEL, D_SSM, D_ATT), (BF16, F32, BF16), (False, True, True))
    (k, kb, v, vb), (ks, kbs, vs, vbs) = call(
        "kv", 1, (xn, xns), (k_g,), [gain(HEAD_DIM)],
        (D_ATT, D_ATT, D_ATT, None), (F32, BF16, F32, BF16), (True, True, True, True), vt=True)
    (gate_s, logf), (gate_ss, logfs) = call(
        "gate_ssm", 2, (xn, xns), (b_f,), [gain(LANES)],
        (PROJ_COLS, LANES), (BF16, F32), (True, True))
    (gate_a, gate_t), (gate_as, gate_ts) = call(
        "gate_att", 3, (xn, xns), (w_tail,), [pl.BlockSpec((D_MODEL, LANES), const)],
        (PROJ_COLS, LANES), (BF16, BF16), (True, False))
    return ((u, qb, k, kb, v, vb, (gate_s, gate_a, gate_t), logf),
            (us, qbs, ks, kbs, vs, vbs, (gate_ss, gate_as, gate_ts), logfs))


CUM_ROWS = 256


def _cumaug_kernel(lf_ref, eq_ref, ek_ref, caq_ref, cak_ref, carry_sc):
    @pl.when(pl.program_id(1) == 0)
    def _():
        carry_sc[...] = jnp.zeros_like(carry_sc)

    lf = lf_ref[...]
    r = lax.broadcasted_iota(jnp.int32, (CUM_ROWS, CUM_ROWS), 0)
    c = lax.broadcasted_iota(jnp.int32, (CUM_ROWS, CUM_ROWS), 1)
    tri = (c <= r).astype(BF16)
    lf_hi = lf.astype(BF16)
    lf_r = lf - lf_hi.astype(F32)
    lf_mid = lf_r.astype(BF16)
    lf_lo = (lf_r - lf_mid.astype(F32)).astype(BF16)
    parts = jnp.dot(tri, jnp.concatenate([lf_hi, lf_mid, lf_lo], axis=1), preferred_element_type=F32)
    cs = (parts[:, 0:LANES] + parts[:, LANES:2 * LANES] + parts[:, 2 * LANES:3 * LANES]) + carry_sc[...]
    carry_sc[...] = cs[CUM_ROWS - 1:CUM_ROWS, :]
    cs2 = cs * LOG2E
    hi = cs2.astype(BF16).astype(F32)
    r1 = cs2 - hi
    mid = r1.astype(BF16).astype(F32)
    lo = (r1 - mid).astype(BF16).astype(F32)
    lane = lax.broadcasted_iota(jnp.int32, (CUM_ROWS, LANES), 1)
    packed = jnp.where(lane < N_HEADS, hi,
                       jnp.where(lane < 2 * N_HEADS, pltpu.roll(mid, N_HEADS, axis=1),
                                 jnp.where(lane < 3 * N_HEADS, pltpu.roll(lo, 2 * N_HEADS, axis=1),
                                           jnp.where(lane == 3 * N_HEADS, 1.0, 0.0)))).astype(BF16)
    caq_ref[...] = jnp.dot(packed, eq_ref[...], preferred_element_type=F32).astype(BF16)
    cak_ref[...] = jnp.dot(packed, ek_ref[...], preferred_element_type=F32).astype(BF16)


def _placement_matrices():
    eq = np.zeros((LANES, D_ATT), np.float32)
    ek = np.zeros((LANES, D_ATT), np.float32)
    for h in range(N_HEADS):
        base = h * HEAD_DIM
        for piece in range(3):
            eq[piece * N_HEADS + h, base + piece] = 1.0
            ek[piece * N_HEADS + h, base + 3 + piece] = -1.0
            eq[3 * N_HEADS, base + 3 + piece] = 1.0
            ek[3 * N_HEADS, base + piece] = 1.0
    return jnp.asarray(eq, BF16), jnp.asarray(ek, BF16)


def _cumaug(logf, batch, seq):
    nb = seq // CUM_ROWS
    eq, ek = _placement_matrices()
    const = lambda: pl.BlockSpec((LANES, D_ATT), lambda b, i: (0, 0))
    return pl.pallas_call(
        _cumaug_kernel,
        out_shape=(jax.ShapeDtypeStruct((batch * seq, D_ATT), BF16),
                   jax.ShapeDtypeStruct((batch * seq, D_ATT), BF16)),
        grid=(batch, nb),
        in_specs=[pl.BlockSpec((CUM_ROWS, LANES), lambda b, i: (b * nb + i, 0)), const(), const()],
        out_specs=(pl.BlockSpec((CUM_ROWS, D_ATT), lambda b, i: (b * nb + i, 0)),
                   pl.BlockSpec((CUM_ROWS, D_ATT), lambda b, i: (b * nb + i, 0))),
        scratch_shapes=[pltpu.VMEM((1, LANES), F32)],
        compiler_params=_params(("arbitrary", "arbitrary")),
        name="cumaug",
    )(logf, eq, ek)


BF16_SUBLANES = 16
ACC_ROWS = HEAD_DIM + BF16_SUBLANES


ATT_HEADS = 4


def _attn_kernel(q_ref, cq_ref, k_ref, ck_ref, vt_ref, o_ref, *scratch):
    i = pl.program_id(2)
    tk = ATT_TK
    heads = range(ATT_HEADS)
    m_sc, acc_sc, sa_sc, sb_sc = (scratch[0::4], scratch[1::4], scratch[2::4], scratch[3::4])
    for hd in heads:
        m_sc[hd][...] = jnp.full(m_sc[hd].shape, NEG_INF, F32)
        acc_sc[hd][...] = jnp.zeros_like(acc_sc[hd])
    sub = lax.broadcasted_iota(jnp.int32, (ACC_ROWS - HEAD_DIM, tk), 0)
    ones_rows = jnp.where(sub == 0, 1.0, 0.0).astype(BF16)

    def logits(hd, j, s_ref, lo):
        ks = pl.multiple_of(j * tk, tk)
        cols = slice(hd * HEAD_DIM, (hd + 1) * HEAD_DIM)
        kk = jnp.concatenate([k_ref[pl.ds(ks, tk), cols], ck_ref[pl.ds(ks, tk), cols]], axis=1)
        qq = jnp.concatenate([q_ref[lo:, cols], cq_ref[lo:, cols]], axis=1)
        s_ref[hd][:, lo:] = lax.dot_general(kk, qq, _NT, preferred_element_type=F32)

    def consume(hd, j, s_ref, lo, masked):
        s = s_ref[hd][:, lo:]
        if masked:
            key = lax.broadcasted_iota(jnp.int32, s.shape, 0)
            qry = lax.broadcasted_iota(jnp.int32, s.shape, 1)
            s = jnp.where(key <= qry, s, NEG_INF)
        vt = jnp.concatenate([vt_ref[j, hd * HEAD_DIM:(hd + 1) * HEAD_DIM, :], ones_rows], axis=0)
        m_old = m_sc[hd][:, lo:]
        m_new = jnp.maximum(m_old, jnp.max(s, axis=0, keepdims=True))
        alpha = jnp.exp2(m_old - m_new)
        p = jnp.exp2(s - m_new).astype(BF16)
        acc_sc[hd][:, lo:] = alpha * acc_sc[hd][:, lo:] + jnp.dot(vt, p, preferred_element_type=F32)
        m_sc[hd][:, lo:] = m_new

    for hd in heads:
        logits(hd, 0, sa_sc, 0)

    def pair(jj, carry):
        for hd in heads:
            logits(hd, 2 * jj + 1, sb_sc, 0)
            consume(hd, 2 * jj, sa_sc, 0, False)
        for hd in heads:
            logits(hd, 2 * jj + 2, sa_sc, 0)
            consume(hd, 2 * jj + 1, sb_sc, 0, False)
        return carry

    lax.fori_loop(0, i, pair, 0)
    for hd in heads:
        logits(hd, 2 * i + 1, sb_sc, tk)
        consume(hd, 2 * i, sa_sc, 0, True)
    for hd in heads:
        consume(hd, 2 * i + 1, sb_sc, tk, True)
        acc = acc_sc[hd][...]
        out_t = acc[0:HEAD_DIM, :] / acc[HEAD_DIM:HEAD_DIM + 1, :]
        o_ref[:, hd * HEAD_DIM:(hd + 1) * HEAD_DIM] = out_t.T.astype(BF16)


def _attention(qb, caq, kb, cak, vt, batch, seq):
    nq = seq // ATT_TQ
    nkb = seq // ATT_TK
    width = ATT_HEADS * HEAD_DIM
    qspec = lambda: pl.BlockSpec((ATT_TQ, width), lambda b, h, i: (b * nq + i, h))
    kspec = lambda: pl.BlockSpec((seq, width), lambda b, h, i: (b, h))
    per_head = [pltpu.VMEM((1, ATT_TQ), F32), pltpu.VMEM((ACC_ROWS, ATT_TQ), F32),
                pltpu.VMEM((ATT_TK, ATT_TQ), F32), pltpu.VMEM((ATT_TK, ATT_TQ), F32)]
    return pl.pallas_call(
        _attn_kernel,
        out_shape=jax.ShapeDtypeStruct((batch * seq, D_ATT), BF16),
        grid=(batch, N_HEADS // ATT_HEADS, nq),
        in_specs=[qspec(), qspec(), kspec(), kspec(),
                  pl.BlockSpec((nkb, width, ATT_TK), lambda b, h, i: (b, h, 0))],
        out_specs=qspec(),
        scratch_shapes=per_head * ATT_HEADS,
        compiler_params=_params(("arbitrary", "arbitrary", "arbitrary")),
        name="fox_prompt",
    )(qb, caq, kb, cak, vt)


def _discretise(a_re, a_im, log_dt):
    dt = jnp.exp(log_dt)
    mag = jnp.exp(dt * a_re)
    ang = dt * a_im
    abr = mag * jnp.cos(ang)
    abi = mag * jnp.sin(ang)
    e_re = abr - 1.0
    e_im = abi
    inv_den = 1.0 / (a_re * a_re + a_im * a_im)
    f_re = (e_re * a_re + e_im * a_im) * inv_den
    f_im = (e_im * a_re - e_re * a_im) * inv_den
    return abr, abi, f_re, f_im


def _powers(abr, abi, n):
    pr, pi = [jnp.ones_like(abr)], [jnp.zeros_like(abi)]
    for _ in range(n):
        r, i = pr[-1], pi[-1]
        pr.append(r * abr - i * abi)
        pi.append(r * abi + i * abr)
    return pr, pi


def _s5gen_kernel(ar_row, ai_row, dt_row, bre_ref, bim_ref, cre_ref, cim_ref,
                  vrev_ref, bsum_ref, m_ref, bmat_ref, cmat_ref, a1_ref, a16_ref):
    ns = BLOCK_STATE
    abr, abi, f_re, f_im = _discretise(ar_row[...], ai_row[...], dt_row[...])
    bre, bim = bre_ref[...], bim_ref[...]
    bbr = f_re * bre - f_im * bim
    bbi = f_re * bim + f_im * bre
    bmat = jnp.concatenate([bbr, bbi], axis=1)
    bmat_ref[...] = bmat
    pr, pi = _powers(abr, abi, CHUNK)
    a1_ref[...] = jnp.concatenate([pr[1], pi[1]], axis=1)
    a16_ref[...] = jnp.concatenate([pr[CHUNK], pi[CHUNK]], axis=1)
    for i in range(CHUNK):
        r, im = pr[CHUNK - 1 - i], pi[CHUNK - 1 - i]
        blk = jnp.concatenate([r * bbr - im * bbi, r * bbi + im * bbr], axis=1)
        bsum_ref[i * LANES:(i + 1) * LANES, :] = blk.astype(BF16)
    npow = CHUNK + 1
    stacked = jnp.concatenate(pr + pi + [jnp.zeros((LANES - 2 * npow, ns), F32)], axis=0)
    pt = stacked.T
    qr = [pt[:, t:t + 1] for t in range(npow)]
    qi = [pt[:, npow + t:npow + t + 1] for t in range(npow)]
    cre, cim = cre_ref[...], cim_ref[...]
    vrev_ref[(CHUNK - 1) * LANES:CHUNK * LANES, 0:LANES] = jnp.zeros((LANES, LANES), BF16)
    bmat_hi = bmat.astype(BF16)
    for tau in range(CHUNK + 1):
        blk = jnp.concatenate([cre * qr[tau] - cim * qi[tau],
                               -(cre * qi[tau] + cim * qr[tau])], axis=0)
        if tau == 0:
            cmat_ref[...] = blk
        else:
            m_ref[:, (tau - 1) * LANES:tau * LANES] = blk.astype(BF16)
        if tau < CHUNK:
            w = jnp.dot(bmat_hi, blk.astype(BF16), preferred_element_type=F32).astype(BF16)
            k = CHUNK - 1 - tau
            vrev_ref[k * LANES:(k + 1) * LANES, LANES:2 * LANES] = w
            if k >= 1:
                vrev_ref[(k - 1) * LANES:k * LANES, 0:LANES] = w


def _s5gen(ar_row, ai_row, dt_row, bd_bre, bd_bim, bd_cre, bd_cim):
    ns = BLOCK_STATE
    b3 = lambda s1, s2: pl.BlockSpec((None, s1, s2), lambda o: (o, 0, 0))
    out_shape = (
        jax.ShapeDtypeStruct((N_BLOCKS, CHUNK * LANES, 2 * LANES), BF16),
        jax.ShapeDtypeStruct((N_BLOCKS, CHUNK * LANES, 2 * ns), BF16),
        jax.ShapeDtypeStruct((N_BLOCKS, 2 * ns, CHUNK * LANES), BF16),
        jax.ShapeDtypeStruct((N_BLOCKS, LANES, 2 * ns), F32),
        jax.ShapeDtypeStruct((N_BLOCKS, 2 * ns, LANES), F32),
        jax.ShapeDtypeStruct((N_BLOCKS, 1, 2 * ns), F32),
        jax.ShapeDtypeStruct((N_BLOCKS, 1, 2 * ns), F32),
    )
    return pl.pallas_call(
        _s5gen_kernel,
        out_shape=out_shape,
        grid=(N_BLOCKS,),
        in_specs=[b3(1, ns), b3(1, ns), b3(1, ns),
                  b3(LANES, ns), b3(LANES, ns), b3(ns, LANES), b3(ns, LANES)],
        out_specs=(b3(CHUNK * LANES, 2 * LANES), b3(CHUNK * LANES, 2 * ns), b3(2 * ns, CHUNK * LANES),
                   b3(LANES, 2 * ns), b3(2 * ns, LANES), b3(1, 2 * ns), b3(1, 2 * ns)),
        compiler_params=_params(("arbitrary",)),
        name="s5_operators",
    )(ar_row, ai_row, dt_row, bd_bre, bd_bim, bd_cre, bd_cim)


def _s5_prompt_kernel(u_ref, vrev_ref, bsum_ref, m_ref, a16_ref, d_ref,
                      y_ref, hre_ref, him_ref, ucat_sc, s_sc, hin_sc, yt_sc, *, n_chunks):
    ns = BLOCK_STATE
    for i in range(CHUNK):
        ucat_sc[:, i * LANES:(i + 1) * LANES] = u_ref[pl.ds(i, n_chunks, stride=CHUNK), :].astype(BF16)
    s_sc[...] = jnp.dot(ucat_sc[...], bsum_ref[...], preferred_element_type=F32)
    ar = a16_ref[:, 0:ns]
    ai = a16_ref[:, ns:2 * ns]
    for jp in range(CHUNK // 2):
        j = 2 * jp
        kk = (j + 2) * LANES
        yt_sc[:, j * LANES:(j + 2) * LANES] = jnp.dot(
            ucat_sc[:, 0:kk], vrev_ref[(CHUNK - 2 - j) * LANES:, :], preferred_element_type=F32)

    hr = jnp.zeros((1, ns), F32)
    hi = jnp.zeros((1, ns), F32)
    for k in range(n_chunks):
        hin_sc[k:k + 1, 0:ns] = hr
        hin_sc[k:k + 1, ns:2 * ns] = hi
        sr = s_sc[k:k + 1, 0:ns]
        si = s_sc[k:k + 1, ns:2 * ns]
        hr, hi = ar * hr - ai * hi + sr, ar * hi + ai * hr + si
    hre_ref[...] = hr
    him_ref[...] = hi
    hin = hin_sc[...].astype(BF16)
    d = d_ref[...]
    for jp in range(CHUNK // 2):
        j = 2 * jp
        acc = yt_sc[:, j * LANES:(j + 2) * LANES] + jnp.dot(
            hin, m_ref[:, j * LANES:(j + 2) * LANES], preferred_element_type=F32)
        for jj in range(2):
            uj = u_ref[pl.ds(j + jj, n_chunks, stride=CHUNK), :]
            val = acc[:, jj * LANES:(jj + 1) * LANES] + d * uj
            y_ref[pl.ds(j + jj, n_chunks, stride=CHUNK), :] = _gelu_exact(val)


def _s5_prompt(u, vrev, bsum, m, a16, d, batch, seq):
    ns = BLOCK_STATE
    n_chunks = seq // CHUNK
    w3 = lambda s1, s2: pl.BlockSpec((None, s1, s2), lambda o, b: (o, 0, 0))
    return pl.pallas_call(
        functools.partial(_s5_prompt_kernel, n_chunks=n_chunks),
        out_shape=(jax.ShapeDtypeStruct((batch * seq, D_SSM), F32),
                   jax.ShapeDtypeStruct((batch, 1, N_GROUPS * SSM_STATE), F32),
                   jax.ShapeDtypeStruct((batch, 1, N_GROUPS * SSM_STATE), F32)),
        grid=(N_BLOCKS, batch),
        in_specs=[pl.BlockSpec((seq, LANES), lambda o, b: (b, o)),
                  w3(CHUNK * LANES, 2 * LANES), w3(CHUNK * LANES, 2 * ns), w3(2 * ns, CHUNK * LANES),
                  w3(1, 2 * ns),
                  pl.BlockSpec((1, LANES), lambda o, b: (0, o))],
        out_specs=(pl.BlockSpec((seq, LANES), lambda o, b: (b, o)),
                   pl.BlockSpec((None, 1, ns), lambda o, b: (b, 0, o)),
                   pl.BlockSpec((None, 1, ns), lambda o, b: (b, 0, o))),
        scratch_shapes=[pltpu.VMEM((n_chunks, CHUNK * LANES), BF16),
                        pltpu.VMEM((n_chunks, 2 * ns), F32),
                        pltpu.VMEM((n_chunks, 2 * ns), F32),
                        pltpu.VMEM((n_chunks, CHUNK * LANES), F32)],
        compiler_params=_params(("arbitrary", "arbitrary")),
        name="s5_prompt",
    )(u, vrev, bsum, m, a16, d)


def _s5_sample_kernel(u_ref, h0r_ref, h0i_ref, bmat_ref, cmat_ref, a1_ref, d_ref,
                      y_ref, hre_ref, him_ref, *, n_seq, n_tok):
    ns = BLOCK_STATE
    u = u_ref[...]
    bu = jnp.dot(u, bmat_ref[...], precision=HIGHEST, preferred_element_type=F32)
    ar = a1_ref[:, 0:ns]
    ai = a1_ref[:, ns:2 * ns]
    hr, hi = h0r_ref[...], h0i_ref[...]
    hs = []
    for t in range(n_tok):
        br = bu[t * n_seq:(t + 1) * n_seq, 0:ns]
        bi = bu[t * n_seq:(t + 1) * n_seq, ns:2 * ns]
        hr, hi = ar * hr - ai * hi + br, ar * hi + ai * hr + bi
        hs.append(jnp.concatenate([hr, hi], axis=1))
    hcat = jnp.concatenate(hs, axis=0)
    val = jnp.dot(hcat, cmat_ref[...], precision=HIGHEST, preferred_element_type=F32) + d_ref[...] * u
    y_ref[...] = _gelu_exact(val)
    hre_ref[...] = hr
    him_ref[...] = hi


def _s5_sample(u, h0r, h0i, bmat, cmat, a1, d, n_seq, n_tok):
    ns = BLOCK_STATE
    rows = n_seq * n_tok
    w3 = lambda s1, s2: pl.BlockSpec((None, s1, s2), lambda o: (o, 0, 0))
    col = lambda r, c: pl.BlockSpec((r, c), lambda o: (0, o))
    return pl.pallas_call(
        functools.partial(_s5_sample_kernel, n_seq=n_seq, n_tok=n_tok),
        out_shape=(jax.ShapeDtypeStruct((rows, D_SSM), F32),
                   jax.ShapeDtypeStruct((n_seq, N_GROUPS * SSM_STATE), F32),
                   jax.ShapeDtypeStruct((n_seq, N_GROUPS * SSM_STATE), F32)),
        grid=(N_BLOCKS,),
        in_specs=[col(rows, LANES), col(n_seq, ns), col(n_seq, ns),
                  w3(LANES, 2 * ns), w3(2 * ns, LANES), w3(1, 2 * ns), col(1, LANES)],
        out_specs=(col(rows, LANES), col(n_seq, ns), col(n_seq, ns)),
        compiler_params=_params(("arbitrary",)),
        name="s5_sample",
    )(u, h0r, h0i, bmat, cmat, a1, d)


PAGES_PER_STEP = 16
PAGE_GROUP = 4

PAGE_KEYS = PAGE_SIZE * N_HEADS


def _decode_kernel(pt_ref, q_ref, lfrow_ref, lfcol_ref, kn_ref, vn_ref, *refs, n_tok, n_steps):
    g = PAGES_PER_STEP
    k_refs = refs[0:g]
    v_refs = refs[g:2 * g]
    lf_refs = refs[2 * g:3 * g]
    o_ref = refs[3 * g]
    m_sc, l_sc, acc_sc, carry_sc = refs[3 * g + 1:]
    j = pl.program_id(1)
    rows = n_tok * N_HEADS
    head_mask = N_HEADS - 1
    head_shift = N_HEADS.bit_length() - 1

    @pl.when(j == 0)
    def _():
        m_sc[...] = jnp.full(m_sc.shape, NEG_INF, F32)
        l_sc[...] = jnp.zeros_like(l_sc)
        acc_sc[...] = jnp.zeros_like(acc_sc)
        carry_sc[...] = jnp.zeros_like(carry_sc)

    q = q_ref[...]
    lfcol = lfcol_ref[...]
    run = jnp.zeros((N_HEADS, 1), F32)
    pieces = []
    for t in range(n_tok):
        run = run + lfcol[t * N_HEADS:(t + 1) * N_HEADS, :]
        pieces.append(run)
    c_col = jnp.concatenate(pieces, axis=0)

    def online(s_blocks, v_blocks):
        m = m_sc[...]
        smax = s_blocks[0]
        for s in s_blocks[1:]:
            smax = jnp.maximum(smax, s)
        m_new = jnp.maximum(m, _row_max(smax))
        alpha = jnp.exp2(m - m_new)
        psum = None
        pv = None
        for s, vblk in zip(s_blocks, v_blocks):
            p = jnp.exp2(s - m_new)
            psum = p if psum is None else psum + p
            d = jnp.dot(p.astype(BF16), vblk, preferred_element_type=F32)
            pv = d if pv is None else pv + d
        l_sc[...] = alpha * l_sc[...] + _row_sum(psum)
        acc_sc[...] = alpha * acc_sc[...] + pv
        m_sc[...] = m_new

    col = lax.broadcasted_iota(jnp.int32, (rows, PAGE_KEYS), 1)
    row = lax.broadcasted_iota(jnp.int32, (rows, PAGE_KEYS), 0)
    own = (col & head_mask) == (row & head_mask)
    fixed = jnp.where(own, c_col * LOG2E, NEG_INF)

    lf = jnp.concatenate([lf_refs[pg][...] for pg in range(g)], axis=0)
    lane = lax.broadcasted_iota(jnp.int32, (g, PAGE_KEYS), 1)
    suffix = lf
    total = lf
    sh = N_HEADS
    while sh < PAGE_KEYS:
        suffix = suffix + jnp.where(lane + sh < PAGE_KEYS, pltpu.roll(suffix, PAGE_KEYS - sh, axis=1), 0.0)
        total = total + pltpu.roll(total, PAGE_KEYS - sh, axis=1)
        sh *= 2
    carry = carry_sc[...]
    past = []
    for pg in range(g):
        past.append((suffix[pg:pg + 1, :] - lf[pg:pg + 1, :] + carry) * LOG2E)
        carry = carry + total[pg:pg + 1, :]
    carry_sc[...] = carry

    def logits(pages):
        out = []
        for pg in pages:
            kp = k_refs[pg][...].reshape(PAGE_KEYS, HEAD_DIM).astype(BF16)
            s = lax.dot_general(q, kp, _NT, preferred_element_type=F32)
            out.append(s + fixed + past[pg])
        return out

    def values(pages):
        return [v_refs[pg][...].reshape(PAGE_KEYS, HEAD_DIM).astype(BF16) for pg in pages]

    groups = [list(range(a, a + PAGE_GROUP)) for a in range(0, g, PAGE_GROUP)]
    s_next = logits(groups[0])
    for gi, pages in enumerate(groups):
        s_cur = s_next
        if gi + 1 < len(groups):
            s_next = logits(groups[gi + 1])
        online(s_cur, values(pages))

    @pl.when(j == n_steps - 1)
    def _():
        ln = lax.broadcasted_iota(jnp.int32, (1, LANES), 1)
        c_row = lfrow_ref[...]
        sh2 = N_HEADS
        while sh2 < rows:
            c_row = c_row + jnp.where(ln >= sh2, pltpu.roll(c_row, sh2, axis=1), 0.0)
            sh2 *= 2
        cl = lax.broadcasted_iota(jnp.int32, (rows, LANES), 1)
        rw = lax.broadcasted_iota(jnp.int32, (rows, LANES), 0)
        valid = ((cl < rows) & ((cl & head_mask) == (rw & head_mask))
                 & ((cl >> head_shift) <= (rw >> head_shift)))
        s = lax.dot_general(q, kn_ref[...], _NT, preferred_element_type=F32)
        s = jnp.where(valid, s + (c_col - c_row) * LOG2E, NEG_INF)
        online([s], [vn_ref[...]])
        o_ref[...] = acc_sc[...] / l_sc[...]


def _decode_attention(page_table, q, lfrow, lfcol, kn, vn, cache_k, cache_v, cache_lf, n_tok):
    n_seq, n_pages = page_table.shape
    g = PAGES_PER_STEP
    n_steps = n_pages // g
    rows = n_tok * N_HEADS

    def page5(pg):
        return lambda b, j, pt: (0, pt[b * n_pages + (n_pages - 1 - (j * g + pg))], 0, 0, 0)

    def page3(pg):
        return lambda b, j, pt: (pt[b * n_pages + (n_pages - 1 - (j * g + pg))], 0, 0)

    seq3 = lambda s1, s2: pl.BlockSpec((None, s1, s2), lambda b, j, pt: (b, 0, 0))
    kv_block = (None, None, PAGE_SIZE, N_HEADS, HEAD_DIM)
    in_specs = [seq3(rows, HEAD_DIM), seq3(1, LANES), seq3(rows, 1), seq3(LANES, HEAD_DIM), seq3(LANES, HEAD_DIM)]
    in_specs += [pl.BlockSpec(kv_block, page5(pg)) for pg in range(g)]
    in_specs += [pl.BlockSpec(kv_block, page5(pg)) for pg in range(g)]
    in_specs += [pl.BlockSpec((None, 1, PAGE_KEYS), page3(pg)) for pg in range(g)]
    grid_spec = pltpu.PrefetchScalarGridSpec(
        num_scalar_prefetch=1,
        grid=(n_seq, n_steps),
        in_specs=in_specs,
        out_specs=pl.BlockSpec((None, rows, HEAD_DIM), lambda b, j, pt: (b, 0, 0)),
        scratch_shapes=[pltpu.VMEM((rows, 1), F32),
                        pltpu.VMEM((rows, 1), F32),
                        pltpu.VMEM((rows, HEAD_DIM), F32),
                        pltpu.VMEM((1, PAGE_KEYS), F32)],
    )
    return pl.pallas_call(
        functools.partial(_decode_kernel, n_tok=n_tok, n_steps=n_steps),
        out_shape=jax.ShapeDtypeStruct((n_seq, rows, HEAD_DIM), F32),
        grid_spec=grid_spec,
        compiler_params=_params(("arbitrary", "arbitrary")),
        name="fox_sample",
    )(page_table.reshape(-1), q, lfrow, lfcol, kn, vn,
      *([cache_k] * g), *([cache_v] * g), *([cache_lf] * g))


def _merge_kernel(y_ref, ya_ref, gs_ref, ga_ref, gt_ref, x_ref, wglu_ref, bglu_ref, wbs_ref, wba_ref, wo_ref,
                  ng_ref, x1_ref, xn_ref):
    y = y_ref[...]
    gl = jnp.dot(y.astype(BF16), wglu_ref[...], preferred_element_type=F32) + bglu_ref[...]
    ys = (y * _sigmoid(gl)).astype(BF16)
    ms = jnp.dot(ys, wbs_ref[...], preferred_element_type=F32)
    ma = jnp.dot(ya_ref[...], wba_ref[...], preferred_element_type=F32)
    ga = ga_ref[...]
    gate_s = jnp.concatenate([gs_ref[...], ga[:, 0:LANES]], axis=1).astype(F32)
    gate_a = jnp.concatenate([ga, gt_ref[...]], axis=1).astype(F32)
    merged = gate_s * ms + gate_a * ma
    x1 = x_ref[...] + jnp.dot(merged.astype(BF16), wo_ref[...], preferred_element_type=F32)
    x1_ref[...] = x1
    ms1 = jnp.mean(x1 * x1, axis=-1, keepdims=True)
    xn_ref[...] = (x1 * lax.rsqrt(ms1 + RMS_EPS) * ng_ref[...]).astype(BF16)


MERGE_COLS = D_MODEL + LANES


def _merge(y, ya, gates, x, w_glu, b_glu, w_bs, w_ba, w_o, n_g, tm):
    t = x.shape[0]
    gate_s, gate_a, gate_t = gates
    row = lambda c: pl.BlockSpec((tm, c), lambda i: (i, 0))
    const = lambda r, c: pl.BlockSpec((r, c), lambda i: (0, 0), pipeline_mode=pl.Buffered(1))
    return pl.pallas_call(
        _merge_kernel,
        out_shape=(jax.ShapeDtypeStruct((t, D_MODEL), F32), jax.ShapeDtypeStruct((t, D_MODEL), BF16)),
        grid=(t // tm,),
        in_specs=[row(D_SSM), row(D_ATT), row(PROJ_COLS), row(PROJ_COLS), row(LANES), row(D_MODEL),
                  const(D_SSM, D_SSM), const(1, D_SSM), const(D_SSM, MERGE_COLS), const(D_ATT, MERGE_COLS),
                  const(MERGE_COLS, D_MODEL), const(1, D_MODEL)],
        out_specs=(row(D_MODEL), row(D_MODEL)),
        compiler_params=_params(("arbitrary",)),
        name="merge_out",
    )(y, ya, gate_s, gate_a, gate_t, x, w_glu, b_glu, w_bs, w_ba, w_o, n_g)


FFN_TF = 256


def _ffn_kernel(x1_ref, xn_ref, x1s_ref, xns_ref, wg_ref, wu_ref, wd_ref, o_ref, os_ref, wg_sc, wu_sc, wd_sc):
    i, f = pl.program_id(0), pl.program_id(1)

    @pl.when(f == 0)
    def _():
        o_ref[...] = x1_ref[...]

    xn = xn_ref[...]
    wg_sc[...] = wg_ref[...].astype(BF16)
    a = jnp.dot(xn, wg_sc[...], preferred_element_type=F32)
    wu_sc[...] = wu_ref[...].astype(BF16)
    b = jnp.dot(xn, wu_sc[...], preferred_element_type=F32)
    wd_sc[...] = wd_ref[...].astype(BF16)
    o_ref[...] += jnp.dot((a * _sigmoid(a) * b).astype(BF16), wd_sc[...], preferred_element_type=F32)

    @pl.when(i == pl.num_programs(0) - 1)
    def _():
        @pl.when(f == 0)
        def _():
            os_ref[...] = x1s_ref[...]

        xs = xns_ref[...]
        a_s = jnp.dot(xs, wg_sc[...], preferred_element_type=F32)
        b_s = jnp.dot(xs, wu_sc[...], preferred_element_type=F32)
        os_ref[...] += jnp.dot((a_s * _sigmoid(a_s) * b_s).astype(BF16), wd_sc[...], preferred_element_type=F32)


def _ffn(x1, xn, x1s, xns, w_g, w_u, w_d, tm):
    t = x1.shape[0]
    ts = x1s.shape[0]
    small = lambda: pl.BlockSpec((ts, D_MODEL), lambda i, f: (0, 0))
    return pl.pallas_call(
        _ffn_kernel,
        out_shape=(jax.ShapeDtypeStruct((t, D_MODEL), F32), jax.ShapeDtypeStruct((ts, D_MODEL), F32)),
        grid=(t // tm, D_FF // FFN_TF),
        in_specs=[pl.BlockSpec((tm, D_MODEL), lambda i, f: (i, 0)),
                  pl.BlockSpec((tm, D_MODEL), lambda i, f: (i, 0), pipeline_mode=pl.Buffered(1)),
                  small(), small(),
                  pl.BlockSpec((D_MODEL, FFN_TF), lambda i, f: (0, f)),
                  pl.BlockSpec((D_MODEL, FFN_TF), lambda i, f: (0, f)),
                  pl.BlockSpec((FFN_TF, D_MODEL), lambda i, f: (f, 0))],
        out_specs=(pl.BlockSpec((tm, D_MODEL), lambda i, f: (i, 0)), small()),
        scratch_shapes=[pltpu.VMEM((D_MODEL, FFN_TF), BF16), pltpu.VMEM((D_MODEL, FFN_TF), BF16),
                        pltpu.VMEM((FFN_TF, D_MODEL), BF16)],
        compiler_params=_params(("arbitrary", "arbitrary")),
        name="ffn",
    )(x1, xn, x1s, xns, w_g, w_u, w_d)


def _same_group_mask():
    chan_group = np.arange(LANES)[:, None] // SSM_GROUP
    state_group = np.arange(BLOCK_STATE)[None, :] // SSM_STATE
    return chan_group == state_group


def _block_diag_lanes(p):
    p4 = p.reshape(N_BLOCKS, GROUPS_PER_BLOCK, SSM_STATE, SSM_GROUP)
    rows = p4.transpose(0, 3, 1, 2).reshape(N_BLOCKS, SSM_GROUP, BLOCK_STATE)
    tiled = jnp.tile(rows, (1, GROUPS_PER_BLOCK, 1))
    return jnp.where(jnp.asarray(_same_group_mask()), tiled, 0.0)


def _block_diag_sublanes(p):
    p4 = p.reshape(N_BLOCKS, GROUPS_PER_BLOCK, SSM_GROUP, SSM_STATE)
    cols = p4.transpose(0, 1, 3, 2).reshape(N_BLOCKS, BLOCK_STATE, SSM_GROUP)
    tiled = jnp.tile(cols, (1, 1, GROUPS_PER_BLOCK))
    return jnp.where(jnp.asarray(_same_group_mask().T), tiled, 0.0)


def kernel(x_prompt, x_sample, cache_k, cache_v, cache_logf, state_ssm_re, state_ssm_im, page_table,
           norm_mix_g, w_in, b_f, q_norm_g, k_norm_g, ssm_a_re, ssm_a_im, ssm_log_dt, ssm_b_re, ssm_b_im,
           ssm_c_re, ssm_c_im, ssm_d, w_glu, b_glu, w_br_ssm, w_br_att, w_out, norm_ffn_g,
           w_ffn_gate, w_ffn_up, w_ffn_down):
    batch, seq, _ = x_prompt.shape
    n_seq, n_tok, _ = x_sample.shape
    assert w_in.shape[0] == 1, "single-layer trunk"
    l = 0
    n_qkv = D_SSM + 3 * D_ATT

    w_in_b = w_in[l].astype(BF16)
    n_slabs = w_in_b.shape[1] // PROJ_COLS
    n_tail = w_in_b.shape[1] - n_slabs * PROJ_COLS
    assert n_qkv == 2 * PROJ_COLS and n_tail == GATE_SHIFT == N_HEADS
    w_tail = jnp.pad(w_in_b[:, n_slabs * PROJ_COLS:], ((0, 0), (0, LANES - n_tail)))
    b_f_pad = jnp.pad(b_f[l], (0, LANES - N_HEADS)).reshape(1, LANES)
    norm_g = norm_mix_g[l].reshape(1, D_MODEL)
    q_g = q_norm_g[l].reshape(1, HEAD_DIM)
    k_g = k_norm_g[l].reshape(1, HEAD_DIM)
    ns = BLOCK_STATE
    ar_row = ssm_a_re[l].reshape(N_BLOCKS, 1, ns)
    ai_row = ssm_a_im[l].reshape(N_BLOCKS, 1, ns)
    dt_row = jnp.repeat(ssm_log_dt[l], SSM_STATE).reshape(N_BLOCKS, 1, ns)
    bd_bre = _block_diag_lanes(ssm_b_re[l])
    bd_bim = _block_diag_lanes(ssm_b_im[l])
    bd_cre = _block_diag_sublanes(ssm_c_re[l])
    bd_cim = _block_diag_sublanes(ssm_c_im[l])
    d_row = ssm_d[l].reshape(1, D_SSM)
    w_glu_b = w_glu[l].astype(BF16)
    b_glu_r = b_glu[l].reshape(1, D_SSM)
    shift_pad = (GATE_SHIFT, MERGE_COLS - D_MODEL - GATE_SHIFT)
    w_bs = jnp.pad(w_br_ssm[l].astype(BF16), ((0, 0), shift_pad))
    w_ba = jnp.pad(w_br_att[l].astype(BF16), ((0, 0), shift_pad))
    w_o = jnp.pad(w_out[l].astype(BF16), (shift_pad, (0, 0)))
    n_g = norm_ffn_g[l].reshape(1, D_MODEL)
    w_g = w_ffn_gate[l]
    w_u = w_ffn_up[l]
    w_d = w_ffn_down[l]

    vrev, bsum, m_op, bmat, cmat, a1, a16 = _s5gen(ar_row, ai_row, dt_row, bd_bre, bd_bim, bd_cre, bd_cim)

    xp = x_prompt.reshape(batch * seq, D_MODEL)
    rows = n_seq * n_tok
    xs = x_sample.transpose(1, 0, 2).reshape(rows, D_MODEL)
    ((u, qb, k, kb, v, vt, gates, logf),
     (us, qbs, ks, kbs, vs, vbs, gates_s, logf_s)) = _inproj(xp, xs, norm_g, w_in_b, w_tail, b_f_pad, q_g, k_g,
                                                             tm=PROJ_TM)
    caq, cak = _cumaug(logf, batch, seq)
    y_ssm, hre, him = _s5_prompt(u, vrev, bsum, m_op, a16, d_row, batch, seq)
    y_att = _attention(qb, caq, kb, cak, vt, batch, seq)
    x1, xn1 = _merge(y_ssm, y_att, gates, xp, w_glu_b, b_glu_r, w_bs, w_ba, w_o, n_g, tm=MERGE_TM)

    ys_ssm, hre_s, him_s = _s5_sample(us, state_ssm_re[l].reshape(n_seq, -1), state_ssm_im[l].reshape(n_seq, -1),
                                      bmat, cmat, a1, d_row, n_seq, n_tok)
    by_seq = lambda a: a.reshape(n_tok, n_seq, -1).transpose(1, 0, 2)
    th = n_tok * N_HEADS
    th_rows = lambda a: by_seq(a).reshape(n_seq, th, HEAD_DIM)
    pad_keys = lambda a: jnp.pad(th_rows(a), ((0, 0), (0, LANES - th), (0, 0)))
    lf_s = by_seq(logf_s[:, :N_HEADS]).reshape(n_seq, th)
    lfrow = jnp.pad(lf_s, ((0, 0), (0, LANES - th))).reshape(n_seq, 1, LANES)
    lfcol = lf_s.reshape(n_seq, th, 1)
    n_pool = cache_k.shape[1]
    ya_s = _decode_attention(page_table, th_rows(qbs), lfrow, lfcol, pad_keys(kbs), pad_keys(vbs),
                             cache_k, cache_v, cache_logf[l].reshape(n_pool, 1, PAGE_KEYS), n_tok)
    ya_s = ya_s.reshape(n_seq, n_tok, D_ATT).transpose(1, 0, 2).reshape(rows, D_ATT).astype(BF16)
    x1s, xn1s = _merge(ys_ssm, ya_s, gates_s, xs, w_glu_b, b_glu_r, w_bs, w_ba, w_o, n_g, tm=rows)
    y_p, y_s = _ffn(x1, xn1, x1s, xn1s, w_g, w_u, w_d, tm=FFN_TM)
    y_p = y_p.reshape(batch, seq, D_MODEL)
    y_s = y_s.reshape(n_tok, n_seq, D_MODEL).transpose(1, 0, 2)

    heads = lambda a, b_, t_: a.reshape(1, b_, t_, N_HEADS, HEAD_DIM)
    tok_major = lambda a: a.reshape(n_tok, n_seq, -1).transpose(1, 0, 2)
    return (
        y_p, y_s,
        heads(k, batch, seq), heads(v, batch, seq),
        logf[:, :N_HEADS].reshape(1, batch, seq, N_HEADS),
        hre.reshape(1, batch, N_GROUPS, SSM_STATE), him.reshape(1, batch, N_GROUPS, SSM_STATE),
        heads(tok_major(ks), n_seq, n_tok), heads(tok_major(vs), n_seq, n_tok),
        tok_major(logf_s[:, :N_HEADS]).reshape(1, n_seq, n_tok, N_HEADS),
        hre_s.reshape(1, n_seq, N_GROUPS, SSM_STATE), him_s.reshape(1, n_seq, N_GROUPS, SSM_STATE),
    )
```

```python
import functools
import math

import jax
import jax.numpy as jnp
import numpy as np
from jax import lax
from jax.experimental import pallas as pl
from jax.experimental.pallas import tpu as pltpu

F32 = jnp.float32
BF16 = jnp.bfloat16
HIGHEST = lax.Precision.HIGHEST

D_MODEL = 2048
D_SSM = D_MODEL // 2
SSM_GROUP = 16
N_GROUPS = D_SSM // SSM_GROUP
SSM_STATE = 64
HEAD_DIM = 128
N_HEADS = (D_MODEL // 2) // HEAD_DIM
D_ATT = N_HEADS * HEAD_DIM
D_FF = ((8 * D_MODEL + 3 * 256 - 1) // (3 * 256)) * 256
PAGE_SIZE = 128
RMS_EPS = 1e-6
NEG_INF = -1e30

LANES = 128
GROUPS_PER_BLOCK = LANES // SSM_GROUP
N_BLOCKS = N_GROUPS // GROUPS_PER_BLOCK
BLOCK_STATE = GROUPS_PER_BLOCK * SSM_STATE
CHUNK = 16
PROJ_COLS = 2048
GATE_SHIFT = N_HEADS
ATT_TK = 512
ATT_TQ = 2 * ATT_TK
VMEM_LIMIT = 56 * 1024 * 1024
PROJ_TM = 512
MERGE_TM = 256
FFN_TM = 1024

LOG2E = math.log2(math.e)
QK_SCALE_LOG2 = (HEAD_DIM ** -0.5) * LOG2E

_NT = (((1,), (1,)), ((), ()))


def _params(sem):
    return pltpu.CompilerParams(dimension_semantics=sem, vmem_limit_bytes=VMEM_LIMIT)


def _gelu_exact(x):
    return 0.5 * x * (1.0 + lax.erf(x * math.sqrt(0.5)))


def _sigmoid(x):
    return 1.0 / (1.0 + jnp.exp(-x))


def _fold_lanes(x, op):
    acc = x[:, 0:LANES]
    for c in range(1, x.shape[1] // LANES):
        acc = op(acc, x[:, c * LANES:(c + 1) * LANES])
    return acc


def _row_max(x):
    return jnp.max(_fold_lanes(x, jnp.maximum), axis=-1, keepdims=True)


def _row_sum(x):
    return jnp.sum(_fold_lanes(x, jnp.add), axis=-1, keepdims=True)


def _log_sigmoid(x):
    return jnp.minimum(x, 0.0) - jnp.log1p(jnp.exp(-jnp.abs(x)))


def _head_norm(z, gain):
    outs = []
    for h in range(N_HEADS):
        blk = z[:, h * HEAD_DIM:(h + 1) * HEAD_DIM]
        ms = jnp.mean(blk * blk, axis=-1, keepdims=True)
        outs.append(blk * lax.rsqrt(ms + RMS_EPS) * gain)
    return jnp.concatenate(outs, axis=1)


def _proj_kernel(*refs, kind, nt, v_transposed):
    za, zb, zs = refs[-3:]
    n_out = _PROJ_OUTPUTS[kind]
    in_p, in_s, w_ref = refs[0], refs[1], refs[2]
    extra = refs[3:len(refs) - 3 - 2 * n_out]
    out_p = refs[len(refs) - 3 - 2 * n_out:len(refs) - 3 - n_out]
    out_s = refs[len(refs) - 3 - n_out:len(refs) - 3]
    s = pl.program_id(0)
    half = PROJ_COLS // 2

    def group(in_ref, outs, transposed):
        if kind == "uq":
            g_ref, qg_ref = extra
            xn_ref, u_ref, qb_ref = outs

            def lhs():
                x = in_ref[...]
                ms = jnp.mean(x * x, axis=-1, keepdims=True)
                xn = (x * lax.rsqrt(ms + RMS_EPS) * g_ref[...]).astype(BF16)
                xn_ref[...] = xn
                return xn

            def epilogue(z):
                u_ref[...] = z[:, 0:half]
                qb_ref[...] = (_head_norm(z[:, half:], qg_ref[...]) * QK_SCALE_LOG2).astype(BF16)
        elif kind == "kv":
            (kg_ref,) = extra
            k_ref, kb_ref, v_ref, vb_ref = outs
            lhs = lambda: in_ref[...]

            def epilogue(z):
                kn = _head_norm(z[:, 0:half], kg_ref[...])
                k_ref[...] = kn
                kb_ref[...] = kn.astype(BF16)
                v = z[:, half:]
                v_ref[...] = v
                if transposed:
                    vt = v.T
                    for c in range(vb_ref.shape[0]):
                        vb_ref[c] = vt[:, c * ATT_TK:(c + 1) * ATT_TK].astype(BF16)
                else:
                    vb_ref[...] = v.astype(BF16)
        elif kind == "gate_ssm":
            (bf_ref,) = extra
            gate_ref, logf_ref = outs
            lhs = lambda: in_ref[...]

            def epilogue(z, c0=0):
                gate_ref[:, c0:c0 + z.shape[1]] = _sigmoid(z).astype(BF16)
                if c0 == 0:
                    logf_ref[...] = _log_sigmoid(z[:, 0:LANES] + bf_ref[...])
        else:
            (wt_ref,) = extra
            gate_ref, tail_ref = outs

            def lhs():
                xn = in_ref[...]
                tail_ref[...] = _sigmoid(jnp.dot(xn, wt_ref[...], preferred_element_type=F32)).astype(BF16)
                return xn

            def epilogue(z, c0=0):
                gate_ref[:, c0:c0 + z.shape[1]] = _sigmoid(z).astype(BF16)
        return lhs, epilogue

    lhs_p, epilogue_p = group(in_p, out_p, v_transposed)
    lhs_s, epilogue_s = group(in_s, out_s, False)

    @pl.when(s == 0)
    def _():
        zb[...] = jnp.zeros_like(zb)

    def body(z_write, z_read):
        if kind in ("gate_ssm", "gate_att"):
            xn = lhs_p()
            for c0 in (0, half):
                epilogue_p(z_read[:, c0:c0 + half], c0)
                z_write[:, c0:c0 + half] = jnp.dot(xn, w_ref[:, c0:c0 + half], preferred_element_type=F32)
        else:
            epilogue_p(z_read[...])
            z_write[...] = jnp.dot(lhs_p(), w_ref[...], preferred_element_type=F32)

    pl.when((s < nt) & (s % 2 == 0))(lambda: body(za, zb))
    pl.when((s < nt) & (s % 2 == 1))(lambda: body(zb, za))

    @pl.when(s == nt)
    def _():
        epilogue_p((za if (nt - 1) % 2 == 0 else zb)[...])
        zs[...] = jnp.dot(lhs_s(), w_ref[...], preferred_element_type=F32)

    @pl.when(s == nt + 1)
    def _():
        epilogue_s(zs[...])


_PROJ_OUTPUTS = {"uq": 3, "kv": 4, "gate_ssm": 2, "gate_att": 2}


def _inproj(x, xs, norm_g, w_in_b, w_tail, b_f, q_g, k_g, tm):
    t, ts = x.shape[0], xs.shape[0]
    nt = t // tm
    hi = nt - 1
    cur = lambda s: (jnp.minimum(s, hi), 0)
    lag = lambda s: (jnp.clip(s - 1, 0, hi), 0)
    lag3 = lambda s: (jnp.clip(s - 1, 0, hi), 0, 0)
    const = lambda s: (0, 0)
    sds = jax.ShapeDtypeStruct

    def call(kind, slab_index, row_inputs, extra, extra_specs, widths, dtypes, lagged, vt=False):
        in_width = row_inputs[0].shape[1]
        in_specs = [pl.BlockSpec((tm, in_width), cur), pl.BlockSpec((ts, in_width), const),
                    pl.BlockSpec((D_MODEL, PROJ_COLS), lambda s: (0, slab_index), pipeline_mode=pl.Buffered(1))]
        out_shape, out_specs = [], []
        for rows_n, block_n, prompt in ((t, tm, True), (ts, ts, False)):
            for w, dt, lg in zip(widths, dtypes, lagged):
                if vt and prompt and w is None:
                    out_shape.append(sds((t // ATT_TK, D_ATT, ATT_TK), BF16))
                    out_specs.append(pl.BlockSpec((tm // ATT_TK, D_ATT, ATT_TK), lag3))
                    continue
                w = D_ATT if w is None else w
                out_shape.append(sds((rows_n, w), dt))
                index = (lag if lg else cur) if prompt else const
                out_specs.append(pl.BlockSpec((block_n, w), index))
        outs = pl.pallas_call(
            functools.partial(_proj_kernel, kind=kind, nt=nt, v_transposed=vt),
            out_shape=tuple(out_shape),
            grid=(nt + 2,),
            in_specs=in_specs + extra_specs,
            out_specs=tuple(out_specs),
            scratch_shapes=[pltpu.VMEM((tm, PROJ_COLS), F32), pltpu.VMEM((tm, PROJ_COLS), F32),
                            pltpu.VMEM((ts, PROJ_COLS), F32)],
            compiler_params=_params(("arbitrary",)),
            name="proj_" + kind,
        )(*row_inputs, w_in_b, *extra)
        n = len(widths)
        return outs[:n], outs[n:]

    gain = lambda w: pl.BlockSpec((1, w), const)
    (xn, u, qb), (xns, us, qbs) = call(
        "uq", 0, (x, xs), (norm_g, q_g), [gain(D_MODEL), gain(HEAD_DIM)],
        (D_MODEL, D_SSM, D_ATT), (BF16, F32, BF16), (False, True, True))
    (k, kb, v, vb), (ks, kbs, vs, vbs) = call(
        "kv", 1, (xn, xns), (k_g,), [gain(HEAD_DIM)],
        (D_ATT, D_ATT, D_ATT, None), (F32, BF16, F32, BF16), (True, True, True, True), vt=True)
    (gate_s, logf), (gate_ss, logfs) = call(
        "gate_ssm", 2, (xn, xns), (b_f,), [gain(LANES)],
        (PROJ_COLS, LANES), (BF16, F32), (True, True))
    (gate_a, gate_t), (gate_as, gate_ts) = call(
        "gate_att", 3, (xn, xns), (w_tail,), [pl.BlockSpec((D_MODEL, LANES), const)],
        (PROJ_COLS, LANES), (BF16, BF16), (True, False))
    return ((u, qb, k, kb, v, vb, (gate_s, gate_a, gate_t), logf),
            (us, qbs, ks, kbs, vs, vbs, (gate_ss, gate_as, gate_ts), logfs))


CUM_ROWS = 256


def _cumaug_kernel(lf_ref, eq_ref, ek_ref, caq_ref, cak_ref, carry_sc):
    @pl.when(pl.program_id(1) == 0)
    def _():
        carry_sc[...] = jnp.zeros_like(carry_sc)

    lf = lf_ref[...]
    r = lax.broadcasted_iota(jnp.int32, (CUM_ROWS, CUM_ROWS), 0)
    c = lax.broadcasted_iota(jnp.int32, (CUM_ROWS, CUM_ROWS), 1)
    tri = (c <= r).astype(BF16)
    lf_hi = lf.astype(BF16)
    lf_r = lf - lf_hi.astype(F32)
    lf_mid = lf_r.astype(BF16)
    lf_lo = (lf_r - lf_mid.astype(F32)).astype(BF16)
    parts = jnp.dot(tri, jnp.concatenate([lf_hi, lf_mid, lf_lo], axis=1), preferred_element_type=F32)
    cs = (parts[:, 0:LANES] + parts[:, LANES:2 * LANES] + parts[:, 2 * LANES:3 * LANES]) + carry_sc[...]
    carry_sc[...] = cs[CUM_ROWS - 1:CUM_ROWS, :]
    cs2 = cs * LOG2E
    hi = cs2.astype(BF16).astype(F32)
    r1 = cs2 - hi
    mid = r1.astype(BF16).astype(F32)
    lo = (r1 - mid).astype(BF16).astype(F32)
    lane = lax.broadcasted_iota(jnp.int32, (CUM_ROWS, LANES), 1)
    packed = jnp.where(lane < N_HEADS, hi,
                       jnp.where(lane < 2 * N_HEADS, pltpu.roll(mid, N_HEADS, axis=1),
                                 jnp.where(lane < 3 * N_HEADS, pltpu.roll(lo, 2 * N_HEADS, axis=1),
                                           jnp.where(lane == 3 * N_HEADS, 1.0, 0.0)))).astype(BF16)
    caq_ref[...] = jnp.dot(packed, eq_ref[...], preferred_element_type=F32).astype(BF16)
    cak_ref[...] = jnp.dot(packed, ek_ref[...], preferred_element_type=F32).astype(BF16)


def _placement_matrices():
    eq = np.zeros((LANES, D_ATT), np.float32)
    ek = np.zeros((LANES, D_ATT), np.float32)
    for h in range(N_HEADS):
        base = h * HEAD_DIM
        for piece in range(3):
            eq[piece * N_HEADS + h, base + piece] = 1.0
            ek[piece * N_HEADS + h, base + 3 + piece] = -1.0
            eq[3 * N_HEADS, base + 3 + piece] = 1.0
            ek[3 * N_HEADS, base + piece] = 1.0
    return jnp.asarray(eq, BF16), jnp.asarray(ek, BF16)


def _cumaug(logf, batch, seq):
    nb = seq // CUM_ROWS
    eq, ek = _placement_matrices()
    const = lambda: pl.BlockSpec((LANES, D_ATT), lambda b, i: (0, 0))
    return pl.pallas_call(
        _cumaug_kernel,
        out_shape=(jax.ShapeDtypeStruct((batch * seq, D_ATT), BF16),
                   jax.ShapeDtypeStruct((batch * seq, D_ATT), BF16)),
        grid=(batch, nb),
        in_specs=[pl.BlockSpec((CUM_ROWS, LANES), lambda b, i: (b * nb + i, 0)), const(), const()],
        out_specs=(pl.BlockSpec((CUM_ROWS, D_ATT), lambda b, i: (b * nb + i, 0)),
                   pl.BlockSpec((CUM_ROWS, D_ATT), lambda b, i: (b * nb + i, 0))),
        scratch_shapes=[pltpu.VMEM((1, LANES), F32)],
        compiler_params=_params(("arbitrary", "arbitrary")),
        name="cumaug",
    )(logf, eq, ek)


BF16_SUBLANES = 16
ACC_ROWS = HEAD_DIM + BF16_SUBLANES


ATT_HEADS = 4


def _attn_kernel(q_ref, cq_ref, k_ref, ck_ref, vt_ref, o_ref, *scratch):
    i = pl.program_id(2)
    tk = ATT_TK
    heads = range(ATT_HEADS)
    m_sc, acc_sc, sa_sc, sb_sc = (scratch[0::4], scratch[1::4], scratch[2::4], scratch[3::4])
    for hd in heads:
        m_sc[hd][...] = jnp.full(m_sc[hd].shape, NEG_INF, F32)
        acc_sc[hd][...] = jnp.zeros_like(acc_sc[hd])
    sub = lax.broadcasted_iota(jnp.int32, (ACC_ROWS - HEAD_DIM, tk), 0)
    ones_rows = jnp.where(sub == 0, 1.0, 0.0).astype(BF16)

    def logits(hd, j, s_ref, lo):
        ks = pl.multiple_of(j * tk, tk)
        cols = slice(hd * HEAD_DIM, (hd + 1) * HEAD_DIM)
        kk = jnp.concatenate([k_ref[pl.ds(ks, tk), cols], ck_ref[pl.ds(ks, tk), cols]], axis=1)
        qq = jnp.concatenate([q_ref[lo:, cols], cq_ref[lo:, cols]], axis=1)
        s_ref[hd][:, lo:] = lax.dot_general(kk, qq, _NT, preferred_element_type=F32)

    def consume(hd, j, s_ref, lo, masked):
        s = s_ref[hd][:, lo:]
        if masked:
            key = lax.broadcasted_iota(jnp.int32, s.shape, 0)
            qry = lax.broadcasted_iota(jnp.int32, s.shape, 1)
            s = jnp.where(key <= qry, s, NEG_INF)
        vt = jnp.concatenate([vt_ref[j, hd * HEAD_DIM:(hd + 1) * HEAD_DIM, :], ones_rows], axis=0)
        m_old = m_sc[hd][:, lo:]
        m_new = jnp.maximum(m_old, jnp.max(s, axis=0, keepdims=True))
        alpha = jnp.exp2(m_old - m_new)
        p = jnp.exp2(s - m_new).astype(BF16)
        acc_sc[hd][:, lo:] = alpha * acc_sc[hd][:, lo:] + jnp.dot(vt, p, preferred_element_type=F32)
        m_sc[hd][:, lo:] = m_new

    for hd in heads:
        logits(hd, 0, sa_sc, 0)

    def pair(jj, carry):
        for hd in heads:
            logits(hd, 2 * jj + 1, sb_sc, 0)
            consume(hd, 2 * jj, sa_sc, 0, False)
        for hd in heads:
            logits(hd, 2 * jj + 2, sa_sc, 0)
            consume(hd, 2 * jj + 1, sb_sc, 0, False)
        return carry

    lax.fori_loop(0, i, pair, 0)
    for hd in heads:
        logits(hd, 2 * i + 1, sb_sc, tk)
        consume(hd, 2 * i, sa_sc, 0, True)
    for hd in heads:
        consume(hd, 2 * i + 1, sb_sc, tk, True)
        acc = acc_sc[hd][...]
        out_t = acc[0:HEAD_DIM, :] / acc[HEAD_DIM:HEAD_DIM + 1, :]
        o_ref[:, hd * HEAD_DIM:(hd + 1) * HEAD_DIM] = out_t.T.astype(BF16)


def _attention(qb, caq, kb, cak, vt, batch, seq):
    nq = seq // ATT_TQ
    nkb = seq // ATT_TK
    width = ATT_HEADS * HEAD_DIM
    qspec = lambda: pl.BlockSpec((ATT_TQ, width), lambda b, h, i: (b * nq + i, h))
    kspec = lambda: pl.BlockSpec((seq, width), lambda b, h, i: (b, h))
    per_head = [pltpu.VMEM((1, ATT_TQ), F32), pltpu.VMEM((ACC_ROWS, ATT_TQ), F32),
                pltpu.VMEM((ATT_TK, ATT_TQ), F32), pltpu.VMEM((ATT_TK, ATT_TQ), F32)]
    return pl.pallas_call(
        _attn_kernel,
        out_shape=jax.ShapeDtypeStruct((batch * seq, D_ATT), BF16),
        grid=(batch, N_HEADS // ATT_HEADS, nq),
        in_specs=[qspec(), qspec(), kspec(), kspec(),
                  pl.BlockSpec((nkb, width, ATT_TK), lambda b, h, i: (b, h, 0))],
        out_specs=qspec(),
        scratch_shapes=per_head * ATT_HEADS,
        compiler_params=_params(("arbitrary", "arbitrary", "arbitrary")),
        name="fox_prompt",
    )(qb, caq, kb, cak, vt)


def _discretise(a_re, a_im, log_dt):
    dt = jnp.exp(log_dt)
    mag = jnp.exp(dt * a_re)
    ang = dt * a_im
    abr = mag * jnp.cos(ang)
    abi = mag * jnp.sin(ang)
    e_re = abr - 1.0
    e_im = abi
    inv_den = 1.0 / (a_re * a_re + a_im * a_im)
    f_re = (e_re * a_re + e_im * a_im) * inv_den
    f_im = (e_im * a_re - e_re * a_im) * inv_den
    return abr, abi, f_re, f_im


def _powers(abr, abi, n):
    pr, pi = [jnp.ones_like(abr)], [jnp.zeros_like(abi)]
    for _ in range(n):
        r, i = pr[-1], pi[-1]
        pr.append(r * abr - i * abi)
        pi.append(r * abi + i * abr)
    return pr, pi


def _s5gen_kernel(ar_row, ai_row, dt_row, bre_ref, bim_ref, cre_ref, cim_ref,
                  vrev_ref, bsum_ref, m_ref, bmat_ref, cmat_ref, a1_ref, a16_ref):
    ns = BLOCK_STATE
    abr, abi, f_re, f_im = _discretise(ar_row[...], ai_row[...], dt_row[...])
    bre, bim = bre_ref[...], bim_ref[...]
    bbr = f_re * bre - f_im * bim
    bbi = f_re * bim + f_im * bre
    bmat = jnp.concatenate([bbr, bbi], axis=1)
    bmat_ref[...] = bmat
    pr, pi = _powers(abr, abi, CHUNK)
    a1_ref[...] = jnp.concatenate([pr[1], pi[1]], axis=1)
    a16_ref[...] = jnp.concatenate([pr[CHUNK], pi[CHUNK]], axis=1)
    for i in range(CHUNK):
        r, im = pr[CHUNK - 1 - i], pi[CHUNK - 1 - i]
        blk = jnp.concatenate([r * bbr - im * bbi, r * bbi + im * bbr], axis=1)
        bsum_ref[i * LANES:(i + 1) * LANES, :] = blk.astype(BF16)
    npow = CHUNK + 1
    stacked = jnp.concatenate(pr + pi + [jnp.zeros((LANES - 2 * npow, ns), F32)], axis=0)
    pt = stacked.T
    qr = [pt[:, t:t + 1] for t in range(npow)]
    qi = [pt[:, npow + t:npow + t + 1] for t in range(npow)]
    cre, cim = cre_ref[...], cim_ref[...]
    vrev_ref[(CHUNK - 1) * LANES:CHUNK * LANES, 0:LANES] = jnp.zeros((LANES, LANES), BF16)
    bmat_hi = bmat.astype(BF16)
    for tau in range(CHUNK + 1):
        blk = jnp.concatenate([cre * qr[tau] - cim * qi[tau],
                               -(cre * qi[tau] + cim * qr[tau])], axis=0)
        if tau == 0:
            cmat_ref[...] = blk
        else:
            m_ref[:, (tau - 1) * LANES:tau * LANES] = blk.astype(BF16)
        if tau < CHUNK:
            w = jnp.dot(bmat_hi, blk.astype(BF16), preferred_element_type=F32).astype(BF16)
            k = CHUNK - 1 - tau
            vrev_ref[k * LANES:(k + 1) * LANES, LANES:2 * LANES] = w
            if k >= 1:
                vrev_ref[(k - 1) * LANES:k * LANES, 0:LANES] = w


def _s5gen(ar_row, ai_row, dt_row, bd_bre, bd_bim, bd_cre, bd_cim):
    ns = BLOCK_STATE
    b3 = lambda s1, s2: pl.BlockSpec((None, s1, s2), lambda o: (o, 0, 0))
    out_shape = (
        jax.ShapeDtypeStruct((N_BLOCKS, CHUNK * LANES, 2 * LANES), BF16),
        jax.ShapeDtypeStruct((N_BLOCKS, CHUNK * LANES, 2 * ns), BF16),
        jax.ShapeDtypeStruct((N_BLOCKS, 2 * ns, CHUNK * LANES), BF16),
        jax.ShapeDtypeStruct((N_BLOCKS, LANES, 2 * ns), F32),
        jax.ShapeDtypeStruct((N_BLOCKS, 2 * ns, LANES), F32),
        jax.ShapeDtypeStruct((N_BLOCKS, 1, 2 * ns), F32),
        jax.ShapeDtypeStruct((N_BLOCKS, 1, 2 * ns), F32),
    )
    return pl.pallas_call(
        _s5gen_kernel,
        out_shape=out_shape,
        grid=(N_BLOCKS,),
        in_specs=[b3(1, ns), b3(1, ns), b3(1, ns),
                  b3(LANES, ns), b3(LANES, ns), b3(ns, LANES), b3(ns, LANES)],
        out_specs=(b3(CHUNK * LANES, 2 * LANES), b3(CHUNK * LANES, 2 * ns), b3(2 * ns, CHUNK * LANES),
                   b3(LANES, 2 * ns), b3(2 * ns, LANES), b3(1, 2 * ns), b3(1, 2 * ns)),
        compiler_params=_params(("arbitrary",)),
        name="s5_operators",
    )(ar_row, ai_row, dt_row, bd_bre, bd_bim, bd_cre, bd_cim)


def _s5_prompt_kernel(u_ref, vrev_ref, bsum_ref, m_ref, a16_ref, d_ref,
                      y_ref, hre_ref, him_ref, ucat_sc, s_sc, hin_sc, yt_sc, *, n_chunks):
    ns = BLOCK_STATE
    for i in range(CHUNK):
        ucat_sc[:, i * LANES:(i + 1) * LANES] = u_ref[pl.ds(i, n_chunks, stride=CHUNK), :].astype(BF16)
    s_sc[...] = jnp.dot(ucat_sc[...], bsum_ref[...], preferred_element_type=F32)
    ar = a16_ref[:, 0:ns]
    ai = a16_ref[:, ns:2 * ns]
    for jp in range(CHUNK // 2):
        j = 2 * jp
        kk = (j + 2) * LANES
        yt_sc[:, j * LANES:(j + 2) * LANES] = jnp.dot(
            ucat_sc[:, 0:kk], vrev_ref[(CHUNK - 2 - j) * LANES:, :], preferred_element_type=F32)

    hr = jnp.zeros((1, ns), F32)
    hi = jnp.zeros((1, ns), F32)
    for k in range(n_chunks):
        hin_sc[k:k + 1, 0:ns] = hr
        hin_sc[k:k + 1, ns:2 * ns] = hi
        sr = s_sc[k:k + 1, 0:ns]
        si = s_sc[k:k + 1, ns:2 * ns]
        hr, hi = ar * hr - ai * hi + sr, ar * hi + ai * hr + si
    hre_ref[...] = hr
    him_ref[...] = hi
    hin = hin_sc[...].astype(BF16)
    d = d_ref[...]
    for jp in range(CHUNK // 2):
        j = 2 * jp
        acc = yt_sc[:, j * LANES:(j + 2) * LANES] + jnp.dot(
            hin, m_ref[:, j * LANES:(j + 2) * LANES], preferred_element_type=F32)
        for jj in range(2):
            uj = u_ref[pl.ds(j + jj, n_chunks, stride=CHUNK), :]
            val = acc[:, jj * LANES:(jj + 1) * LANES] + d * uj
            y_ref[pl.ds(j + jj, n_chunks, stride=CHUNK), :] = _gelu_exact(val)


def _s5_prompt(u, vrev, bsum, m, a16, d, batch, seq):
    ns = BLOCK_STATE
    n_chunks = seq // CHUNK
    w3 = lambda s1, s2: pl.BlockSpec((None, s1, s2), lambda o, b: (o, 0, 0))
    return pl.pallas_call(
        functools.partial(_s5_prompt_kernel, n_chunks=n_chunks),
        out_shape=(jax.ShapeDtypeStruct((batch * seq, D_SSM), F32),
                   jax.ShapeDtypeStruct((batch, 1, N_GROUPS * SSM_STATE), F32),
                   jax.ShapeDtypeStruct((batch, 1, N_GROUPS * SSM_STATE), F32)),
        grid=(N_BLOCKS, batch),
        in_specs=[pl.BlockSpec((seq, LANES), lambda o, b: (b, o)),
                  w3(CHUNK * LANES, 2 * LANES), w3(CHUNK * LANES, 2 * ns), w3(2 * ns, CHUNK * LANES),
                  w3(1, 2 * ns),
                  pl.BlockSpec((1, LANES), lambda o, b: (0, o))],
        out_specs=(pl.BlockSpec((seq, LANES), lambda o, b: (b, o)),
                   pl.BlockSpec((None, 1, ns), lambda o, b: (b, 0, o)),
                   pl.BlockSpec((None, 1, ns), lambda o, b: (b, 0, o))),
        scratch_shapes=[pltpu.VMEM((n_chunks, CHUNK * LANES), BF16),
                        pltpu.VMEM((n_chunks, 2 * ns), F32),
                        pltpu.VMEM((n_chunks, 2 * ns), F32),
                        pltpu.VMEM((n_chunks, CHUNK * LANES), F32)],
        compiler_params=_params(("arbitrary", "arbitrary")),
        name="s5_prompt",
    )(u, vrev, bsum, m, a16, d)


def _s5_sample_kernel(u_ref, h0r_ref, h0i_ref, bmat_ref, cmat_ref, a1_ref, d_ref,
                      y_ref, hre_ref, him_ref, *, n_seq, n_tok):
    ns = BLOCK_STATE
    u = u_ref[...]
    bu = jnp.dot(u, bmat_ref[...], precision=HIGHEST, preferred_element_type=F32)
    ar = a1_ref[:, 0:ns]
    ai = a1_ref[:, ns:2 * ns]
    hr, hi = h0r_ref[...], h0i_ref[...]
    hs = []
    for t in range(n_tok):
        br = bu[t * n_seq:(t + 1) * n_seq, 0:ns]
        bi = bu[t * n_seq:(t + 1) * n_seq, ns:2 * ns]
        hr, hi = ar * hr - ai * hi + br, ar * hi + ai * hr + bi
        hs.append(jnp.concatenate([hr, hi], axis=1))
    hcat = jnp.concatenate(hs, axis=0)
    val = jnp.dot(hcat, cmat_ref[...], precision=HIGHEST, preferred_element_type=F32) + d_ref[...] * u
    y_ref[...] = _gelu_exact(val)
    hre_ref[...] = hr
    him_ref[...] = hi


def _s5_sample(u, h0r, h0i, bmat, cmat, a1, d, n_seq, n_tok):
    ns = BLOCK_STATE
    rows = n_seq * n_tok
    w3 = lambda s1, s2: pl.BlockSpec((None, s1, s2), lambda o: (o, 0, 0))
    col = lambda r, c: pl.BlockSpec((r, c), lambda o: (0, o))
    return pl.pallas_call(
        functools.partial(_s5_sample_kernel, n_seq=n_seq, n_tok=n_tok),
        out_shape=(jax.ShapeDtypeStruct((rows, D_SSM), F32),
                   jax.ShapeDtypeStruct((n_seq, N_GROUPS * SSM_STATE), F32),
                   jax.ShapeDtypeStruct((n_seq, N_GROUPS * SSM_STATE), F32)),
        grid=(N_BLOCKS,),
        in_specs=[col(rows, LANES), col(n_seq, ns), col(n_seq, ns),
                  w3(LANES, 2 * ns), w3(2 * ns, LANES), w3(1, 2 * ns), col(1, LANES)],
        out_specs=(col(rows, LANES), col(n_seq, ns), col(n_seq, ns)),
        compiler_params=_params(("arbitrary",)),
        name="s5_sample",
    )(u, h0r, h0i, bmat, cmat, a1, d)


PAGES_PER_STEP = 16
PAGE_GROUP = 4

PAGE_KEYS = PAGE_SIZE * N_HEADS


def _decode_kernel(pt_ref, q_ref, lfrow_ref, lfcol_ref, kn_ref, vn_ref, *refs, n_tok, n_steps):
    g = PAGES_PER_STEP
    k_refs = refs[0:g]
    v_refs = refs[g:2 * g]
    lf_refs = refs[2 * g:3 * g]
    o_ref = refs[3 * g]
    m_sc, l_sc, acc_sc, carry_sc = refs[3 * g + 1:]
    j = pl.program_id(1)
    rows = n_tok * N_HEADS
    head_mask = N_HEADS - 1
    head_shift = N_HEADS.bit_length() - 1

    @pl.when(j == 0)
    def _():
        m_sc[...] = jnp.full(m_sc.shape, NEG_INF, F32)
        l_sc[...] = jnp.zeros_like(l_sc)
        acc_sc[...] = jnp.zeros_like(acc_sc)
        carry_sc[...] = jnp.zeros_like(carry_sc)

    q = q_ref[...]
    lfcol = lfcol_ref[...]
    run = jnp.zeros((N_HEADS, 1), F32)
    pieces = []
    for t in range(n_tok):
        run = run + lfcol[t * N_HEADS:(t + 1) * N_HEADS, :]
        pieces.append(run)
    c_col = jnp.concatenate(pieces, axis=0)

    def online(s_blocks, v_blocks):
        m = m_sc[...]
        smax = s_blocks[0]
        for s in s_blocks[1:]:
            smax = jnp.maximum(smax, s)
        m_new = jnp.maximum(m, _row_max(smax))
        alpha = jnp.exp2(m - m_new)
        psum = None
        pv = None
        for s, vblk in zip(s_blocks, v_blocks):
            p = jnp.exp2(s - m_new)
            psum = p if psum is None else psum + p
            d = jnp.dot(p.astype(BF16), vblk, preferred_element_type=F32)
            pv = d if pv is None else pv + d
        l_sc[...] = alpha * l_sc[...] + _row_sum(psum)
        acc_sc[...] = alpha * acc_sc[...] + pv
        m_sc[...] = m_new

    col = lax.broadcasted_iota(jnp.int32, (rows, PAGE_KEYS), 1)
    row = lax.broadcasted_iota(jnp.int32, (rows, PAGE_KEYS), 0)
    own = (col & head_mask) == (row & head_mask)
    fixed = jnp.where(own, c_col * LOG2E, NEG_INF)

    lf = jnp.concatenate([lf_refs[pg][...] for pg in range(g)], axis=0)
    lane = lax.broadcasted_iota(jnp.int32, (g, PAGE_KEYS), 1)
    suffix = lf
    total = lf
    sh = N_HEADS
    while sh < PAGE_KEYS:
        suffix = suffix + jnp.where(lane + sh < PAGE_KEYS, pltpu.roll(suffix, PAGE_KEYS - sh, axis=1), 0.0)
        total = total + pltpu.roll(total, PAGE_KEYS - sh, axis=1)
        sh *= 2
    carry = carry_sc[...]
    past = []
    for pg in range(g):
        past.append((suffix[pg:pg + 1, :] - lf[pg:pg + 1, :] + carry) * LOG2E)
        carry = carry + total[pg:pg + 1, :]
    carry_sc[...] = carry

    def logits(pages):
        out = []
        for pg in pages:
            kp = k_refs[pg][...].reshape(PAGE_KEYS, HEAD_DIM).astype(BF16)
            s = lax.dot_general(q, kp, _NT, preferred_element_type=F32)
            out.append(s + fixed + past[pg])
        return out

    def values(pages):
        return [v_refs[pg][...].reshape(PAGE_KEYS, HEAD_DIM).astype(BF16) for pg in pages]

    groups = [list(range(a, a + PAGE_GROUP)) for a in range(0, g, PAGE_GROUP)]
    s_next = logits(groups[0])
    for gi, pages in enumerate(groups):
        s_cur = s_next
        if gi + 1 < len(groups):
            s_next = logits(groups[gi + 1])
        online(s_cur, values(pages))

    @pl.when(j == n_steps - 1)
    def _():
        ln = lax.broadcasted_iota(jnp.int32, (1, LANES), 1)
        c_row = lfrow_ref[...]
        sh2 = N_HEADS
        while sh2 < rows:
            c_row = c_row + jnp.where(ln >= sh2, pltpu.roll(c_row, sh2, axis=1), 0.0)
            sh2 *= 2
        cl = lax.broadcasted_iota(jnp.int32, (rows, LANES), 1)
        rw = lax.broadcasted_iota(jnp.int32, (rows, LANES), 0)
        valid = ((cl < rows) & ((cl & head_mask) == (rw & head_mask))
                 & ((cl >> head_shift) <= (rw >> head_shift)))
        s = lax.dot_general(q, kn_ref[...], _NT, preferred_element_type=F32)
        s = jnp.where(valid, s + (c_col - c_row) * LOG2E, NEG_INF)
        online([s], [vn_ref[...]])
        o_ref[...] = acc_sc[...] / l_sc[...]


def _decode_attention(page_table, q, lfrow, lfcol, kn, vn, cache_k, cache_v, cache_lf, n_tok):
    n_seq, n_pages = page_table.shape
    g = PAGES_PER_STEP
    n_steps = n_pages // g
    rows = n_tok * N_HEADS

    def page5(pg):
        return lambda b, j, pt: (0, pt[b * n_pages + (n_pages - 1 - (j * g + pg))], 0, 0, 0)

    def page3(pg):
        return lambda b, j, pt: (pt[b * n_pages + (n_pages - 1 - (j * g + pg))], 0, 0)

    seq3 = lambda s1, s2: pl.BlockSpec((None, s1, s2), lambda b, j, pt: (b, 0, 0))
    kv_block = (None, None, PAGE_SIZE, N_HEADS, HEAD_DIM)
    in_specs = [seq3(rows, HEAD_DIM), seq3(1, LANES), seq3(rows, 1), seq3(LANES, HEAD_DIM), seq3(LANES, HEAD_DIM)]
    in_specs += [pl.BlockSpec(kv_block, page5(pg)) for pg in range(g)]
    in_specs += [pl.BlockSpec(kv_block, page5(pg)) for pg in range(g)]
    in_specs += [pl.BlockSpec((None, 1, PAGE_KEYS), page3(pg)) for pg in range(g)]
    grid_spec = pltpu.PrefetchScalarGridSpec(
        num_scalar_prefetch=1,
        grid=(n_seq, n_steps),
        in_specs=in_specs,
        out_specs=pl.BlockSpec((None, rows, HEAD_DIM), lambda b, j, pt: (b, 0, 0)),
        scratch_shapes=[pltpu.VMEM((rows, 1), F32),
                        pltpu.VMEM((rows, 1), F32),
                        pltpu.VMEM((rows, HEAD_DIM), F32),
                        pltpu.VMEM((1, PAGE_KEYS), F32)],
    )
    return pl.pallas_call(
        functools.partial(_decode_kernel, n_tok=n_tok, n_steps=n_steps),
        out_shape=jax.ShapeDtypeStruct((n_seq, rows, HEAD_DIM), F32),
        grid_spec=grid_spec,
        compiler_params=_params(("arbitrary", "arbitrary")),
        name="fox_sample",
    )(page_table.reshape(-1), q, lfrow, lfcol, kn, vn,
      *([cache_k] * g), *([cache_v] * g), *([cache_lf] * g))


def _merge_kernel(*refs, nt):
    rows_p, rows_s = refs[0:6], refs[6:12]
    wglu_ref, bglu_ref, wbs_ref, wba_ref, wo_ref, ng_ref = refs[12:18]
    outs_p, outs_s = refs[18:20], refs[20:22]

    def rows(y_ref, ya_ref, gs_ref, ga_ref, gt_ref, x_ref, x1_ref, xn_ref):
        y = y_ref[...]
        gl = jnp.dot(y.astype(BF16), wglu_ref[...], preferred_element_type=F32) + bglu_ref[...]
        ys = (y * _sigmoid(gl)).astype(BF16)
        ms = jnp.dot(ys, wbs_ref[...], preferred_element_type=F32)
        ma = jnp.dot(ya_ref[...], wba_ref[...], preferred_element_type=F32)
        ga = ga_ref[...]
        gate_s = jnp.concatenate([gs_ref[...], ga[:, 0:LANES]], axis=1).astype(F32)
        gate_a = jnp.concatenate([ga, gt_ref[...]], axis=1).astype(F32)
        merged = gate_s * ms + gate_a * ma
        x1 = x_ref[...] + jnp.dot(merged.astype(BF16), wo_ref[...], preferred_element_type=F32)
        x1_ref[...] = x1
        ms1 = jnp.mean(x1 * x1, axis=-1, keepdims=True)
        xn_ref[...] = (x1 * lax.rsqrt(ms1 + RMS_EPS) * ng_ref[...]).astype(BF16)

    i = pl.program_id(0)
    pl.when(i < nt)(lambda: rows(*rows_p, *outs_p))
    pl.when(i == nt)(lambda: rows(*rows_s, *outs_s))


MERGE_COLS = D_MODEL + LANES


def _merge(prompt, sample, w_glu, b_glu, w_bs, w_ba, w_o, n_g, tm):
    flat = lambda g: (g[0], g[1], *g[2], g[3])
    t, ts = prompt[3].shape[0], sample[3].shape[0]
    nt = t // tm
    widths = (D_SSM, D_ATT, PROJ_COLS, PROJ_COLS, LANES, D_MODEL)
    tile = lambda c: pl.BlockSpec((tm, c), lambda i: (jnp.minimum(i, nt - 1), 0))
    small = lambda c: pl.BlockSpec((ts, c), lambda i: (0, 0))
    const = lambda r, c: pl.BlockSpec((r, c), lambda i: (0, 0), pipeline_mode=pl.Buffered(1))
    sds = jax.ShapeDtypeStruct
    return pl.pallas_call(
        functools.partial(_merge_kernel, nt=nt),
        out_shape=(sds((t, D_MODEL), F32), sds((t, D_MODEL), BF16), sds((ts, D_MODEL), F32), sds((ts, D_MODEL), BF16)),
        grid=(nt + 1,),
        in_specs=[tile(c) for c in widths] + [small(c) for c in widths]
        + [const(D_SSM, D_SSM), const(1, D_SSM), const(D_SSM, MERGE_COLS), const(D_ATT, MERGE_COLS),
           const(MERGE_COLS, D_MODEL), const(1, D_MODEL)],
        out_specs=(tile(D_MODEL), tile(D_MODEL), small(D_MODEL), small(D_MODEL)),
        compiler_params=_params(("arbitrary",)),
        name="merge_out",
    )(*flat(prompt), *flat(sample), w_glu, b_glu, w_bs, w_ba, w_o, n_g)


FFN_TF = 256


def _ffn_kernel(x1_ref, xn_ref, x1s_ref, xns_ref, wg_ref, wu_ref, wd_ref, o_ref, os_ref, wg_sc, wu_sc, wd_sc):
    i, f = pl.program_id(0), pl.program_id(1)

    @pl.when(f == 0)
    def _():
        o_ref[...] = x1_ref[...]

    xn = xn_ref[...]
    wg_sc[...] = wg_ref[...].astype(BF16)
    a = jnp.dot(xn, wg_sc[...], preferred_element_type=F32)
    wu_sc[...] = wu_ref[...].astype(BF16)
    b = jnp.dot(xn, wu_sc[...], preferred_element_type=F32)
    wd_sc[...] = wd_ref[...].astype(BF16)
    o_ref[...] += jnp.dot((a * _sigmoid(a) * b).astype(BF16), wd_sc[...], preferred_element_type=F32)

    @pl.when(i == pl.num_programs(0) - 1)
    def _():
        @pl.when(f == 0)
        def _():
            os_ref[...] = x1s_ref[...]

        xs = xns_ref[...]
        a_s = jnp.dot(xs, wg_sc[...], preferred_element_type=F32)
        b_s = jnp.dot(xs, wu_sc[...], preferred_element_type=F32)
        os_ref[...] += jnp.dot((a_s * _sigmoid(a_s) * b_s).astype(BF16), wd_sc[...], preferred_element_type=F32)


def _ffn(x1, xn, x1s, xns, w_g, w_u, w_d, tm):
    t = x1.shape[0]
    ts = x1s.shape[0]
    small = lambda: pl.BlockSpec((ts, D_MODEL), lambda i, f: (0, 0))
    return pl.pallas_call(
        _ffn_kernel,
        out_shape=(jax.ShapeDtypeStruct((t, D_MODEL), F32), jax.ShapeDtypeStruct((ts, D_MODEL), F32)),
        grid=(t // tm, D_FF // FFN_TF),
        in_specs=[pl.BlockSpec((tm, D_MODEL), lambda i, f: (i, 0)),
                  pl.BlockSpec((tm, D_MODEL), lambda i, f: (i, 0), pipeline_mode=pl.Buffered(1)),
                  small(), small(),
                  pl.BlockSpec((D_MODEL, FFN_TF), lambda i, f: (0, f)),
                  pl.BlockSpec((D_MODEL, FFN_TF), lambda i, f: (0, f)),
                  pl.BlockSpec((FFN_TF, D_MODEL), lambda i, f: (f, 0))],
        out_specs=(pl.BlockSpec((tm, D_MODEL), lambda i, f: (i, 0)), small()),
        scratch_shapes=[pltpu.VMEM((D_MODEL, FFN_TF), BF16), pltpu.VMEM((D_MODEL, FFN_TF), BF16),
                        pltpu.VMEM((FFN_TF, D_MODEL), BF16)],
        compiler_params=_params(("arbitrary", "arbitrary")),
        name="ffn",
    )(x1, xn, x1s, xns, w_g, w_u, w_d)


def _same_group_mask():
    chan_group = np.arange(LANES)[:, None] // SSM_GROUP
    state_group = np.arange(BLOCK_STATE)[None, :] // SSM_STATE
    return chan_group == state_group


def _block_diag_lanes(p):
    p4 = p.reshape(N_BLOCKS, GROUPS_PER_BLOCK, SSM_STATE, SSM_GROUP)
    rows = p4.transpose(0, 3, 1, 2).reshape(N_BLOCKS, SSM_GROUP, BLOCK_STATE)
    tiled = jnp.tile(rows, (1, GROUPS_PER_BLOCK, 1))
    return jnp.where(jnp.asarray(_same_group_mask()), tiled, 0.0)


def _block_diag_sublanes(p):
    p4 = p.reshape(N_BLOCKS, GROUPS_PER_BLOCK, SSM_GROUP, SSM_STATE)
    cols = p4.transpose(0, 1, 3, 2).reshape(N_BLOCKS, BLOCK_STATE, SSM_GROUP)
    tiled = jnp.tile(cols, (1, 1, GROUPS_PER_BLOCK))
    return jnp.where(jnp.asarray(_same_group_mask().T), tiled, 0.0)


def kernel(x_prompt, x_sample, cache_k, cache_v, cache_logf, state_ssm_re, state_ssm_im, page_table,
           norm_mix_g, w_in, b_f, q_norm_g, k_norm_g, ssm_a_re, ssm_a_im, ssm_log_dt, ssm_b_re, ssm_b_im,
           ssm_c_re, ssm_c_im, ssm_d, w_glu, b_glu, w_br_ssm, w_br_att, w_out, norm_ffn_g,
           w_ffn_gate, w_ffn_up, w_ffn_down):
    batch, seq, _ = x_prompt.shape
    n_seq, n_tok, _ = x_sample.shape
    assert w_in.shape[0] == 1, "single-layer trunk"
    l = 0
    n_qkv = D_SSM + 3 * D_ATT

    w_in_b = w_in[l].astype(BF16)
    n_slabs = w_in_b.shape[1] // PROJ_COLS
    n_tail = w_in_b.shape[1] - n_slabs * PROJ_COLS
    assert n_qkv == 2 * PROJ_COLS and n_tail == GATE_SHIFT == N_HEADS
    w_tail = jnp.pad(w_in_b[:, n_slabs * PROJ_COLS:], ((0, 0), (0, LANES - n_tail)))
    b_f_pad = jnp.pad(b_f[l], (0, LANES - N_HEADS)).reshape(1, LANES)
    norm_g = norm_mix_g[l].reshape(1, D_MODEL)
    q_g = q_norm_g[l].reshape(1, HEAD_DIM)
    k_g = k_norm_g[l].reshape(1, HEAD_DIM)
    ns = BLOCK_STATE
    ar_row = ssm_a_re[l].reshape(N_BLOCKS, 1, ns)
    ai_row = ssm_a_im[l].reshape(N_BLOCKS, 1, ns)
    dt_row = jnp.repeat(ssm_log_dt[l], SSM_STATE).reshape(N_BLOCKS, 1, ns)
    bd_bre = _block_diag_lanes(ssm_b_re[l])
    bd_bim = _block_diag_lanes(ssm_b_im[l])
    bd_cre = _block_diag_sublanes(ssm_c_re[l])
    bd_cim = _block_diag_sublanes(ssm_c_im[l])
    d_row = ssm_d[l].reshape(1, D_SSM)
    w_glu_b = w_glu[l].astype(BF16)
    b_glu_r = b_glu[l].reshape(1, D_SSM)
    shift_pad = (GATE_SHIFT, MERGE_COLS - D_MODEL - GATE_SHIFT)
    w_bs = jnp.pad(w_br_ssm[l].astype(BF16), ((0, 0), shift_pad))
    w_ba = jnp.pad(w_br_att[l].astype(BF16), ((0, 0), shift_pad))
    w_o = jnp.pad(w_out[l].astype(BF16), (shift_pad, (0, 0)))
    n_g = norm_ffn_g[l].reshape(1, D_MODEL)
    w_g = w_ffn_gate[l]
    w_u = w_ffn_up[l]
    w_d = w_ffn_down[l]

    vrev, bsum, m_op, bmat, cmat, a1, a16 = _s5gen(ar_row, ai_row, dt_row, bd_bre, bd_bim, bd_cre, bd_cim)

    xp = x_prompt.reshape(batch * seq, D_MODEL)
    rows = n_seq * n_tok
    xs = x_sample.transpose(1, 0, 2).reshape(rows, D_MODEL)
    ((u, qb, k, kb, v, vt, gates, logf),
     (us, qbs, ks, kbs, vs, vbs, gates_s, logf_s)) = _inproj(xp, xs, norm_g, w_in_b, w_tail, b_f_pad, q_g, k_g,
                                                             tm=PROJ_TM)
    caq, cak = _cumaug(logf, batch, seq)
    y_ssm, hre, him = _s5_prompt(u, vrev, bsum, m_op, a16, d_row, batch, seq)
    y_att = _attention(qb, caq, kb, cak, vt, batch, seq)

    ys_ssm, hre_s, him_s = _s5_sample(us, state_ssm_re[l].reshape(n_seq, -1), state_ssm_im[l].reshape(n_seq, -1),
                                      bmat, cmat, a1, d_row, n_seq, n_tok)
    by_seq = lambda a: a.reshape(n_tok, n_seq, -1).transpose(1, 0, 2)
    th = n_tok * N_HEADS
    th_rows = lambda a: by_seq(a).reshape(n_seq, th, HEAD_DIM)
    pad_keys = lambda a: jnp.pad(th_rows(a), ((0, 0), (0, LANES - th), (0, 0)))
    lf_s = by_seq(logf_s[:, :N_HEADS]).reshape(n_seq, th)
    lfrow = jnp.pad(lf_s, ((0, 0), (0, LANES - th))).reshape(n_seq, 1, LANES)
    lfcol = lf_s.reshape(n_seq, th, 1)
    n_pool = cache_k.shape[1]
    ya_s = _decode_attention(page_table, th_rows(qbs), lfrow, lfcol, pad_keys(kbs), pad_keys(vbs),
                             cache_k, cache_v, cache_logf[l].reshape(n_pool, 1, PAGE_KEYS), n_tok)
    ya_s = ya_s.reshape(n_seq, n_tok, D_ATT).transpose(1, 0, 2).reshape(rows, D_ATT).astype(BF16)

    x1, xn1, x1s, xn1s = _merge((y_ssm, y_att, gates, xp), (ys_ssm, ya_s, gates_s, xs),
                                w_glu_b, b_glu_r, w_bs, w_ba, w_o, n_g, tm=MERGE_TM)
    y_p, y_s = _ffn(x1, xn1, x1s, xn1s, w_g, w_u, w_d, tm=FFN_TM)
    y_p = y_p.reshape(batch, seq, D_MODEL)
    y_s = y_s.reshape(n_tok, n_seq, D_MODEL).transpose(1, 0, 2)

    heads = lambda a, b_, t_: a.reshape(1, b_, t_, N_HEADS, HEAD_DIM)
    tok_major = lambda a: a.reshape(n_tok, n_seq, -1).transpose(1, 0, 2)
    return (
        y_p, y_s,
        heads(k, batch, seq), heads(v, batch, seq),
        logf[:, :N_HEADS].reshape(1, batch, seq, N_HEADS),
        hre.reshape(1, batch, N_GROUPS, SSM_STATE), him.reshape(1, batch, N_GROUPS, SSM_STATE),
        heads(tok_major(ks), n_seq, n_tok), heads(tok_major(vs), n_seq, n_tok),
        tok_major(logf_s[:, :N_HEADS]).reshape(1, n_seq, n_tok, N_HEADS),
        hre_s.reshape(1, n_seq, N_GROUPS, SSM_STATE), him_s.reshape(1, n_seq, N_GROUPS, SSM_STATE),
    )
```

```python
import functools
import math

import jax
import jax.numpy as jnp
import numpy as np
from jax import lax
from jax.experimental import pallas as pl
from jax.experimental.pallas import tpu as pltpu

F32 = jnp.float32
BF16 = jnp.bfloat16
HIGHEST = lax.Precision.HIGHEST

D_MODEL = 2048
D_SSM = D_MODEL // 2
SSM_GROUP = 16
N_GROUPS = D_SSM // SSM_GROUP
SSM_STATE = 64
HEAD_DIM = 128
N_HEADS = (D_MODEL // 2) // HEAD_DIM
D_ATT = N_HEADS * HEAD_DIM
D_FF = ((8 * D_MODEL + 3 * 256 - 1) // (3 * 256)) * 256
PAGE_SIZE = 128
RMS_EPS = 1e-6
NEG_INF = -1e30

LANES = 128
GROUPS_PER_BLOCK = LANES // SSM_GROUP
N_BLOCKS = N_GROUPS // GROUPS_PER_BLOCK
BLOCK_STATE = GROUPS_PER_BLOCK * SSM_STATE
CHUNK = 16
PROJ_COLS = 2048
GATE_SHIFT = N_HEADS
ATT_TK = 512
ATT_TQ = 2 * ATT_TK
VMEM_LIMIT = 56 * 1024 * 1024
PROJ_TM = 512
MERGE_TM = 256
FFN_TM = 1024

LOG2E = math.log2(math.e)
QK_SCALE_LOG2 = (HEAD_DIM ** -0.5) * LOG2E

_NT = (((1,), (1,)), ((), ()))


def _params(sem):
    return pltpu.CompilerParams(dimension_semantics=sem, vmem_limit_bytes=VMEM_LIMIT)


def _gelu_exact(x):
    return 0.5 * x * (1.0 + lax.erf(x * math.sqrt(0.5)))


def _sigmoid(x):
    return 1.0 / (1.0 + jnp.exp(-x))


def _fold_lanes(x, op):
    acc = x[:, 0:LANES]
    for c in range(1, x.shape[1] // LANES):
        acc = op(acc, x[:, c * LANES:(c + 1) * LANES])
    return acc


def _row_max(x):
    return jnp.max(_fold_lanes(x, jnp.maximum), axis=-1, keepdims=True)


def _row_sum(x):
    return jnp.sum(_fold_lanes(x, jnp.add), axis=-1, keepdims=True)


def _log_sigmoid(x):
    return jnp.minimum(x, 0.0) - jnp.log1p(jnp.exp(-jnp.abs(x)))


def _head_norm(z, gain):
    outs = []
    for h in range(N_HEADS):
        blk = z[:, h * HEAD_DIM:(h + 1) * HEAD_DIM]
        ms = jnp.mean(blk * blk, axis=-1, keepdims=True)
        outs.append(blk * lax.rsqrt(ms + RMS_EPS) * gain)
    return jnp.concatenate(outs, axis=1)


def _proj_kernel(*refs, kind, nt, v_transposed):
    za, zb, zs = refs[-3:]
    n_out = _PROJ_OUTPUTS[kind]
    in_p, in_s, w_ref = refs[0], refs[1], refs[2]
    extra = refs[3:len(refs) - 3 - 2 * n_out]
    out_p = refs[len(refs) - 3 - 2 * n_out:len(refs) - 3 - n_out]
    out_s = refs[len(refs) - 3 - n_out:len(refs) - 3]
    s = pl.program_id(0)
    half = PROJ_COLS // 2

    def group(in_ref, outs, transposed):
        if kind == "uq":
            g_ref, qg_ref = extra
            xn_ref, u_ref, qb_ref = outs

            def lhs():
                x = in_ref[...]
                ms = jnp.mean(x * x, axis=-1, keepdims=True)
                xn = (x * lax.rsqrt(ms + RMS_EPS) * g_ref[...]).astype(BF16)
                xn_ref[...] = xn
                return xn

            def epilogue(z):
                u_ref[...] = z[:, 0:half]
                qb_ref[...] = (_head_norm(z[:, half:], qg_ref[...]) * QK_SCALE_LOG2).astype(BF16)
        elif kind == "kv":
            (kg_ref,) = extra
            k_ref, kb_ref, v_ref, vb_ref = outs
            lhs = lambda: in_ref[...]

            def epilogue(z):
                kn = _head_norm(z[:, 0:half], kg_ref[...])
                k_ref[...] = kn
                kb_ref[...] = kn.astype(BF16)
                v = z[:, half:]
                v_ref[...] = v
                if transposed:
                    vt = v.T
                    for c in range(vb_ref.shape[0]):
                        vb_ref[c] = vt[:, c * ATT_TK:(c + 1) * ATT_TK].astype(BF16)
                else:
                    vb_ref[...] = v.astype(BF16)
        elif kind == "gate_ssm":
            (bf_ref,) = extra
            gate_ref, logf_ref = outs
            lhs = lambda: in_ref[...]

            def epilogue(z, c0=0):
                gate_ref[:, c0:c0 + z.shape[1]] = _sigmoid(z).astype(BF16)
                if c0 == 0:
                    logf_ref[...] = _log_sigmoid(z[:, 0:LANES] + bf_ref[...])
        else:
            (wt_ref,) = extra
            gate_ref, tail_ref = outs

            def lhs():
                xn = in_ref[...]
                tail_ref[...] = _sigmoid(jnp.dot(xn, wt_ref[...], preferred_element_type=F32)).astype(BF16)
                return xn

            def epilogue(z, c0=0):
                gate_ref[:, c0:c0 + z.shape[1]] = _sigmoid(z).astype(BF16)
        return lhs, epilogue

    lhs_p, epilogue_p = group(in_p, out_p, v_transposed)
    lhs_s, epilogue_s = group(in_s, out_s, False)

    @pl.when(s == 0)
    def _():
        zb[...] = jnp.zeros_like(zb)

    def body(z_write, z_read):
        if kind in ("gate_ssm", "gate_att"):
            xn = lhs_p()
            for c0 in (0, half):
                epilogue_p(z_read[:, c0:c0 + half], c0)
                z_write[:, c0:c0 + half] = jnp.dot(xn, w_ref[:, c0:c0 + half], preferred_element_type=F32)
        else:
            epilogue_p(z_read[...])
            z_write[...] = jnp.dot(lhs_p(), w_ref[...], preferred_element_type=F32)

    pl.when((s < nt) & (s % 2 == 0))(lambda: body(za, zb))
    pl.when((s < nt) & (s % 2 == 1))(lambda: body(zb, za))

    @pl.when(s == nt)
    def _():
        epilogue_p((za if (nt - 1) % 2 == 0 else zb)[...])
        zs[...] = jnp.dot(lhs_s(), w_ref[...], preferred_element_type=F32)

    @pl.when(s == nt + 1)
    def _():
        epilogue_s(zs[...])


_PROJ_OUTPUTS = {"uq": 3, "kv": 4, "gate_ssm": 2, "gate_att": 2}


def _inproj(x, xs, norm_g, w_in_b, w_tail, b_f, q_g, k_g, tm):
    t, ts = x.shape[0], xs.shape[0]
    nt = t // tm
    hi = nt - 1
    cur = lambda s: (jnp.minimum(s, hi), 0)
    lag = lambda s: (jnp.clip(s - 1, 0, hi), 0)
    lag3 = lambda s: (jnp.clip(s - 1, 0, hi), 0, 0)
    const = lambda s: (0, 0)
    sds = jax.ShapeDtypeStruct

    def call(kind, slab_index, row_inputs, extra, extra_specs, widths, dtypes, lagged, vt=False):
        in_width = row_inputs[0].shape[1]
        in_specs = [pl.BlockSpec((tm, in_width), cur), pl.BlockSpec((ts, in_width), const),
                    pl.BlockSpec((D_MODEL, PROJ_COLS), lambda s: (0, slab_index), pipeline_mode=pl.Buffered(1))]
        out_shape, out_specs = [], []
        for rows_n, block_n, prompt in ((t, tm, True), (ts, ts, False)):
            for w, dt, lg in zip(widths, dtypes, lagged):
                if vt and prompt and w is None:
                    out_shape.append(sds((t // ATT_TK, D_ATT, ATT_TK), BF16))
                    out_specs.append(pl.BlockSpec((tm // ATT_TK, D_ATT, ATT_TK), lag3))
                    continue
                w = D_ATT if w is None else w
                out_shape.append(sds((rows_n, w), dt))
                index = (lag if lg else cur) if prompt else const
                out_specs.append(pl.BlockSpec((block_n, w), index))
        outs = pl.pallas_call(
            functools.partial(_proj_kernel, kind=kind, nt=nt, v_transposed=vt),
            out_shape=tuple(out_shape),
            grid=(nt + 2,),
            in_specs=in_specs + extra_specs,
            out_specs=tuple(out_specs),
            scratch_shapes=[pltpu.VMEM((tm, PROJ_COLS), F32), pltpu.VMEM((tm, PROJ_COLS), F32),
                            pltpu.VMEM((ts, PROJ_COLS), F32)],
            compiler_params=_params(("arbitrary",)),
            name="proj_" + kind,
        )(*row_inputs, w_in_b, *extra)
        n = len(widths)
        return outs[:n], outs[n:]

    gain = lambda w: pl.BlockSpec((1, w), const)
    (xn, u, qb), (xns, us, qbs) = call(
        "uq", 0, (x, xs), (norm_g, q_g), [gain(D_MODEL), gain(HEAD_DIM)],
        (D_MODEL, D_SSM, D_ATT), (BF16, F32, BF16), (False, True, True))
    (k, kb, v, vb), (ks, kbs, vs, vbs) = call(
        "kv", 1, (xn, xns), (k_g,), [gain(HEAD_DIM)],
        (D_ATT, D_ATT, D_ATT, None), (F32, BF16, F32, BF16), (True, True, True, True), vt=True)
    (gate_s, logf), (gate_ss, logfs) = call(
        "gate_ssm", 2, (xn, xns), (b_f,), [gain(LANES)],
        (PROJ_COLS, LANES), (BF16, F32), (True, True))
    (gate_a, gate_t), (gate_as, gate_ts) = call(
        "gate_att", 3, (xn, xns), (w_tail,), [pl.BlockSpec((D_MODEL, LANES), const)],
        (PROJ_COLS, LANES), (BF16, BF16), (True, False))
    return ((u, qb, k, kb, v, vb, (gate_s, gate_a, gate_t), logf),
            (us, qbs, ks, kbs, vs, vbs, (gate_ss, gate_as, gate_ts), logfs))


CUM_ROWS = 256


def _cumaug_kernel(lf_ref, eq_ref, ek_ref, caq_ref, cak_ref, carry_sc):
    @pl.when(pl.program_id(1) == 0)
    def _():
        carry_sc[...] = jnp.zeros_like(carry_sc)

    lf = lf_ref[...]
    r = lax.broadcasted_iota(jnp.int32, (CUM_ROWS, CUM_ROWS), 0)
    c = lax.broadcasted_iota(jnp.int32, (CUM_ROWS, CUM_ROWS), 1)
    tri = (c <= r).astype(BF16)
    lf_hi = lf.astype(BF16)
    lf_r = lf - lf_hi.astype(F32)
    lf_mid = lf_r.astype(BF16)
    lf_lo = (lf_r - lf_mid.astype(F32)).astype(BF16)
    parts = jnp.dot(tri, jnp.concatenate([lf_hi, lf_mid, lf_lo], axis=1), preferred_element_type=F32)
    cs = (parts[:, 0:LANES] + parts[:, LANES:2 * LANES] + parts[:, 2 * LANES:3 * LANES]) + carry_sc[...]
    carry_sc[...] = cs[CUM_ROWS - 1:CUM_ROWS, :]
    cs2 = cs * LOG2E
    hi = cs2.astype(BF16).astype(F32)
    r1 = cs2 - hi
    mid = r1.astype(BF16).astype(F32)
    lo = (r1 - mid).astype(BF16).astype(F32)
    lane = lax.broadcasted_iota(jnp.int32, (CUM_ROWS, LANES), 1)
    packed = jnp.where(lane < N_HEADS, hi,
                       jnp.where(lane < 2 * N_HEADS, pltpu.roll(mid, N_HEADS, axis=1),
                                 jnp.where(lane < 3 * N_HEADS, pltpu.roll(lo, 2 * N_HEADS, axis=1),
                                           jnp.where(lane == 3 * N_HEADS, 1.0, 0.0)))).astype(BF16)
    caq_ref[...] = jnp.dot(packed, eq_ref[...], preferred_element_type=F32).astype(BF16)
    cak_ref[...] = jnp.dot(packed, ek_ref[...], preferred_element_type=F32).astype(BF16)


def _placement_matrices():
    eq = np.zeros((LANES, D_ATT), np.float32)
    ek = np.zeros((LANES, D_ATT), np.float32)
    for h in range(N_HEADS):
        base = h * HEAD_DIM
        for piece in range(3):
            eq[piece * N_HEADS + h, base + piece] = 1.0
            ek[piece * N_HEADS + h, base + 3 + piece] = -1.0
            eq[3 * N_HEADS, base + 3 + piece] = 1.0
            ek[3 * N_HEADS, base + piece] = 1.0
    return jnp.asarray(eq, BF16), jnp.asarray(ek, BF16)


def _cumaug(logf, batch, seq):
    nb = seq // CUM_ROWS
    eq, ek = _placement_matrices()
    const = lambda: pl.BlockSpec((LANES, D_ATT), lambda b, i: (0, 0))
    return pl.pallas_call(
        _cumaug_kernel,
        out_shape=(jax.ShapeDtypeStruct((batch * seq, D_ATT), BF16),
                   jax.ShapeDtypeStruct((batch * seq, D_ATT), BF16)),
        grid=(batch, nb),
        in_specs=[pl.BlockSpec((CUM_ROWS, LANES), lambda b, i: (b * nb + i, 0)), const(), const()],
        out_specs=(pl.BlockSpec((CUM_ROWS, D_ATT), lambda b, i: (b * nb + i, 0)),
                   pl.BlockSpec((CUM_ROWS, D_ATT), lambda b, i: (b * nb + i, 0))),
        scratch_shapes=[pltpu.VMEM((1, LANES), F32)],
        compiler_params=_params(("arbitrary", "arbitrary")),
        name="cumaug",
    )(logf, eq, ek)


BF16_SUBLANES = 16
ACC_ROWS = HEAD_DIM + BF16_SUBLANES


ATT_HEADS = 4


def _attn_kernel(q_ref, cq_ref, k_ref, ck_ref, vt_ref, o_ref, *scratch):
    i = pl.program_id(2)
    tk = ATT_TK
    heads = range(ATT_HEADS)
    m_sc, acc_sc, sa_sc, sb_sc = (scratch[0::4], scratch[1::4], scratch[2::4], scratch[3::4])
    for hd in heads:
        m_sc[hd][...] = jnp.full(m_sc[hd].shape, NEG_INF, F32)
        acc_sc[hd][...] = jnp.zeros_like(acc_sc[hd])
    sub = lax.broadcasted_iota(jnp.int32, (ACC_ROWS - HEAD_DIM, tk), 0)
    ones_rows = jnp.where(sub == 0, 1.0, 0.0).astype(BF16)

    def logits(hd, j, s_ref, lo):
        ks = pl.multiple_of(j * tk, tk)
        cols = slice(hd * HEAD_DIM, (hd + 1) * HEAD_DIM)
        kk = jnp.concatenate([k_ref[pl.ds(ks, tk), cols], ck_ref[pl.ds(ks, tk), cols]], axis=1)
        qq = jnp.concatenate([q_ref[lo:, cols], cq_ref[lo:, cols]], axis=1)
        s_ref[hd][:, lo:] = lax.dot_general(kk, qq, _NT, preferred_element_type=F32)

    def consume(hd, j, s_ref, lo, masked):
        s = s_ref[hd][:, lo:]
        if masked:
            key = lax.broadcasted_iota(jnp.int32, s.shape, 0)
            qry = lax.broadcasted_iota(jnp.int32, s.shape, 1)
            s = jnp.where(key <= qry, s, NEG_INF)
        vt = jnp.concatenate([vt_ref[j, hd * HEAD_DIM:(hd + 1) * HEAD_DIM, :], ones_rows], axis=0)
        m_old = m_sc[hd][:, lo:]
        m_new = jnp.maximum(m_old, jnp.max(s, axis=0, keepdims=True))
        alpha = jnp.exp2(m_old - m_new)
        p = jnp.exp2(s - m_new).astype(BF16)
        acc_sc[hd][:, lo:] = alpha * acc_sc[hd][:, lo:] + jnp.dot(vt, p, preferred_element_type=F32)
        m_sc[hd][:, lo:] = m_new

    for hd in heads:
        logits(hd, 0, sa_sc, 0)

    def pair(jj, carry):
        for hd in heads:
            logits(hd, 2 * jj + 1, sb_sc, 0)
            consume(hd, 2 * jj, sa_sc, 0, False)
        for hd in heads:
            logits(hd, 2 * jj + 2, sa_sc, 0)
            consume(hd, 2 * jj + 1, sb_sc, 0, False)
        return carry

    lax.fori_loop(0, i, pair, 0)
    for hd in heads:
        logits(hd, 2 * i + 1, sb_sc, tk)
        consume(hd, 2 * i, sa_sc, 0, True)
    for hd in heads:
        consume(hd, 2 * i + 1, sb_sc, tk, True)
        acc = acc_sc[hd][...]
        out_t = acc[0:HEAD_DIM, :] / acc[HEAD_DIM:HEAD_DIM + 1, :]
        o_ref[:, hd * HEAD_DIM:(hd + 1) * HEAD_DIM] = out_t.T.astype(BF16)


def _attention(qb, caq, kb, cak, vt, batch, seq):
    nq = seq // ATT_TQ
    nkb = seq // ATT_TK
    width = ATT_HEADS * HEAD_DIM
    qspec = lambda: pl.BlockSpec((ATT_TQ, width), lambda b, h, i: (b * nq + i, h))
    kspec = lambda: pl.BlockSpec((seq, width), lambda b, h, i: (b, h))
    per_head = [pltpu.VMEM((1, ATT_TQ), F32), pltpu.VMEM((ACC_ROWS, ATT_TQ), F32),
                pltpu.VMEM((ATT_TK, ATT_TQ), F32), pltpu.VMEM((ATT_TK, ATT_TQ), F32)]
    return pl.pallas_call(
        _attn_kernel,
        out_shape=jax.ShapeDtypeStruct((batch * seq, D_ATT), BF16),
        grid=(batch, N_HEADS // ATT_HEADS, nq),
        in_specs=[qspec(), qspec(), kspec(), kspec(),
                  pl.BlockSpec((nkb, width, ATT_TK), lambda b, h, i: (b, h, 0))],
        out_specs=qspec(),
        scratch_shapes=per_head * ATT_HEADS,
        compiler_params=_params(("arbitrary", "arbitrary", "arbitrary")),
        name="fox_prompt",
    )(qb, caq, kb, cak, vt)


def _discretise(a_re, a_im, log_dt):
    dt = jnp.exp(log_dt)
    mag = jnp.exp(dt * a_re)
    ang = dt * a_im
    abr = mag * jnp.cos(ang)
    abi = mag * jnp.sin(ang)
    e_re = abr - 1.0
    e_im = abi
    inv_den = 1.0 / (a_re * a_re + a_im * a_im)
    f_re = (e_re * a_re + e_im * a_im) * inv_den
    f_im = (e_im * a_re - e_re * a_im) * inv_den
    return abr, abi, f_re, f_im


def _powers(abr, abi, n):
    pr, pi = [jnp.ones_like(abr)], [jnp.zeros_like(abi)]
    for _ in range(n):
        r, i = pr[-1], pi[-1]
        pr.append(r * abr - i * abi)
        pi.append(r * abi + i * abr)
    return pr, pi


def _s5gen_kernel(ar_row, ai_row, dt_row, bre_ref, bim_ref, cre_ref, cim_ref,
                  vrev_ref, bsum_ref, m_ref, bmat_ref, cmat_ref, a1_ref, a16_ref):
    ns = BLOCK_STATE
    abr, abi, f_re, f_im = _discretise(ar_row[...], ai_row[...], dt_row[...])
    bre, bim = bre_ref[...], bim_ref[...]
    bbr = f_re * bre - f_im * bim
    bbi = f_re * bim + f_im * bre
    bmat = jnp.concatenate([bbr, bbi], axis=1)
    bmat_ref[...] = bmat
    pr, pi = _powers(abr, abi, CHUNK)
    a1_ref[...] = jnp.concatenate([pr[1], pi[1]], axis=1)
    a16_ref[...] = jnp.concatenate([pr[CHUNK], pi[CHUNK]], axis=1)
    for c in range(ns // LANES):
        lo, hi = c * LANES, (c + 1) * LANES
        bbr_c, bbi_c = bbr[:, lo:hi], bbi[:, lo:hi]
        for i in range(CHUNK):
            r, im = pr[CHUNK - 1 - i][:, lo:hi], pi[CHUNK - 1 - i][:, lo:hi]
            bsum_ref[i * LANES:(i + 1) * LANES, lo:hi] = (r * bbr_c - im * bbi_c).astype(BF16)
            bsum_ref[i * LANES:(i + 1) * LANES, ns + lo:ns + hi] = (r * bbi_c + im * bbr_c).astype(BF16)
    npow = CHUNK + 1
    stacked = jnp.concatenate(pr + pi + [jnp.zeros((LANES - 2 * npow, ns), F32)], axis=0)
    pt = stacked.T
    qr = [pt[:, t:t + 1] for t in range(npow)]
    qi = [pt[:, npow + t:npow + t + 1] for t in range(npow)]
    cre, cim = cre_ref[...], cim_ref[...]
    vrev_ref[(CHUNK - 1) * LANES:CHUNK * LANES, 0:LANES] = jnp.zeros((LANES, LANES), BF16)
    bmat_hi = bmat.astype(BF16)
    for tau in range(CHUNK + 1):
        blk = jnp.concatenate([cre * qr[tau] - cim * qi[tau],
                               -(cre * qi[tau] + cim * qr[tau])], axis=0)
        if tau == 0:
            cmat_ref[...] = blk
        else:
            m_ref[:, (tau - 1) * LANES:tau * LANES] = blk.astype(BF16)
        if tau < CHUNK:
            w = jnp.dot(bmat_hi, blk.astype(BF16), preferred_element_type=F32).astype(BF16)
            k = CHUNK - 1 - tau
            vrev_ref[k * LANES:(k + 1) * LANES, LANES:2 * LANES] = w
            if k >= 1:
                vrev_ref[(k - 1) * LANES:k * LANES, 0:LANES] = w


def _s5gen(ar_row, ai_row, dt_row, bd_bre, bd_bim, bd_cre, bd_cim):
    ns = BLOCK_STATE
    b3 = lambda s1, s2: pl.BlockSpec((None, s1, s2), lambda o: (o, 0, 0))
    out_shape = (
        jax.ShapeDtypeStruct((N_BLOCKS, CHUNK * LANES, 2 * LANES), BF16),
        jax.ShapeDtypeStruct((N_BLOCKS, CHUNK * LANES, 2 * ns), BF16),
        jax.ShapeDtypeStruct((N_BLOCKS, 2 * ns, CHUNK * LANES), BF16),
        jax.ShapeDtypeStruct((N_BLOCKS, LANES, 2 * ns), F32),
        jax.ShapeDtypeStruct((N_BLOCKS, 2 * ns, LANES), F32),
        jax.ShapeDtypeStruct((N_BLOCKS, 1, 2 * ns), F32),
        jax.ShapeDtypeStruct((N_BLOCKS, 1, 2 * ns), F32),
    )
    return pl.pallas_call(
        _s5gen_kernel,
        out_shape=out_shape,
        grid=(N_BLOCKS,),
        in_specs=[b3(1, ns), b3(1, ns), b3(1, ns),
                  b3(LANES, ns), b3(LANES, ns), b3(ns, LANES), b3(ns, LANES)],
        out_specs=(b3(CHUNK * LANES, 2 * LANES), b3(CHUNK * LANES, 2 * ns), b3(2 * ns, CHUNK * LANES),
                   b3(LANES, 2 * ns), b3(2 * ns, LANES), b3(1, 2 * ns), b3(1, 2 * ns)),
        compiler_params=_params(("arbitrary",)),
        name="s5_operators",
    )(ar_row, ai_row, dt_row, bd_bre, bd_bim, bd_cre, bd_cim)


def _s5_prompt_kernel(u_ref, vrev_ref, bsum_ref, m_ref, a16_ref, d_ref,
                      y_ref, hre_ref, him_ref, ucat_sc, s_sc, hin_sc, yt_sc, *, n_chunks):
    ns = BLOCK_STATE
    for i in range(CHUNK):
        ucat_sc[:, i * LANES:(i + 1) * LANES] = u_ref[pl.ds(i, n_chunks, stride=CHUNK), :].astype(BF16)
    s_sc[...] = jnp.dot(ucat_sc[...], bsum_ref[...], preferred_element_type=F32)
    ar = a16_ref[:, 0:ns]
    ai = a16_ref[:, ns:2 * ns]
    for jp in range(CHUNK // 2):
        j = 2 * jp
        kk = (j + 2) * LANES
        yt_sc[:, j * LANES:(j + 2) * LANES] = jnp.dot(
            ucat_sc[:, 0:kk], vrev_ref[(CHUNK - 2 - j) * LANES:, :], preferred_element_type=F32)

    hr = jnp.zeros((1, ns), F32)
    hi = jnp.zeros((1, ns), F32)
    for k in range(n_chunks):
        hin_sc[k:k + 1, 0:ns] = hr
        hin_sc[k:k + 1, ns:2 * ns] = hi
        sr = s_sc[k:k + 1, 0:ns]
        si = s_sc[k:k + 1, ns:2 * ns]
        hr, hi = ar * hr - ai * hi + sr, ar * hi + ai * hr + si
    hre_ref[...] = hr
    him_ref[...] = hi
    hin = hin_sc[...].astype(BF16)
    d = d_ref[...]
    for jp in range(CHUNK // 2):
        j = 2 * jp
        acc = yt_sc[:, j * LANES:(j + 2) * LANES] + jnp.dot(
            hin, m_ref[:, j * LANES:(j + 2) * LANES], preferred_element_type=F32)
        for jj in range(2):
            uj = u_ref[pl.ds(j + jj, n_chunks, stride=CHUNK), :]
            val = acc[:, jj * LANES:(jj + 1) * LANES] + d * uj
            y_ref[pl.ds(j + jj, n_chunks, stride=CHUNK), :] = _gelu_exact(val)


def _s5_prompt(u, vrev, bsum, m, a16, d, batch, seq):
    ns = BLOCK_STATE
    n_chunks = seq // CHUNK
    w3 = lambda s1, s2: pl.BlockSpec((None, s1, s2), lambda o, b: (o, 0, 0))
    return pl.pallas_call(
        functools.partial(_s5_prompt_kernel, n_chunks=n_chunks),
        out_shape=(jax.ShapeDtypeStruct((batch * seq, D_SSM), F32),
                   jax.ShapeDtypeStruct((batch, 1, N_GROUPS * SSM_STATE), F32),
                   jax.ShapeDtypeStruct((batch, 1, N_GROUPS * SSM_STATE), F32)),
        grid=(N_BLOCKS, batch),
        in_specs=[pl.BlockSpec((seq, LANES), lambda o, b: (b, o)),
                  w3(CHUNK * LANES, 2 * LANES), w3(CHUNK * LANES, 2 * ns), w3(2 * ns, CHUNK * LANES),
                  w3(1, 2 * ns),
                  pl.BlockSpec((1, LANES), lambda o, b: (0, o))],
        out_specs=(pl.BlockSpec((seq, LANES), lambda o, b: (b, o)),
                   pl.BlockSpec((None, 1, ns), lambda o, b: (b, 0, o)),
                   pl.BlockSpec((None, 1, ns), lambda o, b: (b, 0, o))),
        scratch_shapes=[pltpu.VMEM((n_chunks, CHUNK * LANES), BF16),
                        pltpu.VMEM((n_chunks, 2 * ns), F32),
                        pltpu.VMEM((n_chunks, 2 * ns), F32),
                        pltpu.VMEM((n_chunks, CHUNK * LANES), F32)],
        compiler_params=_params(("arbitrary", "arbitrary")),
        name="s5_prompt",
    )(u, vrev, bsum, m, a16, d)


def _s5_sample_kernel(u_ref, h0r_ref, h0i_ref, bmat_ref, cmat_ref, a1_ref, d_ref,
                      y_ref, hre_ref, him_ref, *, n_seq, n_tok):
    ns = BLOCK_STATE
    u = u_ref[...]
    bu = jnp.dot(u, bmat_ref[...], precision=HIGHEST, preferred_element_type=F32)
    ar = a1_ref[:, 0:ns]
    ai = a1_ref[:, ns:2 * ns]
    hr, hi = h0r_ref[...], h0i_ref[...]
    hs = []
    for t in range(n_tok):
        br = bu[t * n_seq:(t + 1) * n_seq, 0:ns]
        bi = bu[t * n_seq:(t + 1) * n_seq, ns:2 * ns]
        hr, hi = ar * hr - ai * hi + br, ar * hi + ai * hr + bi
        hs.append(jnp.concatenate([hr, hi], axis=1))
    hcat = jnp.concatenate(hs, axis=0)
    val = jnp.dot(hcat, cmat_ref[...], precision=HIGHEST, preferred_element_type=F32) + d_ref[...] * u
    y_ref[...] = _gelu_exact(val)
    hre_ref[...] = hr
    him_ref[...] = hi


def _s5_sample(u, h0r, h0i, bmat, cmat, a1, d, n_seq, n_tok):
    ns = BLOCK_STATE
    rows = n_seq * n_tok
    w3 = lambda s1, s2: pl.BlockSpec((None, s1, s2), lambda o: (o, 0, 0))
    col = lambda r, c: pl.BlockSpec((r, c), lambda o: (0, o))
    return pl.pallas_call(
        functools.partial(_s5_sample_kernel, n_seq=n_seq, n_tok=n_tok),
        out_shape=(jax.ShapeDtypeStruct((rows, D_SSM), F32),
                   jax.ShapeDtypeStruct((n_seq, N_GROUPS * SSM_STATE), F32),
                   jax.ShapeDtypeStruct((n_seq, N_GROUPS * SSM_STATE), F32)),
        grid=(N_BLOCKS,),
        in_specs=[col(rows, LANES), col(n_seq, ns), col(n_seq, ns),
                  w3(LANES, 2 * ns), w3(2 * ns, LANES), w3(1, 2 * ns), col(1, LANES)],
        out_specs=(col(rows, LANES), col(n_seq, ns), col(n_seq, ns)),
        compiler_params=_params(("arbitrary",)),
        name="s5_sample",
    )(u, h0r, h0i, bmat, cmat, a1, d)


PAGES_PER_STEP = 16
PAGE_GROUP = 4

PAGE_KEYS = PAGE_SIZE * N_HEADS


def _decode_kernel(pt_ref, q_ref, lfrow_ref, lfcol_ref, kn_ref, vn_ref, *refs, n_tok, n_steps):
    g = PAGES_PER_STEP
    k_refs = refs[0:g]
    v_refs = refs[g:2 * g]
    lf_refs = refs[2 * g:3 * g]
    o_ref = refs[3 * g]
    m_sc, l_sc, acc_sc, carry_sc = refs[3 * g + 1:]
    j = pl.program_id(1)
    rows = n_tok * N_HEADS
    head_mask = N_HEADS - 1
    head_shift = N_HEADS.bit_length() - 1

    @pl.when(j == 0)
    def _():
        m_sc[...] = jnp.full(m_sc.shape, NEG_INF, F32)
        l_sc[...] = jnp.zeros_like(l_sc)
        acc_sc[...] = jnp.zeros_like(acc_sc)
        carry_sc[...] = jnp.zeros_like(carry_sc)

    q = q_ref[...]
    lfcol = lfcol_ref[...]
    run = jnp.zeros((N_HEADS, 1), F32)
    pieces = []
    for t in range(n_tok):
        run = run + lfcol[t * N_HEADS:(t + 1) * N_HEADS, :]
        pieces.append(run)
    c_col = jnp.concatenate(pieces, axis=0)

    def online(s_blocks, v_blocks):
        m = m_sc[...]
        smax = s_blocks[0]
        for s in s_blocks[1:]:
            smax = jnp.maximum(smax, s)
        m_new = jnp.maximum(m, _row_max(smax))
        alpha = jnp.exp2(m - m_new)
        psum = None
        pv = None
        for s, vblk in zip(s_blocks, v_blocks):
            p = jnp.exp2(s - m_new)
            psum = p if psum is None else psum + p
            d = jnp.dot(p.astype(BF16), vblk, preferred_element_type=F32)
            pv = d if pv is None else pv + d
        l_sc[...] = alpha * l_sc[...] + _row_sum(psum)
        acc_sc[...] = alpha * acc_sc[...] + pv
        m_sc[...] = m_new

    col = lax.broadcasted_iota(jnp.int32, (rows, PAGE_KEYS), 1)
    row = lax.broadcasted_iota(jnp.int32, (rows, PAGE_KEYS), 0)
    own = (col & head_mask) == (row & head_mask)
    fixed = jnp.where(own, c_col * LOG2E, NEG_INF)

    lf = jnp.concatenate([lf_refs[pg][...] for pg in range(g)], axis=0)
    lane = lax.broadcasted_iota(jnp.int32, (g, PAGE_KEYS), 1)
    suffix = lf
    total = lf
    sh = N_HEADS
    while sh < PAGE_KEYS:
        suffix = suffix + jnp.where(lane + sh < PAGE_KEYS, pltpu.roll(suffix, PAGE_KEYS - sh, axis=1), 0.0)
        total = total + pltpu.roll(total, PAGE_KEYS - sh, axis=1)
        sh *= 2
    carry = carry_sc[...]
    past = []
    for pg in range(g):
        past.append((suffix[pg:pg + 1, :] - lf[pg:pg + 1, :] + carry) * LOG2E)
        carry = carry + total[pg:pg + 1, :]
    carry_sc[...] = carry

    def logits(pages):
        out = []
        for pg in pages:
            kp = k_refs[pg][...].reshape(PAGE_KEYS, HEAD_DIM).astype(BF16)
            s = lax.dot_general(q, kp, _NT, preferred_element_type=F32)
            out.append(s + fixed + past[pg])
        return out

    def values(pages):
        return [v_refs[pg][...].reshape(PAGE_KEYS, HEAD_DIM).astype(BF16) for pg in pages]

    groups = [list(range(a, a + PAGE_GROUP)) for a in range(0, g, PAGE_GROUP)]
    s_next = logits(groups[0])
    for gi, pages in enumerate(groups):
        s_cur = s_next
        if gi + 1 < len(groups):
            s_next = logits(groups[gi + 1])
        online(s_cur, values(pages))

    @pl.when(j == n_steps - 1)
    def _():
        ln = lax.broadcasted_iota(jnp.int32, (1, LANES), 1)
        c_row = lfrow_ref[...]
        sh2 = N_HEADS
        while sh2 < rows:
            c_row = c_row + jnp.where(ln >= sh2, pltpu.roll(c_row, sh2, axis=1), 0.0)
            sh2 *= 2
        cl = lax.broadcasted_iota(jnp.int32, (rows, LANES), 1)
        rw = lax.broadcasted_iota(jnp.int32, (rows, LANES), 0)
        valid = ((cl < rows) & ((cl & head_mask) == (rw & head_mask))
                 & ((cl >> head_shift) <= (rw >> head_shift)))
        s = lax.dot_general(q, kn_ref[...], _NT, preferred_element_type=F32)
        s = jnp.where(valid, s + (c_col - c_row) * LOG2E, NEG_INF)
        online([s], [vn_ref[...]])
        o_ref[...] = acc_sc[...] / l_sc[...]


def _decode_attention(page_table, q, lfrow, lfcol, kn, vn, cache_k, cache_v, cache_lf, n_tok):
    n_seq, n_pages = page_table.shape
    g = PAGES_PER_STEP
    n_steps = n_pages // g
    rows = n_tok * N_HEADS

    def page5(pg):
        return lambda b, j, pt: (0, pt[b * n_pages + (n_pages - 1 - (j * g + pg))], 0, 0, 0)

    def page3(pg):
        return lambda b, j, pt: (pt[b * n_pages + (n_pages - 1 - (j * g + pg))], 0, 0)

    seq3 = lambda s1, s2: pl.BlockSpec((None, s1, s2), lambda b, j, pt: (b, 0, 0))
    kv_block = (None, None, PAGE_SIZE, N_HEADS, HEAD_DIM)
    in_specs = [seq3(rows, HEAD_DIM), seq3(1, LANES), seq3(rows, 1), seq3(LANES, HEAD_DIM), seq3(LANES, HEAD_DIM)]
    in_specs += [pl.BlockSpec(kv_block, page5(pg)) for pg in range(g)]
    in_specs += [pl.BlockSpec(kv_block, page5(pg)) for pg in range(g)]
    in_specs += [pl.BlockSpec((None, 1, PAGE_KEYS), page3(pg)) for pg in range(g)]
    grid_spec = pltpu.PrefetchScalarGridSpec(
        num_scalar_prefetch=1,
        grid=(n_seq, n_steps),
        in_specs=in_specs,
        out_specs=pl.BlockSpec((None, rows, HEAD_DIM), lambda b, j, pt: (b, 0, 0)),
        scratch_shapes=[pltpu.VMEM((rows, 1), F32),
                        pltpu.VMEM((rows, 1), F32),
                        pltpu.VMEM((rows, HEAD_DIM), F32),
                        pltpu.VMEM((1, PAGE_KEYS), F32)],
    )
    return pl.pallas_call(
        functools.partial(_decode_kernel, n_tok=n_tok, n_steps=n_steps),
        out_shape=jax.ShapeDtypeStruct((n_seq, rows, HEAD_DIM), F32),
        grid_spec=grid_spec,
        compiler_params=_params(("arbitrary", "arbitrary")),
        name="fox_sample",
    )(page_table.reshape(-1), q, lfrow, lfcol, kn, vn,
      *([cache_k] * g), *([cache_v] * g), *([cache_lf] * g))


def _merge_kernel(*refs, nt):
    rows_p, rows_s = refs[0:6], refs[6:12]
    wglu_ref, bglu_ref, wbs_ref, wba_ref, wo_ref, ng_ref = refs[12:18]
    outs_p, outs_s = refs[18:20], refs[20:22]

    def rows(y_ref, ya_ref, gs_ref, ga_ref, gt_ref, x_ref, x1_ref, xn_ref):
        y = y_ref[...]
        gl = jnp.dot(y.astype(BF16), wglu_ref[...], preferred_element_type=F32) + bglu_ref[...]
        ys = (y * _sigmoid(gl)).astype(BF16)
        ms = jnp.dot(ys, wbs_ref[...], preferred_element_type=F32)
        ma = jnp.dot(ya_ref[...], wba_ref[...], preferred_element_type=F32)
        ga = ga_ref[...]
        gate_s = jnp.concatenate([gs_ref[...], ga[:, 0:LANES]], axis=1).astype(F32)
        gate_a = jnp.concatenate([ga, gt_ref[...]], axis=1).astype(F32)
        merged = gate_s * ms + gate_a * ma
        x1 = x_ref[...] + jnp.dot(merged.astype(BF16), wo_ref[...], preferred_element_type=F32)
        x1_ref[...] = x1
        ms1 = jnp.mean(x1 * x1, axis=-1, keepdims=True)
        xn_ref[...] = (x1 * lax.rsqrt(ms1 + RMS_EPS) * ng_ref[...]).astype(BF16)

    i = pl.program_id(0)
    pl.when(i < nt)(lambda: rows(*rows_p, *outs_p))
    pl.when(i == nt)(lambda: rows(*rows_s, *outs_s))


MERGE_COLS = D_MODEL + LANES


def _merge(prompt, sample, w_glu, b_glu, w_bs, w_ba, w_o, n_g, tm):
    flat = lambda g: (g[0], g[1], *g[2], g[3])
    t, ts = prompt[3].shape[0], sample[3].shape[0]
    nt = t // tm
    widths = (D_SSM, D_ATT, PROJ_COLS, PROJ_COLS, LANES, D_MODEL)
    tile = lambda c: pl.BlockSpec((tm, c), lambda i: (jnp.minimum(i, nt - 1), 0))
    small = lambda c: pl.BlockSpec((ts, c), lambda i: (0, 0))
    const = lambda r, c: pl.BlockSpec((r, c), lambda i: (0, 0), pipeline_mode=pl.Buffered(1))
    sds = jax.ShapeDtypeStruct
    return pl.pallas_call(
        functools.partial(_merge_kernel, nt=nt),
        out_shape=(sds((t, D_MODEL), F32), sds((t, D_MODEL), BF16), sds((ts, D_MODEL), F32), sds((ts, D_MODEL), BF16)),
        grid=(nt + 1,),
        in_specs=[tile(c) for c in widths] + [small(c) for c in widths]
        + [const(D_SSM, D_SSM), const(1, D_SSM), const(D_SSM, MERGE_COLS), const(D_ATT, MERGE_COLS),
           const(MERGE_COLS, D_MODEL), const(1, D_MODEL)],
        out_specs=(tile(D_MODEL), tile(D_MODEL), small(D_MODEL), small(D_MODEL)),
        compiler_params=_params(("arbitrary",)),
        name="merge_out",
    )(*flat(prompt), *flat(sample), w_glu, b_glu, w_bs, w_ba, w_o, n_g)


FFN_TF = 256


def _ffn_kernel(x1_ref, xn_ref, x1s_ref, xns_ref, wg_ref, wu_ref, wd_ref, o_ref, os_ref, wg_sc, wu_sc, wd_sc):
    i, f = pl.program_id(0), pl.program_id(1)

    @pl.when(f == 0)
    def _():
        o_ref[...] = x1_ref[...]

    xn = xn_ref[...]
    wg_sc[...] = wg_ref[...].astype(BF16)
    a = jnp.dot(xn, wg_sc[...], preferred_element_type=F32)
    wu_sc[...] = wu_ref[...].astype(BF16)
    b = jnp.dot(xn, wu_sc[...], preferred_element_type=F32)
    wd_sc[...] = wd_ref[...].astype(BF16)
    o_ref[...] += jnp.dot((a * _sigmoid(a) * b).astype(BF16), wd_sc[...], preferred_element_type=F32)

    @pl.when(i == pl.num_programs(0) - 1)
    def _():
        @pl.when(f == 0)
        def _():
            os_ref[...] = x1s_ref[...]

        xs = xns_ref[...]
        a_s = jnp.dot(xs, wg_sc[...], preferred_element_type=F32)
        b_s = jnp.dot(xs, wu_sc[...], preferred_element_type=F32)
        os_ref[...] += jnp.dot((a_s * _sigmoid(a_s) * b_s).astype(BF16), wd_sc[...], preferred_element_type=F32)


def _ffn(x1, xn, x1s, xns, w_g, w_u, w_d, tm):
    t = x1.shape[0]
    ts = x1s.shape[0]
    small = lambda: pl.BlockSpec((ts, D_MODEL), lambda i, f: (0, 0))
    return pl.pallas_call(
        _ffn_kernel,
        out_shape=(jax.ShapeDtypeStruct((t, D_MODEL), F32), jax.ShapeDtypeStruct((ts, D_MODEL), F32)),
        grid=(t // tm, D_FF // FFN_TF),
        in_specs=[pl.BlockSpec((tm, D_MODEL), lambda i, f: (i, 0)),
                  pl.BlockSpec((tm, D_MODEL), lambda i, f: (i, 0), pipeline_mode=pl.Buffered(1)),
                  small(), small(),
                  pl.BlockSpec((D_MODEL, FFN_TF), lambda i, f: (0, f)),
                  pl.BlockSpec((D_MODEL, FFN_TF), lambda i, f: (0, f)),
                  pl.BlockSpec((FFN_TF, D_MODEL), lambda i, f: (f, 0))],
        out_specs=(pl.BlockSpec((tm, D_MODEL), lambda i, f: (i, 0)), small()),
        scratch_shapes=[pltpu.VMEM((D_MODEL, FFN_TF), BF16), pltpu.VMEM((D_MODEL, FFN_TF), BF16),
                        pltpu.VMEM((FFN_TF, D_MODEL), BF16)],
        compiler_params=_params(("arbitrary", "arbitrary")),
        name="ffn",
    )(x1, xn, x1s, xns, w_g, w_u, w_d)


def _same_group_mask():
    chan_group = np.arange(LANES)[:, None] // SSM_GROUP
    state_group = np.arange(BLOCK_STATE)[None, :] // SSM_STATE
    return chan_group == state_group


def _block_diag_lanes(p):
    p4 = p.reshape(N_BLOCKS, GROUPS_PER_BLOCK, SSM_STATE, SSM_GROUP)
    rows = p4.transpose(0, 3, 1, 2).reshape(N_BLOCKS, SSM_GROUP, BLOCK_STATE)
    tiled = jnp.tile(rows, (1, GROUPS_PER_BLOCK, 1))
    return jnp.where(jnp.asarray(_same_group_mask()), tiled, 0.0)


def _block_diag_sublanes(p):
    p4 = p.reshape(N_BLOCKS, GROUPS_PER_BLOCK, SSM_GROUP, SSM_STATE)
    cols = p4.transpose(0, 1, 3, 2).reshape(N_BLOCKS, BLOCK_STATE, SSM_GROUP)
    tiled = jnp.tile(cols, (1, 1, GROUPS_PER_BLOCK))
    return jnp.where(jnp.asarray(_same_group_mask().T), tiled, 0.0)


def kernel(x_prompt, x_sample, cache_k, cache_v, cache_logf, state_ssm_re, state_ssm_im, page_table,
           norm_mix_g, w_in, b_f, q_norm_g, k_norm_g, ssm_a_re, ssm_a_im, ssm_log_dt, ssm_b_re, ssm_b_im,
           ssm_c_re, ssm_c_im, ssm_d, w_glu, b_glu, w_br_ssm, w_br_att, w_out, norm_ffn_g,
           w_ffn_gate, w_ffn_up, w_ffn_down):
    batch, seq, _ = x_prompt.shape
    n_seq, n_tok, _ = x_sample.shape
    assert w_in.shape[0] == 1, "single-layer trunk"
    l = 0
    n_qkv = D_SSM + 3 * D_ATT

    w_in_b = w_in[l].astype(BF16)
    n_slabs = w_in_b.shape[1] // PROJ_COLS
    n_tail = w_in_b.shape[1] - n_slabs * PROJ_COLS
    assert n_qkv == 2 * PROJ_COLS and n_tail == GATE_SHIFT == N_HEADS
    w_tail = jnp.pad(w_in_b[:, n_slabs * PROJ_COLS:], ((0, 0), (0, LANES - n_tail)))
    b_f_pad = jnp.pad(b_f[l], (0, LANES - N_HEADS)).reshape(1, LANES)
    norm_g = norm_mix_g[l].reshape(1, D_MODEL)
    q_g = q_norm_g[l].reshape(1, HEAD_DIM)
    k_g = k_norm_g[l].reshape(1, HEAD_DIM)
    ns = BLOCK_STATE
    ar_row = ssm_a_re[l].reshape(N_BLOCKS, 1, ns)
    ai_row = ssm_a_im[l].reshape(N_BLOCKS, 1, ns)
    dt_row = jnp.repeat(ssm_log_dt[l], SSM_STATE).reshape(N_BLOCKS, 1, ns)
    bd_bre = _block_diag_lanes(ssm_b_re[l])
    bd_bim = _block_diag_lanes(ssm_b_im[l])
    bd_cre = _block_diag_sublanes(ssm_c_re[l])
    bd_cim = _block_diag_sublanes(ssm_c_im[l])
    d_row = ssm_d[l].reshape(1, D_SSM)
    w_glu_b = w_glu[l].astype(BF16)
    b_glu_r = b_glu[l].reshape(1, D_SSM)
    shift_pad = (GATE_SHIFT, MERGE_COLS - D_MODEL - GATE_SHIFT)
    w_bs = jnp.pad(w_br_ssm[l].astype(BF16), ((0, 0), shift_pad))
    w_ba = jnp.pad(w_br_att[l].astype(BF16), ((0, 0), shift_pad))
    w_o = jnp.pad(w_out[l].astype(BF16), (shift_pad, (0, 0)))
    n_g = norm_ffn_g[l].reshape(1, D_MODEL)
    w_g = w_ffn_gate[l]
    w_u = w_ffn_up[l]
    w_d = w_ffn_down[l]

    vrev, bsum, m_op, bmat, cmat, a1, a16 = _s5gen(ar_row, ai_row, dt_row, bd_bre, bd_bim, bd_cre, bd_cim)

    xp = x_prompt.reshape(batch * seq, D_MODEL)
    rows = n_seq * n_tok
    xs = x_sample.transpose(1, 0, 2).reshape(rows, D_MODEL)
    ((u, qb, k, kb, v, vt, gates, logf),
     (us, qbs, ks, kbs, vs, vbs, gates_s, logf_s)) = _inproj(xp, xs, norm_g, w_in_b, w_tail, b_f_pad, q_g, k_g,
                                                             tm=PROJ_TM)
    caq, cak = _cumaug(logf, batch, seq)
    y_ssm, hre, him = _s5_prompt(u, vrev, bsum, m_op, a16, d_row, batch, seq)
    y_att = _attention(qb, caq, kb, cak, vt, batch, seq)

    ys_ssm, hre_s, him_s = _s5_sample(us, state_ssm_re[l].reshape(n_seq, -1), state_ssm_im[l].reshape(n_seq, -1),
                                      bmat, cmat, a1, d_row, n_seq, n_tok)
    by_seq = lambda a: a.reshape(n_tok, n_seq, -1).transpose(1, 0, 2)
    th = n_tok * N_HEADS
    th_rows = lambda a: by_seq(a).reshape(n_seq, th, HEAD_DIM)
    pad_keys = lambda a: jnp.pad(th_rows(a), ((0, 0), (0, LANES - th), (0, 0)))
    lf_s = by_seq(logf_s[:, :N_HEADS]).reshape(n_seq, th)
    lfrow = jnp.pad(lf_s, ((0, 0), (0, LANES - th))).reshape(n_seq, 1, LANES)
    lfcol = lf_s.reshape(n_seq, th, 1)
    n_pool = cache_k.shape[1]
    ya_s = _decode_attention(page_table, th_rows(qbs), lfrow, lfcol, pad_keys(kbs), pad_keys(vbs),
                             cache_k, cache_v, cache_logf[l].reshape(n_pool, 1, PAGE_KEYS), n_tok)
    ya_s = ya_s.reshape(n_seq, n_tok, D_ATT).transpose(1, 0, 2).reshape(rows, D_ATT).astype(BF16)

    x1, xn1, x1s, xn1s = _merge((y_ssm, y_att, gates, xp), (ys_ssm, ya_s, gates_s, xs),
                                w_glu_b, b_glu_r, w_bs, w_ba, w_o, n_g, tm=MERGE_TM)
    y_p, y_s = _ffn(x1, xn1, x1s, xn1s, w_g, w_u, w_d, tm=FFN_TM)
    y_p = y_p.reshape(batch, seq, D_MODEL)
    y_s = y_s.reshape(n_tok, n_seq, D_MODEL).transpose(1, 0, 2)

    heads = lambda a, b_, t_: a.reshape(1, b_, t_, N_HEADS, HEAD_DIM)
    tok_major = lambda a: a.reshape(n_tok, n_seq, -1).transpose(1, 0, 2)
    return (
        y_p, y_s,
        heads(k, batch, seq), heads(v, batch, seq),
        logf[:, :N_HEADS].reshape(1, batch, seq, N_HEADS),
        hre.reshape(1, batch, N_GROUPS, SSM_STATE), him.reshape(1, batch, N_GROUPS, SSM_STATE),
        heads(tok_major(ks), n_seq, n_tok), heads(tok_major(vs), n_seq, n_tok),
        tok_major(logf_s[:, :N_HEADS]).reshape(1, n_seq, n_tok, N_HEADS),
        hre_s.reshape(1, n_seq, N_GROUPS, SSM_STATE), him_s.reshape(1, n_seq, N_GROUPS, SSM_STATE),
    )
```
